```python
import functools
import jax, jax.numpy as jnp
from jax import lax
import numpy as np

D_MODEL = 1024
BATCH = 16
SEQ = 2048
DEPTH = 1
DEC_BATCH = 32
DEC_SEQ = 1
PAST_LEN = 16384
PAGE_SIZE = 128

ATT_HEAD_DIM = 64
ATT_WIDTH = D_MODEL // 2
ATT_HEADS = ATT_WIDTH // ATT_HEAD_DIM
ATT_KV_HEADS = 2
ATT_GROUP = ATT_HEADS // ATT_KV_HEADS
KV_WIDTH = ATT_KV_HEADS * ATT_HEAD_DIM
ATT_SCALE = ATT_HEAD_DIM ** -0.5
IDX_HEADS = 8
IDX_DIM = 64
IDX_W_SCALE = (IDX_HEADS ** -0.5) * (IDX_DIM ** -0.5)
TOPK_MAX = 256
Q_BLOCK = 128
REC_DIM = 128
REC_WIDTH = D_MODEL - ATT_WIDTH
REC_HEADS = REC_WIDTH // REC_DIM
CHUNK = 64
D_FF = 4 * D_MODEL
MIX_WIDTH = ATT_WIDTH + REC_WIDTH
EPS = 1e-6
IN_SPLITS = (ATT_WIDTH, KV_WIDTH, KV_WIDTH, IDX_HEADS * IDX_DIM, IDX_DIM, IDX_HEADS,
             REC_WIDTH, REC_WIDTH, REC_WIDTH, REC_WIDTH)
IN_WIDTH = sum(IN_SPLITS)
IN_OFFSETS = [int(o) for o in np.cumsum(IN_SPLITS)[:-1]]

kernel_name = "hymba_dsa_hgrn2_decode_step"


def rms_norm(x, g):
    xf = x.astype(jnp.float32)
    y = xf * lax.rsqrt(jnp.mean(xf * xf, axis=-1, keepdims=True) + EPS)
    return (y * g.astype(jnp.float32)).astype(x.dtype)


def gather_rows(t, idx):
    return jax.vmap(lambda a, i: a[i])(t, idx)


def indexer_scores(iq, iw, ik):
    dots = jnp.einsum('bthd,bsd->btsh', iq.astype(jnp.float32), ik.astype(jnp.float32))
    return jnp.einsum('btsh,bth->bts', jax.nn.relu(dots), iw.astype(jnp.float32))


def sparse_attend(q, k_sel, v_sel, valid):
    B, T = q.shape[:2]
    qg = q.reshape(B, T, ATT_KV_HEADS, ATT_GROUP, ATT_HEAD_DIM)
    logits = jnp.einsum('btkgd,btskd->btkgs', qg, k_sel).astype(jnp.float32) * ATT_SCALE
    logits = jnp.where(valid[:, :, None, None, :], logits, -jnp.inf)
    p = jax.nn.softmax(logits, axis=-1).astype(v_sel.dtype)
    o = jnp.einsum('btkgs,btskd->btkgd', p, v_sel)
    return o.reshape(B, T, ATT_WIDTH)


def prompt_attention(aq, ak, av, iq, ik, iw):
    B, L = aq.shape[:2]
    n_sel = min(TOPK_MAX, L // 4)
    n_blk = L // Q_BLOCK

    def to_blocks(t):
        return jnp.moveaxis(t.reshape(B, n_blk, Q_BLOCK, *t.shape[2:]), 1, 0)

    qpos = jnp.arange(L, dtype=jnp.int32).reshape(n_blk, Q_BLOCK)
    kpos = jnp.arange(L, dtype=jnp.int32)

    def block(args):
        q_b, iq_b, iw_b, qp = args
        scores = indexer_scores(iq_b, iw_b, ik)
        causal = kpos[None, :] <= qp[:, None]
        scores = jnp.where(causal[None], scores, -jnp.inf)
        _, idx = lax.top_k(scores, n_sel)
        valid = idx <= qp[None, :, None]
        return sparse_attend(q_b, gather_rows(ak, idx), gather_rows(av, idx), valid)

    out = lax.map(block, (to_blocks(aq), to_blocks(iq), to_blocks(iw), qpos))
    return jnp.moveaxis(out, 0, 1).reshape(B, L, ATT_WIDTH)


def sample_attention(aq, ak, av, iq, ik, iw, cache_k, cache_v, cache_kidx, page_table):
    Bd, T = aq.shape[:2]
    past = page_table.shape[1] * PAGE_SIZE
    L = past + T
    n_sel = min(TOPK_MAX, L // 4)
    ik_past = cache_kidx[page_table].reshape(Bd, past, IDX_DIM)
    ik_all = jnp.concatenate([ik_past.astype(jnp.float32), ik.astype(jnp.float32)], axis=1)
    qpos = past + jnp.arange(T, dtype=jnp.int32)
    kpos = jnp.arange(L, dtype=jnp.int32)
    scores = indexer_scores(iq, iw, ik_all)
    scores = jnp.where((kpos[None, :] <= qpos[:, None])[None], scores, -jnp.inf)
    _, idx = lax.top_k(scores, n_sel)
    in_past = idx < past
    pidx = jnp.minimum(idx, past - 1)
    phys = jax.vmap(lambda pt, i: pt[i])(page_table, pidx // PAGE_SIZE)
    off = pidx % PAGE_SIZE
    nidx = jnp.clip(idx - past, 0, T - 1)
    sel = in_past[..., None, None]
    k_sel = jnp.where(sel, cache_k[phys, off], gather_rows(ak, nidx))
    v_sel = jnp.where(sel, cache_v[phys, off], gather_rows(av, nidx))
    valid = idx <= qpos[None, :, None]
    return sparse_attend(aq, k_sel, v_sel, valid)


def hgrn2_chunked(q, log_f, k, v, s0):
    B, L = q.shape[:2]
    C = CHUNK if L % CHUNK == 0 else L
    n = L // C

    def chunks(t):
        return jnp.moveaxis(t.reshape(B, n, C, *t.shape[2:]), 1, 0)

    causal = jnp.tril(jnp.ones((C, C), dtype=bool))[None, :, :, None, None]

    def step(S, inp):
        qc, gc, kc, vc = inp
        b = jnp.cumsum(gc, axis=1)
        o_inter = jnp.einsum('bchk,bhkv->bchv', qc * jnp.exp(b), S)
        diff = b[:, :, None] - b[:, None, :]
        decay = jnp.exp(jnp.where(causal, diff, -jnp.inf))
        A = jnp.einsum('bthk,bshk,btshk->bhts', qc, kc, decay)
        o_intra = jnp.einsum('bhts,bshv->bthv', A, vc)
        b_last = b[:, -1]
        S_new = jnp.exp(b_last)[..., None] * S + jnp.einsum(
            'bshk,bshv->bhkv', kc * jnp.exp(b_last[:, None] - b), vc)
        return S_new, o_inter + o_intra

    S_fin, o = lax.scan(step, s0, (chunks(q), chunks(log_f), chunks(k), chunks(v)))
    return jnp.moveaxis(o, 0, 1).reshape(B, L, *o.shape[3:]), S_fin


def hgrn2_mixer(rq, rf, ri, rg, lb, rec_norm_g, s0):
    B, L = rq.shape[:2]
    shp = (B, L, REC_HEADS, REC_DIM)
    f32 = jnp.float32
    q = (jax.nn.silu(rq.astype(f32)) * (REC_DIM ** -0.5)).reshape(shp)
    xf = rf.astype(f32).reshape(shp)
    lbh = lb.reshape(REC_HEADS, REC_DIM)
    log_f = jnp.logaddexp(jnp.log(lbh), jnp.log1p(-lbh) + jax.nn.log_sigmoid(xf))
    k = (1.0 - lbh) * jax.nn.sigmoid(-xf)
    v = ri.astype(f32).reshape(shp)
    o, S = hgrn2_chunked(q, log_f, k, v, s0.astype(f32))
    o = rms_norm(o, rec_norm_g).reshape(B, L, REC_WIDTH) * jax.nn.silu(rg.astype(f32))
    return o.astype(rg.dtype), S


def trunk_layer(x, s0, lb, attend, norm1_g, w_in, q_norm_g, k_norm_g, idx_k_norm_g,
                rec_norm_g, w_out, norm2_g, w_up, w_down):
    B, L, _ = x.shape
    h = rms_norm(x, norm1_g)
    aq, ak, av, iq, ik, iw, rq, rf, ri, rg = jnp.split(h @ w_in, IN_OFFSETS, axis=-1)
    aq = rms_norm(aq.reshape(B, L, ATT_HEADS, ATT_HEAD_DIM), q_norm_g)
    ak = rms_norm(ak.reshape(B, L, ATT_KV_HEADS, ATT_HEAD_DIM), k_norm_g)
    av = av.reshape(B, L, ATT_KV_HEADS, ATT_HEAD_DIM)
    iq = iq.reshape(B, L, IDX_HEADS, IDX_DIM)
    ik = rms_norm(ik, idx_k_norm_g)
    iw = iw * IDX_W_SCALE
    att = attend(aq, ak, av, iq, ik, iw)
    rec, S_new = hgrn2_mixer(rq, rf, ri, rg, lb, rec_norm_g, s0)
    x = x + jnp.concatenate([att, rec.astype(att.dtype)], axis=-1) @ w_out
    h2 = rms_norm(x, norm2_g)
    x = x + jnp.square(jax.nn.relu(h2 @ w_up)) @ w_down
    return x, ak, av, ik, S_new


def setup_inputs(seed: int = 0) -> dict:
    key = jax.random.key(seed)
    ks = jax.random.split(key, 20)
    f32 = jnp.float32
    n_pages = PAST_LEN // PAGE_SIZE
    n_used = DEC_BATCH * n_pages
    n_pool = n_used + max(1, n_used // 4)

    def nrm(k, shape, scale=1.0):
        return jax.random.normal(k, shape, f32) * scale

    def gain(k, shape):
        return 1.0 + 0.05 * jax.random.normal(k, shape, f32)

    page_table = jax.random.permutation(ks[6], n_pool)[:n_used].reshape(DEC_BATCH, n_pages).astype(jnp.int32)
    return {
        "x_prompt": nrm(ks[0], (BATCH, SEQ, D_MODEL)),
        "x_sample": nrm(ks[1], (DEC_BATCH, DEC_SEQ, D_MODEL)),
        "cache_k": nrm(ks[2], (DEPTH, n_pool, PAGE_SIZE, ATT_KV_HEADS, ATT_HEAD_DIM)),
        "cache_v": nrm(ks[3], (DEPTH, n_pool, PAGE_SIZE, ATT_KV_HEADS, ATT_HEAD_DIM)),
        "cache_kidx": nrm(ks[4], (DEPTH, n_pool, PAGE_SIZE, IDX_DIM)),
        "state_hgrn": nrm(ks[5], (DEPTH, DEC_BATCH, REC_HEADS, REC_DIM, REC_DIM), 0.5),
        "page_table": page_table,
        "norm1_g": gain(ks[7], (DEPTH, D_MODEL)),
        "w_in": nrm(ks[8], (DEPTH, D_MODEL, IN_WIDTH), D_MODEL ** -0.5),
        "q_norm_g": gain(ks[9], (DEPTH, ATT_HEAD_DIM)),
        "k_norm_g": gain(ks[10], (DEPTH, ATT_HEAD_DIM)),
        "idx_k_norm_g": gain(ks[11], (DEPTH, IDX_DIM)),
        "lower_bounds": nrm(ks[12], (DEPTH + 1, REC_WIDTH), 0.1),
        "rec_norm_g": gain(ks[13], (DEPTH, REC_DIM)),
        "w_out": nrm(ks[14], (DEPTH, MIX_WIDTH, D_MODEL), MIX_WIDTH ** -0.5),
        "norm2_g": gain(ks[15], (DEPTH, D_MODEL)),
        "w_up": nrm(ks[16], (DEPTH, D_MODEL, D_FF), D_MODEL ** -0.5),
        "w_down": nrm(ks[17], (DEPTH, D_FF, D_MODEL), D_FF ** -0.5),
    }


def reference(x_prompt, x_sample, cache_k, cache_v, cache_kidx, state_hgrn, page_table,
              norm1_g, w_in, q_norm_g, k_norm_g, idx_k_norm_g, lower_bounds, rec_norm_g,
              w_out, norm2_g, w_up, w_down):
    lb_all = jnp.cumsum(jax.nn.softmax(lower_bounds.astype(jnp.float32), axis=0), axis=0)
    s0_prompt = jnp.zeros((x_prompt.shape[0], REC_HEADS, REC_DIM, REC_DIM), jnp.float32)
    yp, ys = x_prompt, x_sample
    kp_l, vp_l, ip_l, sp_l, ks_l, vs_l, is_l, ss_l = [], [], [], [], [], [], [], []
    for l in range(DEPTH):
        params = (norm1_g[l], w_in[l], q_norm_g[l], k_norm_g[l], idx_k_norm_g[l], rec_norm_g[l],
                  w_out[l], norm2_g[l], w_up[l], w_down[l])
        yp, kp, vp, ip, sp = trunk_layer(yp, s0_prompt, lb_all[l], prompt_attention, *params)
        attend_sample = functools.partial(sample_attention, cache_k=cache_k[l], cache_v=cache_v[l],
                                          cache_kidx=cache_kidx[l], page_table=page_table)
        ys, k_s, v_s, i_s, s_s = trunk_layer(ys, state_hgrn[l], lb_all[l], attend_sample, *params)
        kp_l.append(kp); vp_l.append(vp); ip_l.append(ip); sp_l.append(sp)
        ks_l.append(k_s); vs_l.append(v_s); is_l.append(i_s); ss_l.append(s_s)
    new_k_prompt = jnp.stack(kp_l)
    new_v_prompt = jnp.stack(vp_l)
    new_kidx_prompt = jnp.stack(ip_l)
    new_state_prompt = jnp.stack(sp_l)
    new_k_sample = jnp.stack(ks_l)
    new_v_sample = jnp.stack(vs_l)
    new_kidx_sample = jnp.stack(is_l)
    new_state_sample = jnp.stack(ss_l)
    return (yp, ys, new_k_prompt, new_v_prompt, new_kidx_prompt, new_state_prompt,
            new_k_sample, new_v_sample, new_kidx_sample, new_state_sample)
```

```python
import functools

import jax
import jax.numpy as jnp
from jax import lax
from jax.experimental import pallas as pl
from jax.experimental.pallas import tpu as pltpu

F32 = jnp.float32
BF16 = jnp.bfloat16
I32 = jnp.int32

EPS = 1e-6
MASKED = -1e30
INT_MIN = -(2 ** 31)
INT_MAX = 2 ** 31 - 1

ATT_HEAD_DIM = 64
ATT_KV_HEADS = 2
IDX_HEADS = 8
IDX_DIM = 64
REC_DIM = 128
TOPK_MAX = 256
CHUNK = 64

V7X_LANES = 128
V7X_VMEM_LIMIT_BYTES = 56 * 1024 * 1024
Q_TILE = 128
KEY_TILE = 256


def _dot(a, b):
    return jnp.dot(a, b, preferred_element_type=F32)


def _dot_nt(a, b):
    return lax.dot_general(a, b, (((1,), (1,)), ((), ())), preferred_element_type=F32)


def _dot_tn(a, b):
    return lax.dot_general(a, b, (((0,), (0,)), ((), ())), preferred_element_type=F32)


def _params(*semantics):
    return pltpu.CompilerParams(dimension_semantics=semantics, vmem_limit_bytes=V7X_VMEM_LIMIT_BYTES)


def _resident(shape):
    nd = len(shape)
    return pl.BlockSpec(shape, lambda *_: (0,) * nd)


def _rms(x, axis):
    return x * lax.rsqrt(jnp.mean(x * x, axis=axis, keepdims=True) + EPS)


def _sigmoid_pair(x):
    t = jnp.exp(-jnp.abs(x))
    r = 1.0 / (1.0 + t)
    tr = t * r
    pos = x >= 0
    return jnp.where(pos, r, tr), jnp.where(pos, tr, r)


def _silu(x):
    s, _ = _sigmoid_pair(x)
    return x * s


def _order_key(score):
    bits = lax.bitcast_convert_type(score, I32)
    return bits ^ ((bits >> 31) & INT_MAX)


def _lower_bound(lbs_ref, layer):
    lbs = lbs_ref[...]
    e = jnp.exp(lbs - jnp.max(lbs, axis=0, keepdims=True))
    p = e / jnp.sum(e, axis=0, keepdims=True)
    return jnp.sum(p[0:layer + 1, :], axis=0, keepdims=True)


def _head_rms_lanes(x, gain_row):
    sq = x * x
    lane = lax.broadcasted_iota(I32, x.shape, 1)
    lo = lane < ATT_HEAD_DIM
    s_lo = jnp.sum(jnp.where(lo, sq, 0.0), axis=-1, keepdims=True)
    s_hi = jnp.sum(jnp.where(lo, 0.0, sq), axis=-1, keepdims=True)
    inv = jnp.where(lo, lax.rsqrt(s_lo / ATT_HEAD_DIM + EPS), lax.rsqrt(s_hi / ATT_HEAD_DIM + EPS))
    return x * inv * gain_row


def _inproj_prompt_kernel(x_ref, g1_ref, wm_ref, wt_ref, gk_ref, gi_ref, gq_ref,
                          akn_ref, av_ref, ikn_ref, akb_ref, ikb_ref, rec4_ref,
                          aqt_ref, iqt_ref, avt_ref, iwt_ref, *, idx_w_scale):
    x = x_ref[...]
    hb = (_rms(x, -1) * g1_ref[...]).astype(BF16)
    y = _dot(hb, wm_ref[...])
    akn = _head_rms_lanes(y[:, 0:128], gk_ref[...])
    ikn = _rms(y[:, 256:256 + IDX_DIM], -1) * gi_ref[...]
    akn_ref[...] = akn
    av_ref[...] = y[:, 128:256]
    ikn_ref[...] = ikn
    akb_ref[...] = akn.astype(BF16)
    ikb_ref[...] = ikn.astype(BF16)
    rec4_ref[...] = y[:, 384:]

    yt = _dot_nt(wt_ref[...], hb)
    n_q = aqt_ref.shape[0] // ATT_HEAD_DIM
    for h in range(n_q):
        rows = slice(h * ATT_HEAD_DIM, (h + 1) * ATT_HEAD_DIM)
        aqt_ref[rows, :] = (_rms(yt[rows, :], 0) * gq_ref[rows, :]).astype(BF16)
    o = aqt_ref.shape[0]
    iqt_ref[...] = yt[o:o + iqt_ref.shape[0], :].astype(BF16)
    o += iqt_ref.shape[0]
    avt_ref[...] = yt[o:o + 128, :].astype(BF16)
    o += 128
    iwt_ref[...] = yt[o:o + IDX_HEADS, :] * idx_w_scale


def _inproj_sample_kernel(x_ref, g1_ref, wm_ref, wq_ref, gk_ref, gi_ref,
                          akn_ref, av_ref, ikn_ref, rec4_ref, aq_ref, iq_ref, iw_ref, *, idx_w_scale):
    x = x_ref[...]
    hb = (_rms(x, -1) * g1_ref[...]).astype(BF16)
    y = _dot(hb, wm_ref[...])
    akn_ref[...] = _head_rms_lanes(y[:, 0:128], gk_ref[...])
    av_ref[...] = y[:, 128:256]
    ikn_ref[...] = _rms(y[:, 256:256 + IDX_DIM], -1) * gi_ref[...]
    rec4_ref[...] = y[:, 384:]
    yq = _dot(hb, wq_ref[...])
    wa = aq_ref.shape[1]
    wi = iq_ref.shape[1]
    aq_ref[...] = yq[:, 0:wa]
    iq_ref[...] = yq[:, wa:wa + wi]
    iw_ref[...] = yq[:, wa + wi:wa + wi + 128] * idx_w_scale


def _hgrn_gates(rq, rf, ri, rg, lb):
    sp, sn = _sigmoid_pair(rf)
    q = _silu(rq) * (REC_DIM ** -0.5)
    log_f = jnp.log(lb + (1.0 - lb) * sp)
    k = (1.0 - lb) * sn
    return q, log_f, k, ri, _silu(rg)


def _hgrn_prompt_kernel(lbs_ref, rec4_ref, gn_ref, rec_ref, sfin_ref, st_ref, *, layer, rec_width):
    n_heads = rec_width // REC_DIM
    tb = rec4_ref.shape[0]

    @pl.when(pl.program_id(1) == 0)
    def _():
        st_ref[...] = jnp.zeros_like(st_ref)

    lb = _lower_bound(lbs_ref, layer)
    w = rec_width
    q, log_f, k, v, gate = _hgrn_gates(rec4_ref[:, 0:w], rec4_ref[:, w:2 * w],
                                       rec4_ref[:, 2 * w:3 * w], rec4_ref[:, 3 * w:4 * w], lb)
    row = lax.broadcasted_iota(I32, (CHUNK, CHUNK), 0)
    col = lax.broadcasted_iota(I32, (CHUNK, CHUNK), 1)
    causal = row >= col
    tri = jnp.where(causal, 1.0, 0.0).astype(BF16)
    gn = gn_ref[...]

    for c in range(tb // CHUNK):
        rows = slice(c * CHUNK, (c + 1) * CHUNK)
        gc = log_f[rows, :]
        g1 = gc.astype(BF16)
        r1 = gc - g1.astype(F32)
        g2 = r1.astype(BF16)
        g3 = (r1 - g2.astype(F32)).astype(BF16)
        b = _dot(tri, g1) + _dot(tri, g2) + _dot(tri, g3)
        b_last = b[CHUNK - 1:CHUNK, :]
        b_mid = b[CHUNK // 2 - 1:CHUNK // 2, :]
        qc, kc, vc = q[rows, :], k[rows, :], v[rows, :]
        q_in = (qc * jnp.exp(b - b_mid)).astype(BF16)
        k_in = (kc * jnp.exp(b_mid - b)).astype(BF16)
        q_st = (qc * jnp.exp(b)).astype(BF16)
        k_st = (kc * jnp.exp(b_last - b)).astype(BF16)
        vb = vc.astype(BF16)
        decay = jnp.exp(b_last)
        for h in range(n_heads):
            cols = slice(h * REC_DIM, (h + 1) * REC_DIM)
            a = jnp.where(causal, _dot_nt(q_in[:, cols], k_in[:, cols]), 0.0)
            st = st_ref[h]
            o = _dot(a.astype(BF16), vb[:, cols]) + _dot_nt(q_st[:, cols], st.astype(BF16))
            st_ref[h] = st * decay[:, cols] + _dot_tn(vb[:, cols], k_st[:, cols])
            o = _rms(o, -1) * gn * gate[rows, cols]
            rec_ref[rows, cols] = o.astype(rec_ref.dtype)

    @pl.when(pl.program_id(1) == pl.num_programs(1) - 1)
    def _():
        for h in range(n_heads):
            sfin_ref[0, h] = st_ref[h].T


def _row_to_col(x_row):
    n = x_row.shape[1]
    eye = lax.broadcasted_iota(I32, (n, n), 0) == lax.broadcasted_iota(I32, (n, n), 1)
    return jnp.sum(jnp.where(eye, jnp.broadcast_to(x_row, (n, n)), 0.0), axis=1, keepdims=True)


def _hgrn_step_kernel(lbs_ref, rec4_ref, gn_ref, s0_ref, rec_ref, s1_ref, *, layer, rec_width):
    n_heads = rec_width // REC_DIM
    lb = _lower_bound(lbs_ref, layer)
    w = rec_width
    r4 = rec4_ref[0]
    q, log_f, k, v, gate = _hgrn_gates(r4[:, 0:w], r4[:, w:2 * w], r4[:, 2 * w:3 * w], r4[:, 3 * w:4 * w], lb)
    f = jnp.exp(log_f)
    gn = gn_ref[...]
    for h in range(n_heads):
        cols = slice(h * REC_DIM, (h + 1) * REC_DIM)
        s1 = _row_to_col(f[:, cols]) * s0_ref[0, h] + _row_to_col(k[:, cols]) * v[:, cols]
        s1_ref[0, h] = s1
        o = jnp.sum(_row_to_col(q[:, cols]) * s1, axis=0, keepdims=True)
        rec_ref[0, :, cols] = (_rms(o, -1) * gn * gate[:, cols]).astype(rec_ref.dtype)


def _prompt_attention_kernel(iqt_ref, aqt_ref, iwt_ref, ikb_ref, akb_ref, avt_ref, o_ref,
                             keys_ref, bias_ref, *, n_sel):
    j = pl.program_id(1)
    n_kt = (j * Q_TILE + Q_TILE + KEY_TILE - 1) // KEY_TILE
    q_pos = j * Q_TILE + lax.broadcasted_iota(I32, (KEY_TILE, Q_TILE), 1)
    k_off = lax.broadcasted_iota(I32, (KEY_TILE, Q_TILE), 0)

    def key_rows(t):
        return pl.ds(pl.multiple_of(t * KEY_TILE, KEY_TILE), KEY_TILE)

    iq_all = jnp.concatenate([iqt_ref[h * IDX_DIM:(h + 1) * IDX_DIM, :] for h in range(IDX_HEADS)], axis=1)
    iw = iwt_ref[...]

    def score_tile(t, carry):
        d = _dot(ikb_ref[key_rows(t), :], iq_all)
        sc = jnp.zeros((KEY_TILE, Q_TILE), F32)
        for h in range(IDX_HEADS):
            sc = sc + jnp.maximum(d[:, h * Q_TILE:(h + 1) * Q_TILE], 0.0) * iw[h:h + 1, :]
        k_pos = t * KEY_TILE + k_off
        keys_ref[key_rows(t), :] = jnp.where(k_pos <= q_pos, _order_key(sc), INT_MIN)
        return carry

    lax.fori_loop(0, n_kt, score_tile, 0)

    def count(pred):
        def body(t, acc):
            m = pred(keys_ref[key_rows(t), :], t * KEY_TILE + k_off).astype(I32)
            return acc + jnp.sum(m.reshape(KEY_TILE // 8, 8, Q_TILE), axis=0)
        acc = lax.fori_loop(0, n_kt, body, jnp.zeros((8, Q_TILE), I32))
        return jnp.sum(acc, axis=0, keepdims=True)

    def bit_step(i, prefix):
        cand = prefix | jnp.left_shift(jnp.int32(1), 31 - i)
        cnt = count(lambda kk, _: kk >= (cand ^ INT_MIN))
        return jnp.where(cnt >= n_sel, cand, prefix)

    n_bits = jnp.where(j * Q_TILE + Q_TILE > n_sel, 32, 0)
    prefix = lax.fori_loop(0, n_bits, bit_step, jnp.zeros((1, Q_TILE), I32))
    thr = jnp.maximum(prefix ^ INT_MIN, INT_MIN + 1)

    need = n_sel - count(lambda kk, _: kk > thr)
    n_tied = count(lambda kk, _: kk == thr)
    any_tie = jnp.max(jnp.where(n_tied > need, 1, 0)) > 0
    pos_bits = max(1, (keys_ref.shape[0] - 1).bit_length())

    def last_kept_tie():
        def pos_step(i, pre):
            cand = pre | jnp.left_shift(jnp.int32(1), pos_bits - 1 - i)
            cnt = count(lambda kk, pos: (kk == thr) & (pos < cand))
            return jnp.where(cnt < need, cand, pre)
        return lax.fori_loop(0, pos_bits, pos_step, jnp.zeros((1, Q_TILE), I32))

    last_tie = lax.cond(any_tie, last_kept_tie, lambda: jnp.full((1, Q_TILE), INT_MAX, I32))

    def bias_tile(t, carry):
        kk = keys_ref[key_rows(t), :]
        keep = (kk > thr) | ((kk == thr) & (t * KEY_TILE + k_off <= last_tie))
        bias_ref[key_rows(t), :] = jnp.where(keep, 0.0, MASKED)
        return carry

    lax.fori_loop(0, n_kt, bias_tile, 0)

    n_heads = aqt_ref.shape[0] // ATT_HEAD_DIM
    group = n_heads // ATT_KV_HEADS
    gw = group * Q_TILE
    for g in range(ATT_KV_HEADS):
        q_g = jnp.concatenate([aqt_ref[(g * group + hh) * ATT_HEAD_DIM:(g * group + hh + 1) * ATT_HEAD_DIM, :]
                               for hh in range(group)], axis=1)
        zeros = jnp.zeros_like(q_g)
        q_pad = jnp.concatenate([q_g, zeros] if g == 0 else [zeros, q_g], axis=0)

        def attend_tile(t, carry, q_pad=q_pad, g=g):
            m, lsum, acc = carry
            logit = _dot(akb_ref[key_rows(t), :], q_pad)
            bias = bias_ref[key_rows(t), :]
            logit = logit + jnp.concatenate([bias] * group, axis=1)
            m_new = jnp.maximum(m, jnp.max(logit, axis=0, keepdims=True))
            alpha = jnp.exp(m - m_new)
            p = jnp.exp(logit - m_new)
            lsum = alpha * lsum + jnp.sum(p.reshape(KEY_TILE // 8, 8, gw), axis=0)
            v_t = avt_ref[t, g * ATT_HEAD_DIM:(g + 1) * ATT_HEAD_DIM, :]
            acc = alpha * acc + _dot(v_t, p.astype(BF16))
            return m_new, lsum, acc

        init = (jnp.full((1, gw), MASKED, F32), jnp.zeros((8, gw), F32), jnp.zeros((ATT_HEAD_DIM, gw), F32))
        _, lsum, acc = lax.fori_loop(0, n_kt, attend_tile, init)
        out_t = acc / jnp.sum(lsum, axis=0, keepdims=True)
        for pair in range(group // 2):
            blk = jnp.concatenate([out_t[:, (2 * pair) * Q_TILE:(2 * pair + 1) * Q_TILE],
                                   out_t[:, (2 * pair + 1) * Q_TILE:(2 * pair + 2) * Q_TILE]], axis=0)
            c0 = (g * group + 2 * pair) * ATT_HEAD_DIM
            o_ref[:, c0:c0 + 2 * ATT_HEAD_DIM] = blk.T.astype(o_ref.dtype)


def _page_copies(pt_ref, src_hbm, buf, sem, seq, slot, n_pages, page):
    def copy(p):
        return pltpu.make_async_copy(src_hbm.at[pt_ref[seq, p]],
                                     buf.at[slot, pl.ds(pl.multiple_of(p * page, page), page)], sem.at[slot])
    return copy


def _start_pages(copy, n_pages):
    def body(p, c):
        copy(p).start()
        return c
    lax.fori_loop(0, n_pages, body, 0)


def _wait_pages(copy, n_pages):
    def body(p, c):
        copy(p).wait()
        return c
    lax.fori_loop(0, n_pages, body, 0)


def _sample_scores_kernel(pt_ref, iq_ref, iw_ref, ikn_ref, kidx_hbm, sc_ref, buf, sem, *, n_pages, page, tile):
    b = pl.program_id(0)
    nb = pl.num_programs(0)
    slot = b % 2

    @pl.when(b == 0)
    def _():
        _start_pages(_page_copies(pt_ref, kidx_hbm, buf, sem, 0, 0, n_pages, page), n_pages)

    @pl.when(b + 1 < nb)
    def _():
        _start_pages(_page_copies(pt_ref, kidx_hbm, buf, sem, b + 1, 1 - slot, n_pages, page), n_pages)

    _wait_pages(_page_copies(pt_ref, kidx_hbm, buf, sem, b, slot, n_pages, page), n_pages)

    iq = iq_ref[0]
    iqb = iq.astype(BF16)
    iw = iw_ref[0]
    past = n_pages * page
    for t in range(past // tile):
        kb = buf[slot, t * tile:(t + 1) * tile, :].astype(BF16)
        d = _dot_nt(iqb, kb)
        sc_ref[0, :, t * tile:(t + 1) * tile] = jnp.sum(jnp.maximum(d, 0.0) * iw, axis=0, keepdims=True)
    d_new = jnp.sum(iq * ikn_ref[0], axis=1, keepdims=True)
    s_new = jnp.sum(jnp.maximum(d_new, 0.0) * iw, axis=0, keepdims=True)
    lane = lax.broadcasted_iota(I32, (1, V7X_LANES), 1)
    sc_ref[0, :, past:past + V7X_LANES] = jnp.where(lane == 0, s_new, 0.0)


def _sample_select_kernel(sc_ref, bias_ref, keys_ref, *, n_sel, past):
    rows, width = sc_ref.shape
    col = lax.broadcasted_iota(I32, (rows, width), 1)
    keys_ref[...] = jnp.where(col <= past, _order_key(sc_ref[...]), INT_MIN)

    def count(pred):
        return jnp.sum(pred(keys_ref[...]).astype(I32), axis=1, keepdims=True)

    def bit_step(i, prefix):
        cand = prefix | jnp.left_shift(jnp.int32(1), 31 - i)
        cnt = count(lambda kk: kk >= (cand ^ INT_MIN))
        return jnp.where(cnt >= n_sel, cand, prefix)

    prefix = lax.fori_loop(0, 32, bit_step, jnp.zeros((rows, 1), I32))
    thr = jnp.maximum(prefix ^ INT_MIN, INT_MIN + 1)
    need = n_sel - count(lambda kk: kk > thr)
    pos_bits = max(1, (width - 1).bit_length())

    def pos_step(i, pre):
        cand = pre | jnp.left_shift(jnp.int32(1), pos_bits - 1 - i)
        cnt = count(lambda kk: (kk == thr) & (col < cand))
        return jnp.where(cnt < need, cand, pre)

    last_tie = lax.fori_loop(0, pos_bits, pos_step, jnp.zeros((rows, 1), I32))
    kk = keys_ref[...]
    keep = (kk > thr) | ((kk == thr) & (col <= last_tie))
    bias_ref[...] = jnp.where(keep, 0.0, MASKED)


def _sample_attention_kernel(pt_ref, aq_ref, gq_ref, akn_ref, av_ref, bias_ref, k_hbm, v_hbm, o_ref,
                             kbuf, vbuf, logit_ref, ksem, vsem, *, n_pages, page, tile):
    b = pl.program_id(0)
    nb = pl.num_programs(0)
    slot = b % 2

    def start(seq, s):
        _start_pages(_page_copies(pt_ref, k_hbm, kbuf, ksem, seq, s, n_pages, page), n_pages)
        _start_pages(_page_copies(pt_ref, v_hbm, vbuf, vsem, seq, s, n_pages, page), n_pages)

    @pl.when(b == 0)
    def _():
        start(0, 0)

    @pl.when(b + 1 < nb)
    def _():
        start(b + 1, 1 - slot)

    _wait_pages(_page_copies(pt_ref, k_hbm, kbuf, ksem, b, slot, n_pages, page), n_pages)
    _wait_pages(_page_copies(pt_ref, v_hbm, vbuf, vsem, b, slot, n_pages, page), n_pages)

    n_heads = aq_ref.shape[1]
    group = n_heads // ATT_KV_HEADS
    past = n_pages * page
    q = _rms(aq_ref[0], -1) * gq_ref[...]
    zeros = jnp.zeros_like(q)
    head = lax.broadcasted_iota(I32, (n_heads, 2 * ATT_HEAD_DIM), 0)
    lo_group = head < group
    q_pad = jnp.where(lo_group, jnp.concatenate([q, zeros], axis=1), jnp.concatenate([zeros, q], axis=1))
    q_pad_b = q_pad.astype(BF16)

    for t in range(past // tile):
        cols = slice(t * tile, (t + 1) * tile)
        kb = kbuf[slot, cols, :].astype(BF16)
        logit_ref[:, cols] = _dot_nt(q_pad_b, kb) + bias_ref[0, :, cols]
    lane = lax.broadcasted_iota(I32, (n_heads, V7X_LANES), 1)
    l_new = jnp.sum(q_pad * akn_ref[0], axis=1, keepdims=True) + bias_ref[0, :, past:past + 1]
    logit_ref[:, past:past + V7X_LANES] = jnp.where(lane == 0, l_new, MASKED)

    m = jnp.max(logit_ref[...], axis=1, keepdims=True)
    lsum = jnp.zeros((n_heads, 1), F32)
    acc = jnp.zeros((n_heads, 2 * ATT_HEAD_DIM), F32)
    for t in range(past // tile):
        cols = slice(t * tile, (t + 1) * tile)
        p = jnp.exp(logit_ref[:, cols] - m)
        lsum = lsum + jnp.sum(p, axis=1, keepdims=True)
        acc = acc + _dot(p.astype(BF16), vbuf[slot, cols, :].astype(BF16))
    p_new = jnp.exp(logit_ref[:, past:past + 1] - m)
    out = (acc + p_new * av_ref[0]) / (lsum + p_new)
    o_ref[0] = jnp.where(lo_group[:, 0:ATT_HEAD_DIM], out[:, 0:ATT_HEAD_DIM], out[:, ATT_HEAD_DIM:])


def _outproj_mlp_kernel(x_ref, att_ref, rec_ref, wo_ref, g2_ref, wup_ref, wdn_ref, o_ref, *, ff_tile):
    mix = jnp.concatenate([att_ref[...].astype(BF16), rec_ref[...].astype(BF16)], axis=1)
    y = x_ref[...] + _dot(mix, wo_ref[...])
    h2 = (_rms(y, -1) * g2_ref[...]).astype(BF16)
    acc = y
    for c in range(wup_ref.shape[1] // ff_tile):
        u = jnp.maximum(_dot(h2, wup_ref[:, c * ff_tile:(c + 1) * ff_tile]), 0.0)
        acc = acc + _dot((u * u).astype(BF16), wdn_ref[c * ff_tile:(c + 1) * ff_tile, :])
    o_ref[...] = acc


def _row_tile(n_rows, want):
    return want if n_rows % want == 0 else n_rows


def _layer_weights(w_in, att_width, kv_width, idx_width, rec_width):
    o = [0]
    for wdt in (att_width, kv_width, kv_width, idx_width, IDX_DIM, IDX_HEADS, rec_width, rec_width, rec_width, rec_width):
        o.append(o[-1] + wdt)
    aq, ak, av, iq, ik, iw, rq, rf, ri, rg = (w_in[:, o[i]:o[i + 1]] for i in range(10))
    d = w_in.shape[0]
    z = lambda n: jnp.zeros((d, n), w_in.dtype)
    w_main = jnp.concatenate([ak, av, ik, z(128 - IDX_DIM), rq, rf, ri, rg], axis=1).astype(BF16)
    w_tok = jnp.concatenate([aq, iq, av, iw, z(16 - IDX_HEADS)], axis=1).T.astype(BF16)
    w_qry = jnp.concatenate([aq, iq, iw, z(128 - IDX_HEADS)], axis=1).astype(BF16)
    return w_main, w_tok, w_qry


def _outproj_mlp(x, att, rec, w_out, g2, w_up, w_dn, name):
    n, d = x.shape
    tm = _row_tile(n, 512)
    ff = w_up.shape[1]
    return pl.pallas_call(
        functools.partial(_outproj_mlp_kernel, ff_tile=min(ff, 1024)),
        grid=(n // tm,),
        in_specs=[pl.BlockSpec((tm, d), lambda i: (i, 0)),
                  pl.BlockSpec((tm, att.shape[1]), lambda i: (i, 0)),
                  pl.BlockSpec((tm, rec.shape[1]), lambda i: (i, 0)),
                  _resident(w_out.shape), _resident(g2.shape), _resident(w_up.shape), _resident(w_dn.shape)],
        out_specs=pl.BlockSpec((tm, d), lambda i: (i, 0)),
        out_shape=jax.ShapeDtypeStruct((n, d), F32),
        compiler_params=_params("arbitrary"),
        name=name,
    )(x, att, rec, w_out, g2, w_up, w_dn)


def _prompt_layer(x, lbs, layer, w_main, w_tok, g1, gq_col, gk_row, gi_row, gn_row, w_out, g2, w_up, w_dn,
                  att_width, idx_width, rec_width, idx_w_scale):
    bsz, seq, d = x.shape
    n = bsz * seq
    xf = x.reshape(n, d)
    tm = _row_tile(n, 512)
    kvw = ATT_KV_HEADS * ATT_HEAD_DIM
    row = lambda w: pl.BlockSpec((tm, w), lambda i: (i, 0))
    colb = lambda h: pl.BlockSpec((h, tm), lambda i: (0, i))
    akn, av, ikn, akb, ikb, rec4, aqt, iqt, avt, iwt = pl.pallas_call(
        functools.partial(_inproj_prompt_kernel, idx_w_scale=idx_w_scale),
        grid=(n // tm,),
        in_specs=[row(d), _resident(g1.shape), _resident(w_main.shape), _resident(w_tok.shape),
                  _resident(gk_row.shape), _resident(gi_row.shape), _resident(gq_col.shape)],
        out_specs=[row(kvw), row(kvw), row(IDX_DIM), row(kvw), row(IDX_DIM), row(4 * rec_width),
                   colb(att_width), colb(idx_width), colb(kvw), colb(IDX_HEADS)],
        out_shape=[jax.ShapeDtypeStruct((n, kvw), F32), jax.ShapeDtypeStruct((n, kvw), F32),
                   jax.ShapeDtypeStruct((n, IDX_DIM), F32), jax.ShapeDtypeStruct((n, kvw), BF16),
                   jax.ShapeDtypeStruct((n, IDX_DIM), BF16), jax.ShapeDtypeStruct((n, 4 * rec_width), F32),
                   jax.ShapeDtypeStruct((att_width, n), BF16), jax.ShapeDtypeStruct((idx_width, n), BF16),
                   jax.ShapeDtypeStruct((kvw, n), BF16), jax.ShapeDtypeStruct((IDX_HEADS, n), F32)],
        compiler_params=_params("arbitrary"),
        name="inproj_prompt",
    )(xf, g1, w_main, w_tok, gk_row, gi_row, gq_col)

    tb = _row_tile(seq, 256)
    n_rec = rec_width // REC_DIM
    tiles = seq // tb
    rec, s_fin = pl.pallas_call(
        functools.partial(_hgrn_prompt_kernel, layer=layer, rec_width=rec_width),
        grid=(bsz, tiles),
        in_specs=[_resident(lbs.shape), pl.BlockSpec((tb, 4 * rec_width), lambda b, t: (b * tiles + t, 0)),
                  _resident(gn_row.shape)],
        out_specs=[pl.BlockSpec((tb, rec_width), lambda b, t: (b * tiles + t, 0)),
                   pl.BlockSpec((1, n_rec, REC_DIM, REC_DIM), lambda b, t: (b, 0, 0, 0))],
        out_shape=[jax.ShapeDtypeStruct((n, rec_width), BF16),
                   jax.ShapeDtypeStruct((bsz, n_rec, REC_DIM, REC_DIM), F32)],
        scratch_shapes=[pltpu.VMEM((n_rec, REC_DIM, REC_DIM), F32)],
        compiler_params=_params("arbitrary", "arbitrary"),
        name="hgrn_prompt",
    )(lbs, rec4, gn_row)

    n_sel = min(TOPK_MAX, seq // 4)
    nq = seq // Q_TILE
    kt_per_seq = seq // KEY_TILE
    avt_tiles = avt.reshape(kvw, n // KEY_TILE, KEY_TILE).transpose(1, 0, 2)
    qblk = lambda h: pl.BlockSpec((h, Q_TILE), lambda b, j: (0, b * nq + j))
    att = pl.pallas_call(
        functools.partial(_prompt_attention_kernel, n_sel=n_sel),
        grid=(bsz, nq),
        in_specs=[qblk(idx_width), qblk(att_width), qblk(IDX_HEADS),
                  pl.BlockSpec((seq, IDX_DIM), lambda b, j: (b, 0)),
                  pl.BlockSpec((seq, kvw), lambda b, j: (b, 0)),
                  pl.BlockSpec((kt_per_seq, kvw, KEY_TILE), lambda b, j: (b, 0, 0))],
        out_specs=pl.BlockSpec((Q_TILE, att_width), lambda b, j: (b * nq + j, 0)),
        out_shape=jax.ShapeDtypeStruct((n, att_width), BF16),
        scratch_shapes=[pltpu.VMEM((seq, Q_TILE), I32), pltpu.VMEM((seq, Q_TILE), F32)],
        compiler_params=_params("arbitrary", "arbitrary"),
        name="attention_prompt",
    )(iqt, aqt, iwt, ikb, akb, avt_tiles)

    y = _outproj_mlp(xf, att, rec, w_out, g2, w_up, w_dn, "outproj_mlp_prompt")
    return (y.reshape(bsz, seq, d), akn.reshape(bsz, seq, ATT_KV_HEADS, ATT_HEAD_DIM),
            av.reshape(bsz, seq, ATT_KV_HEADS, ATT_HEAD_DIM), ikn.reshape(bsz, seq, IDX_DIM), s_fin)


def _sample_layer(x, s0, lbs, layer, cache_k, cache_v, cache_kidx, page_table, w_main, w_qry, g1, gq_row,
                  gk_row, gi_row, gn_row, w_out, g2, w_up, w_dn, att_width, idx_width, rec_width, idx_w_scale):
    bd, t_new, d = x.shape
    assert t_new == 1, "the sample path handles one new token per sequence"
    kvw = ATT_KV_HEADS * ATT_HEAD_DIM
    n_heads = att_width // ATT_HEAD_DIM
    n_rec = rec_width // REC_DIM
    n_pool, page = cache_k.shape[0], cache_k.shape[1]
    n_pages = page_table.shape[1]
    past = n_pages * page
    width = past + V7X_LANES
    tile = 2048 if past % 2048 == 0 else past
    xf = x.reshape(bd, d)

    full = lambda shape: _resident(shape)
    akn, av, ikn, rec4, aq, iq, iw = pl.pallas_call(
        functools.partial(_inproj_sample_kernel, idx_w_scale=idx_w_scale),
        grid=(1,),
        in_specs=[full(xf.shape), full(g1.shape), full(w_main.shape), full(w_qry.shape),
                  full(gk_row.shape), full(gi_row.shape)],
        out_specs=[full((bd, kvw)), full((bd, kvw)), full((bd, IDX_DIM)), full((bd, 4 * rec_width)),
                   full((bd, att_width)), full((bd, idx_width)), full((bd, 128))],
        out_shape=[jax.ShapeDtypeStruct((bd, kvw), F32), jax.ShapeDtypeStruct((bd, kvw), F32),
                   jax.ShapeDtypeStruct((bd, IDX_DIM), F32), jax.ShapeDtypeStruct((bd, 4 * rec_width), F32),
                   jax.ShapeDtypeStruct((bd, att_width), F32), jax.ShapeDtypeStruct((bd, idx_width), F32),
                   jax.ShapeDtypeStruct((bd, 128), F32)],
        compiler_params=_params("arbitrary"),
        name="inproj_sample",
    )(xf, g1, w_main, w_qry, gk_row, gi_row)

    rec, s1 = pl.pallas_call(
        functools.partial(_hgrn_step_kernel, layer=layer, rec_width=rec_width),
        grid=(bd,),
        in_specs=[_resident(lbs.shape), pl.BlockSpec((1, 1, 4 * rec_width), lambda b: (b, 0, 0)),
                  _resident(gn_row.shape), pl.BlockSpec((1, n_rec, REC_DIM, REC_DIM), lambda b: (b, 0, 0, 0))],
        out_specs=[pl.BlockSpec((1, 1, rec_width), lambda b: (b, 0, 0)),
                   pl.BlockSpec((1, n_rec, REC_DIM, REC_DIM), lambda b: (b, 0, 0, 0))],
        out_shape=[jax.ShapeDtypeStruct((bd, 1, rec_width), BF16),
                   jax.ShapeDtypeStruct((bd, n_rec, REC_DIM, REC_DIM), F32)],
        compiler_params=_params("arbitrary"),
        name="hgrn_step",
    )(lbs, rec4.reshape(bd, 1, 4 * rec_width), gn_row, s0)

    iq3 = iq.reshape(bd, IDX_HEADS, IDX_DIM)
    iw3 = iw[:, 0:IDX_HEADS].reshape(bd, IDX_HEADS, 1)
    ikn3 = ikn.reshape(bd, 1, IDX_DIM)
    scores = pl.pallas_call(
        functools.partial(_sample_scores_kernel, n_pages=n_pages, page=page, tile=tile),
        grid_spec=pltpu.PrefetchScalarGridSpec(
            num_scalar_prefetch=1,
            grid=(bd,),
            in_specs=[pl.BlockSpec((1, IDX_HEADS, IDX_DIM), lambda b, pt: (b, 0, 0)),
                      pl.BlockSpec((1, IDX_HEADS, 1), lambda b, pt: (b, 0, 0)),
                      pl.BlockSpec((1, 1, IDX_DIM), lambda b, pt: (b, 0, 0)),
                      pl.BlockSpec(memory_space=pl.ANY)],
            out_specs=pl.BlockSpec((1, 1, width), lambda b, pt: (b, 0, 0)),
            scratch_shapes=[pltpu.VMEM((2, past, IDX_DIM), F32), pltpu.SemaphoreType.DMA((2,))]),
        out_shape=jax.ShapeDtypeStruct((bd, 1, width), F32),
        compiler_params=_params("arbitrary"),
        name="sample_scores",
    )(page_table, iq3, iw3, ikn3, cache_kidx)

    n_sel = min(TOPK_MAX, (past + 1) // 4)
    bias = pl.pallas_call(
        functools.partial(_sample_select_kernel, n_sel=n_sel, past=past),
        grid=(1,),
        in_specs=[full((bd, width))],
        out_specs=full((bd, width)),
        out_shape=jax.ShapeDtypeStruct((bd, width), F32),
        scratch_shapes=[pltpu.VMEM((bd, width), I32)],
        compiler_params=_params("arbitrary"),
        name="sample_select",
    )(scores.reshape(bd, width))

    att = pl.pallas_call(
        functools.partial(_sample_attention_kernel, n_pages=n_pages, page=page, tile=tile),
        grid_spec=pltpu.PrefetchScalarGridSpec(
            num_scalar_prefetch=1,
            grid=(bd,),
            in_specs=[pl.BlockSpec((1, n_heads, ATT_HEAD_DIM), lambda b, pt: (b, 0, 0)),
                      pl.BlockSpec((1, ATT_HEAD_DIM), lambda b, pt: (0, 0)),
                      pl.BlockSpec((1, 1, kvw), lambda b, pt: (b, 0, 0)),
                      pl.BlockSpec((1, 1, kvw), lambda b, pt: (b, 0, 0)),
                      pl.BlockSpec((1, 1, width), lambda b, pt: (b, 0, 0)),
                      pl.BlockSpec(memory_space=pl.ANY), pl.BlockSpec(memory_space=pl.ANY)],
            out_specs=pl.BlockSpec((1, n_heads, ATT_HEAD_DIM), lambda b, pt: (b, 0, 0)),
            scratch_shapes=[pltpu.VMEM((2, past, kvw), F32), pltpu.VMEM((2, past, kvw), F32),
                            pltpu.VMEM((n_heads, width), F32),
                            pltpu.SemaphoreType.DMA((2,)), pltpu.SemaphoreType.DMA((2,))]),
        out_shape=jax.ShapeDtypeStruct((bd, n_heads, ATT_HEAD_DIM), F32),
        compiler_params=_params("arbitrary"),
        name="sample_attention",
    )(page_table, aq.reshape(bd, n_heads, ATT_HEAD_DIM), gq_row, akn.reshape(bd, 1, kvw), av.reshape(bd, 1, kvw),
      bias.reshape(bd, 1, width), cache_k.reshape(n_pool, page, kvw), cache_v.reshape(n_pool, page, kvw))

    y = _outproj_mlp(xf, att.reshape(bd, att_width), rec.reshape(bd, rec_width), w_out, g2, w_up, w_dn,
                     "outproj_mlp_sample")
    return (y.reshape(bd, 1, d), akn.reshape(bd, 1, ATT_KV_HEADS, ATT_HEAD_DIM),
            av.reshape(bd, 1, ATT_KV_HEADS, ATT_HEAD_DIM), ikn.reshape(bd, 1, IDX_DIM), s1)


def kernel(x_prompt, x_sample, cache_k, cache_v, cache_kidx, state_hgrn, page_table, norm1_g, w_in, q_norm_g,
           k_norm_g, idx_k_norm_g, lower_bounds, rec_norm_g, w_out, norm2_g, w_up, w_down):
    depth = w_in.shape[0]
    d_model = x_prompt.shape[-1]
    kv_width = ATT_KV_HEADS * ATT_HEAD_DIM
    idx_width = IDX_HEADS * IDX_DIM
    rec_width = rec_norm_g.shape[-1] * (lower_bounds.shape[-1] // rec_norm_g.shape[-1])
    att_width = w_out.shape[1] - rec_width
    assert w_in.shape[-1] == att_width + 2 * kv_width + idx_width + IDX_DIM + IDX_HEADS + 4 * rec_width
    assert rec_norm_g.shape[-1] == REC_DIM and q_norm_g.shape[-1] == ATT_HEAD_DIM
    idx_w_scale = (IDX_HEADS ** -0.5) * (IDX_DIM ** -0.5)
    att_scale = ATT_HEAD_DIM ** -0.5
    n_heads = att_width // ATT_HEAD_DIM
    lbs = lower_bounds.astype(F32)

    yp, ys = x_prompt, x_sample
    outs = [[] for _ in range(8)]
    for l in range(depth):
        w_main, w_tok, w_qry = _layer_weights(w_in[l], att_width, kv_width, idx_width, rec_width)
        g1 = norm1_g[l].reshape(1, d_model)
        g2 = norm2_g[l].reshape(1, d_model)
        gq_row = (q_norm_g[l] * att_scale).reshape(1, ATT_HEAD_DIM)
        gq_col = jnp.tile(q_norm_g[l] * att_scale, n_heads).reshape(att_width, 1)
        gk_row = jnp.tile(k_norm_g[l], ATT_KV_HEADS).reshape(1, kv_width)
        gi_row = idx_k_norm_g[l].reshape(1, IDX_DIM)
        gn_row = rec_norm_g[l].reshape(1, REC_DIM)
        wo, wu, wd = w_out[l].astype(BF16), w_up[l].astype(BF16), w_down[l].astype(BF16)
        yp, kp, vp, ip, sp = _prompt_layer(yp, lbs, l, w_main, w_tok, g1, gq_col, gk_row, gi_row, gn_row, wo, g2,
                                           wu, wd, att_width, idx_width, rec_width, idx_w_scale)
        ys, k_s, v_s, i_s, s_s = _sample_layer(ys, state_hgrn[l], lbs, l, cache_k[l], cache_v[l], cache_kidx[l],
                                               page_table, w_main, w_qry, g1, gq_row, gk_row, gi_row, gn_row, wo,
                                               g2, wu, wd, att_width, idx_width, rec_width, idx_w_scale)
        for acc, val in zip(outs, (kp, vp, ip, sp, k_s, v_s, i_s, s_s)):
            acc.append(val)
    return (yp, ys) + tuple(jnp.stack(o) for o in outs)
```

```python
import functools

import jax
import jax.numpy as jnp
from jax import lax
from jax.experimental import pallas as pl
from jax.experimental.pallas import tpu as pltpu

F32 = jnp.float32
BF16 = jnp.bfloat16
I32 = jnp.int32

EPS = 1e-6
MASKED = -1e30
INT_MIN = -(2 ** 31)
INT_MAX = 2 ** 31 - 1

ATT_HEAD_DIM = 64
ATT_KV_HEADS = 2
IDX_HEADS = 8
IDX_DIM = 64
REC_DIM = 128
TOPK_MAX = 256
CHUNK = 64

V7X_LANES = 128
V7X_VMEM_LIMIT_BYTES = 56 * 1024 * 1024
Q_TILE = 128
KEY_TILE = 256


def _dot(a, b):
    return jnp.dot(a, b, preferred_element_type=F32)


def _dot_nt(a, b):
    return lax.dot_general(a, b, (((1,), (1,)), ((), ())), preferred_element_type=F32)


def _dot_tn(a, b):
    return lax.dot_general(a, b, (((0,), (0,)), ((), ())), preferred_element_type=F32)


def _params(*semantics):
    return pltpu.CompilerParams(dimension_semantics=semantics, vmem_limit_bytes=V7X_VMEM_LIMIT_BYTES)


def _resident(shape):
    nd = len(shape)
    return pl.BlockSpec(shape, lambda *_: (0,) * nd)


def _rms(x, axis):
    return x * lax.rsqrt(jnp.mean(x * x, axis=axis, keepdims=True) + EPS)


def _sigmoid_pair(x):
    t = jnp.exp(-jnp.abs(x))
    r = 1.0 / (1.0 + t)
    tr = t * r
    pos = x >= 0
    return jnp.where(pos, r, tr), jnp.where(pos, tr, r)


def _silu(x):
    s, _ = _sigmoid_pair(x)
    return x * s


def _order_key(score):
    bits = lax.bitcast_convert_type(score, I32)
    return bits ^ ((bits >> 31) & INT_MAX)


def _lower_bound(lbs_ref, layer):
    lbs = lbs_ref[...]
    e = jnp.exp(lbs - jnp.max(lbs, axis=0, keepdims=True))
    p = e / jnp.sum(e, axis=0, keepdims=True)
    return jnp.sum(p[0:layer + 1, :], axis=0, keepdims=True)


def _head_rms_lanes(x, gain_row):
    sq = x * x
    lane = lax.broadcasted_iota(I32, x.shape, 1)
    lo = lane < ATT_HEAD_DIM
    s_lo = jnp.sum(jnp.where(lo, sq, 0.0), axis=-1, keepdims=True)
    s_hi = jnp.sum(jnp.where(lo, 0.0, sq), axis=-1, keepdims=True)
    inv = jnp.where(lo, lax.rsqrt(s_lo / ATT_HEAD_DIM + EPS), lax.rsqrt(s_hi / ATT_HEAD_DIM + EPS))
    return x * inv * gain_row


def _inproj_prompt_kernel(x_ref, g1_ref, wm_ref, wt_ref, gk_ref, gi_ref, gq_ref, gkc_ref, gic_ref,
                          akb_ref, ikb_ref, rec4_ref, aqt_ref, iqt_ref, avb_ref, iwt_ref, akt_ref, avt_ref, ikt_ref,
                          *, idx_w_scale):
    x = x_ref[...]
    hb = (_rms(x, -1) * g1_ref[...]).astype(BF16)
    y = _dot(hb, wm_ref[...])
    akb_ref[...] = _head_rms_lanes(y[:, 0:128], gk_ref[...]).astype(BF16)
    ikb_ref[...] = (_rms(y[:, 128:128 + IDX_DIM], -1) * gi_ref[...]).astype(BF16)
    rec4_ref[...] = y[:, 256:]

    yt = _dot_nt(wt_ref[...], hb)
    n_q = aqt_ref.shape[0] // ATT_HEAD_DIM
    for h in range(n_q):
        rows = slice(h * ATT_HEAD_DIM, (h + 1) * ATT_HEAD_DIM)
        aqt_ref[rows, :] = (_rms(yt[rows, :], 0) * gq_ref[rows, :]).astype(BF16)
    o = aqt_ref.shape[0]
    iqt_ref[...] = yt[o:o + iqt_ref.shape[0], :].astype(BF16)
    o += iqt_ref.shape[0]
    kvw = ATT_KV_HEADS * ATT_HEAD_DIM
    av_t = yt[o:o + kvw, :]
    avt_ref[0] = av_t
    avb_ref[...] = av_t.astype(BF16)
    o += kvw
    for g in range(ATT_KV_HEADS):
        rows = slice(g * ATT_HEAD_DIM, (g + 1) * ATT_HEAD_DIM)
        akt_ref[0, rows, :] = _rms(yt[o + g * ATT_HEAD_DIM:o + (g + 1) * ATT_HEAD_DIM, :], 0) * gkc_ref[rows, :]
    o += kvw
    ikt_ref[0] = _rms(yt[o:o + IDX_DIM, :], 0) * gic_ref[...]
    o += IDX_DIM
    iwt_ref[...] = yt[o:o + IDX_HEADS, :] * idx_w_scale


def _inproj_sample_kernel(x_ref, g1_ref, wm_ref, wq_ref, gk_ref, gi_ref,
                          akn_ref, av_ref, ikn_ref, rec4_ref, aq_ref, iq_ref, iw_ref, *, idx_w_scale):
    x = x_ref[...]
    hb = (_rms(x, -1) * g1_ref[...]).astype(BF16)
    y = _dot(hb, wm_ref[...])
    akn_ref[...] = _head_rms_lanes(y[:, 0:128], gk_ref[...])
    ikn_ref[...] = _rms(y[:, 128:128 + IDX_DIM], -1) * gi_ref[...]
    rec4_ref[...] = y[:, 256:]
    yq = _dot(hb, wq_ref[...])
    wa = aq_ref.shape[1]
    wi = iq_ref.shape[1]
    aq_ref[...] = yq[:, 0:wa]
    iq_ref[...] = yq[:, wa:wa + wi]
    av_ref[...] = yq[:, wa + wi:wa + wi + 128]
    iw_ref[...] = yq[:, wa + wi + 128:wa + wi + 256] * idx_w_scale


def _hgrn_gates(rq, rf, ri, rg, lb):
    sp, sn = _sigmoid_pair(rf)
    q = _silu(rq) * (REC_DIM ** -0.5)
    log_f = jnp.log(lb + (1.0 - lb) * sp)
    k = (1.0 - lb) * sn
    return q, log_f, k, ri, _silu(rg)


def _hgrn_prompt_kernel(lbs_ref, rec4_ref, gn_ref, rec_ref, sfin_ref, st_ref, *, layer, rec_width):
    n_heads = rec_width // REC_DIM
    tb = rec4_ref.shape[0]

    @pl.when(pl.program_id(1) == 0)
    def _():
        st_ref[...] = jnp.zeros_like(st_ref)

    lb = _lower_bound(lbs_ref, layer)
    w = rec_width
    q, log_f, k, v, gate = _hgrn_gates(rec4_ref[:, 0:w], rec4_ref[:, w:2 * w],
                                       rec4_ref[:, 2 * w:3 * w], rec4_ref[:, 3 * w:4 * w], lb)
    row = lax.broadcasted_iota(I32, (CHUNK, CHUNK), 0)
    col = lax.broadcasted_iota(I32, (CHUNK, CHUNK), 1)
    causal = row >= col
    tri = jnp.where(causal, 1.0, 0.0).astype(BF16)
    gn = gn_ref[...]

    for c in range(tb // CHUNK):
        rows = slice(c * CHUNK, (c + 1) * CHUNK)
        gc = log_f[rows, :]
        g1 = gc.astype(BF16)
        r1 = gc - g1.astype(F32)
        g2 = r1.astype(BF16)
        g3 = (r1 - g2.astype(F32)).astype(BF16)
        b = _dot(tri, g1) + _dot(tri, g2) + _dot(tri, g3)
        b_last = b[CHUNK - 1:CHUNK, :]
        b_mid = b[CHUNK // 2 - 1:CHUNK // 2, :]
        qc, kc, vc = q[rows, :], k[rows, :], v[rows, :]
        q_in = (qc * jnp.exp(b - b_mid)).astype(BF16)
        k_in = (kc * jnp.exp(b_mid - b)).astype(BF16)
        q_st = (qc * jnp.exp(b)).astype(BF16)
        k_st = (kc * jnp.exp(b_last - b)).astype(BF16)
        vb = vc.astype(BF16)
        decay = jnp.exp(b_last)
        for h in range(n_heads):
            cols = slice(h * REC_DIM, (h + 1) * REC_DIM)
            a = jnp.where(causal, _dot_nt(q_in[:, cols], k_in[:, cols]), 0.0)
            st = st_ref[h]
            o = _dot(a.astype(BF16), vb[:, cols]) + _dot_nt(q_st[:, cols], st.astype(BF16))
            st_ref[h] = st * decay[:, cols] + _dot_tn(vb[:, cols], k_st[:, cols])
            o = _rms(o, -1) * gn * gate[rows, cols]
            rec_ref[rows, cols] = o.astype(rec_ref.dtype)

    @pl.when(pl.program_id(1) == pl.num_programs(1) - 1)
    def _():
        for h in range(n_heads):
            sfin_ref[0, h] = st_ref[h].T


def _row_to_col(x_row):
    n = x_row.shape[1]
    eye = lax.broadcasted_iota(I32, (n, n), 0) == lax.broadcasted_iota(I32, (n, n), 1)
    return jnp.sum(jnp.where(eye, jnp.broadcast_to(x_row, (n, n)), 0.0), axis=1, keepdims=True)


def _col_to_row(x_col):
    n = x_col.shape[0]
    eye = lax.broadcasted_iota(I32, (n, n), 0) == lax.broadcasted_iota(I32, (n, n), 1)
    return jnp.sum(jnp.where(eye, jnp.broadcast_to(x_col, (n, n)), 0.0), axis=0, keepdims=True)


def _hgrn_step_kernel(lbs_ref, rec4_ref, gn_ref, s0_ref, rec_ref, s1_ref, *, layer, rec_width):
    n_heads = rec_width // REC_DIM
    lb = _lower_bound(lbs_ref, layer)
    w = rec_width
    r4 = rec4_ref[0]
    q, log_f, k, v, gate = _hgrn_gates(r4[:, 0:w], r4[:, w:2 * w], r4[:, 2 * w:3 * w], r4[:, 3 * w:4 * w], lb)
    f = jnp.exp(log_f)
    gn = gn_ref[...]
    for h in range(n_heads):
        cols = slice(h * REC_DIM, (h + 1) * REC_DIM)
        s1 = _row_to_col(f[:, cols]) * s0_ref[0, h] + _row_to_col(k[:, cols]) * v[:, cols]
        s1_ref[0, h] = s1
        o = jnp.sum(_row_to_col(q[:, cols]) * s1, axis=0, keepdims=True)
        rec_ref[0, :, cols] = (_rms(o, -1) * gn * gate[:, cols]).astype(rec_ref.dtype)


def _prompt_attention_kernel(iqa_ref, iqb_ref, aqa_ref, aqb_ref, iwa_ref, iwb_ref, ikb_ref, akb_ref, avb_ref,
                             oa_ref, ob_ref, iqs_ref, qpad_ref, iws_ref, hi_ref, lo_ref, lo2_ref, bias_ref,
                             logit_ref, acc_ref, *, n_sel, n_units):
    p = pl.program_id(1)
    nq = 2 * pl.num_programs(1)
    j_of = (p, nq - 1 - p)
    n_a = p // 2 + 1
    n_heads = aqa_ref.shape[0] // ATT_HEAD_DIM
    group = n_heads // ATT_KV_HEADS
    gw = group * Q_TILE
    i16 = jnp.int16

    for s, (iq_ref, aq_ref, iw_ref) in enumerate(((iqa_ref, aqa_ref, iwa_ref), (iqb_ref, aqb_ref, iwb_ref))):
        iqs_ref[s] = jnp.concatenate([iq_ref[h * IDX_DIM:(h + 1) * IDX_DIM, :] for h in range(IDX_HEADS)], axis=1)
        halves = [jnp.concatenate([aq_ref[(g * group + hh) * ATT_HEAD_DIM:(g * group + hh + 1) * ATT_HEAD_DIM, :]
                                   for hh in range(group)], axis=1) for g in range(ATT_KV_HEADS)]
        zero = jnp.zeros_like(halves[0])
        qpad_ref[s] = jnp.concatenate([jnp.concatenate([halves[0], zero], axis=1),
                                       jnp.concatenate([zero, halves[1]], axis=1)], axis=0)
        iws_ref[s] = iw_ref[...]

    def unit(u):
        is_b = u >= n_a
        return is_b, jnp.where(is_b, 1, 0), jnp.where(is_b, u - n_a, u), jnp.where(is_b, j_of[1], j_of[0])

    def key_rows(kt):
        return pl.ds(pl.multiple_of(kt * KEY_TILE, KEY_TILE), KEY_TILE)

    row32 = lax.broadcasted_iota(I32, (KEY_TILE, Q_TILE), 0)
    lane32 = lax.broadcasted_iota(I32, (KEY_TILE, Q_TILE), 1)

    for u in range(n_units):
        _, s, kt, j = unit(u)
        d = _dot(ikb_ref[key_rows(kt), :], iqs_ref[s])
        iw = iws_ref[s]
        sc = jnp.zeros((KEY_TILE, Q_TILE), F32)
        for h in range(IDX_HEADS):
            sc = sc + jnp.maximum(d[:, h * Q_TILE:(h + 1) * Q_TILE], 0.0) * iw[h:h + 1, :]
        key = jnp.where(kt * KEY_TILE + row32 <= j * Q_TILE + lane32, _order_key(sc), INT_MIN)
        hi_ref[u] = (key >> 16).astype(i16)
        lo_ref[u] = ((key & 0xFFFF) - 0x8000).astype(i16)

    one16 = jnp.ones((16, Q_TILE), i16)
    zero16 = jnp.zeros((16, Q_TILE), i16)

    def count2(pred):
        tot = [zero16, zero16]
        for u in range(n_units):
            is_b, _, kt, _ = unit(u)
            m = pred(u, lambda a, b_: jnp.where(is_b, b_, a), kt)
            part = zero16
            for r in range(KEY_TILE // 16):
                part = part + jnp.where(m[r * 16:(r + 1) * 16, :], one16, zero16)
            tot[0] = tot[0] + jnp.where(is_b, zero16, part)
            tot[1] = tot[1] + jnp.where(is_b, part, zero16)
        return tuple(jnp.sum(t.astype(I32), axis=0, keepdims=True) for t in tot)

    def radix16(src_ref, want):
        def step(i, pref):
            cand = tuple(pr | jnp.left_shift(jnp.int32(1), 15 - i) for pr in pref)
            c16 = tuple((c - 0x8000).astype(i16) for c in cand)
            cnt = count2(lambda u, pick, kt: src_ref[u] >= pick(c16[0], c16[1]))
            return tuple(jnp.where(cnt[t] >= want[t], cand[t], pref[t]) for t in range(2))
        zero = jnp.zeros((1, Q_TILE), I32)
        return lax.fori_loop(0, 16, step, (zero, zero))

    want = (jnp.full((1, Q_TILE), n_sel, I32),) * 2
    pref_hi = radix16(hi_ref, want)
    thr_hi = tuple((pr - 0x8000).astype(i16) for pr in pref_hi)
    for u in range(n_units):
        is_b, _, _, _ = unit(u)
        lo2_ref[u] = jnp.where(hi_ref[u] == jnp.where(is_b, thr_hi[1], thr_hi[0]), lo_ref[u], i16(-0x8000))
    above_hi = count2(lambda u, pick, kt: hi_ref[u] > pick(thr_hi[0], thr_hi[1]))
    pref_lo = radix16(lo2_ref, tuple(n_sel - a for a in above_hi))
    pref_lo = tuple(jnp.where((pref_hi[t] == 0) & (pref_lo[t] == 0), 1, pref_lo[t]) for t in range(2))
    thr_lo = tuple((pr - 0x8000).astype(i16) for pr in pref_lo)

    def is_thr(u, pick):
        return (hi_ref[u] == pick(thr_hi[0], thr_hi[1])) & (lo_ref[u] == pick(thr_lo[0], thr_lo[1]))

    above_lo = count2(lambda u, pick, kt: lo2_ref[u] > pick(thr_lo[0], thr_lo[1]))
    need = tuple(n_sel - above_hi[t] - above_lo[t] for t in range(2))
    tied = count2(lambda u, pick, kt: is_thr(u, pick))
    any_tie = jnp.max(jnp.maximum(jnp.where(tied[0] > need[0], 1, 0), jnp.where(tied[1] > need[1], 1, 0))) > 0
    pos_bits = max(1, (ikb_ref.shape[0] - 1).bit_length())
    row16 = lax.broadcasted_iota(I32, (KEY_TILE, Q_TILE), 0).astype(i16)

    def rel16(pos, kt):
        return jnp.clip(pos - kt * KEY_TILE, -1, KEY_TILE).astype(i16)

    def last_kept_tie():
        def step(i, pre):
            cand = tuple(pr | jnp.left_shift(jnp.int32(1), pos_bits - 1 - i) for pr in pre)
            cnt = count2(lambda u, pick, kt: is_thr(u, pick) & (row16 < rel16(pick(cand[0], cand[1]), kt)))
            return tuple(jnp.where(cnt[t] < need[t], cand[t], pre[t]) for t in range(2))
        zero = jnp.zeros((1, Q_TILE), I32)
        return lax.fori_loop(0, pos_bits, step, (zero, zero))

    big = jnp.full((1, Q_TILE), INT_MAX, I32)
    last_tie = lax.cond(any_tie, last_kept_tie, lambda: (big, big))

    for u in range(n_units):
        is_b, _, kt, _ = unit(u)
        pick = lambda a, b_: jnp.where(is_b, b_, a)
        keep = ((hi_ref[u] > pick(thr_hi[0], thr_hi[1])) | (lo2_ref[u] > pick(thr_lo[0], thr_lo[1]))
                | (is_thr(u, pick) & (row16 <= rel16(pick(last_tie[0], last_tie[1]), kt))))
        keep32 = jnp.where(keep, i16(1), i16(0)).astype(I32)
        bias_ref[u] = jnp.where(keep32 != 0, 0.0, MASKED)

    neg = jnp.full((8, 2 * gw), MASKED, F32)
    cmax = [neg, neg]
    for u in range(n_units):
        is_b, s, kt, _ = unit(u)
        logit = _dot(akb_ref[key_rows(kt), :], qpad_ref[s])
        logit = logit + jnp.concatenate([bias_ref[u]] * (2 * group), axis=1)
        logit_ref[u] = logit
        part = jnp.max(logit.reshape(KEY_TILE // 8, 8, 2 * gw), axis=0)
        cmax[0] = jnp.maximum(cmax[0], jnp.where(is_b, neg, part))
        cmax[1] = jnp.maximum(cmax[1], jnp.where(is_b, part, neg))
    cmax = [jnp.max(c, axis=0, keepdims=True) for c in cmax]

    acc_ref[...] = jnp.zeros_like(acc_ref)
    zsum = jnp.zeros((8, 2 * gw), F32)
    lsum = [zsum, zsum]
    for u in range(n_units):
        is_b, s, kt, _ = unit(u)
        pr = jnp.exp(logit_ref[u] - jnp.where(is_b, cmax[1], cmax[0]))
        part = jnp.sum(pr.reshape(KEY_TILE // 8, 8, 2 * gw), axis=0)
        lsum[0] = lsum[0] + jnp.where(is_b, zsum, part)
        lsum[1] = lsum[1] + jnp.where(is_b, part, zsum)
        pb = pr.astype(BF16)
        for g in range(ATT_KV_HEADS):
            v_t = avb_ref[kt, g * ATT_HEAD_DIM:(g + 1) * ATT_HEAD_DIM, :]
            acc_ref[s, g] = acc_ref[s, g] + _dot(v_t, pb[:, g * gw:(g + 1) * gw])

    for s, o_ref in enumerate((oa_ref, ob_ref)):
        den = jnp.sum(lsum[s], axis=0, keepdims=True)
        for g in range(ATT_KV_HEADS):
            out_t = acc_ref[s, g] / den[:, g * gw:(g + 1) * gw]
            for pair in range(group // 2):
                blk = jnp.concatenate([out_t[:, (2 * pair) * Q_TILE:(2 * pair + 1) * Q_TILE],
                                       out_t[:, (2 * pair + 1) * Q_TILE:(2 * pair + 2) * Q_TILE]], axis=0)
                c0 = (g * group + 2 * pair) * ATT_HEAD_DIM
                o_ref[0, 0, :, c0:c0 + 2 * ATT_HEAD_DIM] = blk.T.astype(o_ref.dtype)


def _page_copy(pt_ref, src_hbm, layer, buf, sem, seq, slot, page):
    def copy(p):
        dst = buf.at[(slot,) + (slice(None),) * (len(buf.shape) - 2) + (pl.ds(pl.multiple_of(p * page, page), page),)]
        return pltpu.make_async_copy(src_hbm.at[layer, pt_ref[seq, p]], dst, sem.at[slot])
    return copy


def _start_pages(copy, n_pages):
    def body(p, c):
        copy(p).start()
        return c
    lax.fori_loop(0, n_pages, body, 0)


def _wait_pages(copy, n_pages):
    def body(p, c):
        copy(p).wait()
        return c
    lax.fori_loop(0, n_pages, body, 0)


def _sample_scores_kernel(pt_ref, iq_ref, iw_ref, ikn_ref, kidx_hbm, sc_ref, buf, sem, *, layer, n_pages, page, tile):
    b = pl.program_id(0)
    nb = pl.num_programs(0)
    slot = b % 2

    @pl.when(b == 0)
    def _():
        _start_pages(_page_copy(pt_ref, kidx_hbm, layer, buf, sem, 0, 0, page), n_pages)

    @pl.when(b + 1 < nb)
    def _():
        _start_pages(_page_copy(pt_ref, kidx_hbm, layer, buf, sem, b + 1, 1 - slot, page), n_pages)

    _wait_pages(_page_copy(pt_ref, kidx_hbm, layer, buf, sem, b, slot, page), n_pages)

    iq = iq_ref[0]
    iqb = iq.astype(BF16)
    iw = iw_ref[0]
    past = n_pages * page
    for t in range(past // tile):
        kb = buf[slot, :, t * tile:(t + 1) * tile].astype(BF16)
        d = _dot(iqb, kb)
        sc_ref[0, :, t * tile:(t + 1) * tile] = jnp.sum(jnp.maximum(d, 0.0) * iw, axis=0, keepdims=True)
    d_new = jnp.sum(iq * ikn_ref[0], axis=1, keepdims=True)
    s_new = jnp.sum(jnp.maximum(d_new, 0.0) * iw, axis=0, keepdims=True)
    lane = lax.broadcasted_iota(I32, (1, V7X_LANES), 1)
    sc_ref[0, :, past:past + V7X_LANES] = jnp.where(lane == 0, s_new, 0.0)


def _sample_select_kernel(sc_ref, bias_ref, keys_ref, *, n_sel, past):
    rows, width = sc_ref.shape
    col = lax.broadcasted_iota(I32, (rows, width), 1)
    keys_ref[...] = jnp.where(col <= past, _order_key(sc_ref[...]), INT_MIN)

    def count(pred):
        return jnp.sum(pred(keys_ref[...]).astype(I32), axis=1, keepdims=True)

    def bit_step(i, prefix):
        cand = prefix | jnp.left_shift(jnp.int32(1), 31 - i)
        cnt = count(lambda kk: kk >= (cand ^ INT_MIN))
        return jnp.where(cnt >= n_sel, cand, prefix)

    prefix = lax.fori_loop(0, 32, bit_step, jnp.zeros((rows, 1), I32))
    thr = jnp.maximum(prefix ^ INT_MIN, INT_MIN + 1)
    need = n_sel - count(lambda kk: kk > thr)
    pos_bits = max(1, (width - 1).bit_length())

    def pos_step(i, pre):
        cand = pre | jnp.left_shift(jnp.int32(1), pos_bits - 1 - i)
        cnt = count(lambda kk: (kk == thr) & (col < cand))
        return jnp.where(cnt < need, cand, pre)

    last_tie = lax.fori_loop(0, pos_bits, pos_step, jnp.zeros((rows, 1), I32))
    kk = keys_ref[...]
    keep = (kk > thr) | ((kk == thr) & (col <= last_tie))
    bias_ref[...] = jnp.where(keep, 0.0, MASKED)


def _sample_attention_kernel(pt_ref, aq_ref, gq_ref, akn_ref, av_ref, bias_ref, k_hbm, v_hbm, o_ref,
                             kbuf, vbuf, logit_ref, ksem, vsem, *, layer, n_pages, page, tile):
    b = pl.program_id(0)
    nb = pl.num_programs(0)
    slot = b % 2

    def start(seq, s):
        _start_pages(_page_copy(pt_ref, k_hbm, layer, kbuf, ksem, seq, s, page), n_pages)
        _start_pages(_page_copy(pt_ref, v_hbm, layer, vbuf, vsem, seq, s, page), n_pages)

    @pl.when(b == 0)
    def _():
        start(0, 0)

    @pl.when(b + 1 < nb)
    def _():
        start(b + 1, 1 - slot)

    _wait_pages(_page_copy(pt_ref, k_hbm, layer, kbuf, ksem, b, slot, page), n_pages)
    _wait_pages(_page_copy(pt_ref, v_hbm, layer, vbuf, vsem, b, slot, page), n_pages)

    n_heads = aq_ref.shape[1]
    group = n_heads // ATT_KV_HEADS
    kvw = ATT_KV_HEADS * ATT_HEAD_DIM
    past = n_pages * page
    q = _rms(aq_ref[0], -1) * gq_ref[...]
    zeros = jnp.zeros_like(q)
    head = lax.broadcasted_iota(I32, (n_heads, kvw), 0)
    q_pad = jnp.where(head < group, jnp.concatenate([q, zeros], axis=1), jnp.concatenate([zeros, q], axis=1))
    q_pad_b = q_pad.astype(BF16)

    for t in range(past // tile):
        cols = slice(t * tile, (t + 1) * tile)
        kb = kbuf[slot, :, :, cols].reshape(kvw, tile).astype(BF16)
        logit_ref[:, cols] = _dot(q_pad_b, kb) + bias_ref[0, :, cols]
    lane = lax.broadcasted_iota(I32, (n_heads, V7X_LANES), 1)
    l_new = jnp.sum(q_pad * akn_ref[0], axis=1, keepdims=True) + bias_ref[0, :, past:past + 1]
    logit_ref[:, past:past + V7X_LANES] = jnp.where(lane == 0, l_new, MASKED)

    m = jnp.max(logit_ref[...], axis=1, keepdims=True)
    lsum = jnp.zeros((n_heads, 1), F32)
    acc = jnp.zeros((kvw, n_heads), F32)
    for t in range(past // tile):
        cols = slice(t * tile, (t + 1) * tile)
        p = jnp.exp(logit_ref[:, cols] - m)
        lsum = lsum + jnp.sum(p, axis=1, keepdims=True)
        acc = acc + _dot_nt(vbuf[slot, :, :, cols].reshape(kvw, tile).astype(BF16), p.astype(BF16))
    p_new = jnp.exp(logit_ref[:, past:past + 1] - m)
    acc = acc + _row_to_col(av_ref[0]) * _col_to_row(p_new)
    o_ref[0] = acc / _col_to_row(lsum + p_new)


def _outproj_mlp_kernel(x_ref, att_ref, rec_ref, wo_ref, g2_ref, wup_ref, wdn_ref, o_ref, *, ff_tile):
    mix = jnp.concatenate([att_ref[...].astype(BF16), rec_ref[...].astype(BF16)], axis=1)
    y = x_ref[...] + _dot(mix, wo_ref[...])
    h2 = (_rms(y, -1) * g2_ref[...]).astype(BF16)
    acc = y
    for c in range(wup_ref.shape[1] // ff_tile):
        u = jnp.maximum(_dot(h2, wup_ref[:, c * ff_tile:(c + 1) * ff_tile]), 0.0)
        acc = acc + _dot((u * u).astype(BF16), wdn_ref[c * ff_tile:(c + 1) * ff_tile, :])
    o_ref[...] = acc


def _row_tile(n_rows, want):
    return want if n_rows % want == 0 else n_rows


def _layer_weights(w_in, att_width, kv_width, idx_width, rec_width):
    o = [0]
    for wdt in (att_width, kv_width, kv_width, idx_width, IDX_DIM, IDX_HEADS, rec_width, rec_width, rec_width, rec_width):
        o.append(o[-1] + wdt)
    aq, ak, av, iq, ik, iw, rq, rf, ri, rg = (w_in[:, o[i]:o[i + 1]] for i in range(10))
    d = w_in.shape[0]
    z = lambda n: jnp.zeros((d, n), w_in.dtype)
    w_main = jnp.concatenate([ak, ik, z(128 - IDX_DIM), rq, rf, ri, rg], axis=1).astype(BF16)
    w_tok = jnp.concatenate([aq, iq, av, ak, ik, iw, z(16 - IDX_HEADS)], axis=1).T.astype(BF16)
    w_qry = jnp.concatenate([aq, iq, av, iw, z(128 - IDX_HEADS)], axis=1).astype(BF16)
    return w_main, w_tok, w_qry


def _outproj_mlp(x, att, rec, w_out, g2, w_up, w_dn, name):
    n, d = x.shape
    tm = _row_tile(n, 512)
    ff = w_up.shape[1]
    return pl.pallas_call(
        functools.partial(_outproj_mlp_kernel, ff_tile=min(ff, 1024)),
        grid=(n // tm,),
        in_specs=[pl.BlockSpec((tm, d), lambda i: (i, 0)),
                  pl.BlockSpec((tm, att.shape[1]), lambda i: (i, 0)),
                  pl.BlockSpec((tm, rec.shape[1]), lambda i: (i, 0)),
                  _resident(w_out.shape), _resident(g2.shape), _resident(w_up.shape), _resident(w_dn.shape)],
        out_specs=pl.BlockSpec((tm, d), lambda i: (i, 0)),
        out_shape=jax.ShapeDtypeStruct((n, d), F32),
        compiler_params=_params("arbitrary"),
        name=name,
    )(x, att, rec, w_out, g2, w_up, w_dn)


def _prompt_attention(iqt, aqt, iwt, ikb, akb, avb, bsz, seq):
    n = bsz * seq
    kvw = ATT_KV_HEADS * ATT_HEAD_DIM
    att_width, idx_width = aqt.shape[0], iqt.shape[0]
    n_sel = min(TOPK_MAX, seq // 4)
    nq = seq // Q_TILE
    assert KEY_TILE == 2 * Q_TILE and nq % 2 == 0, "query tiles are paired so every step sees nq/2 + 1 key tiles"
    n_pairs = nq // 2
    n_units = n_pairs + 1
    kt_per_seq = seq // KEY_TILE
    avb_tiles = avb.reshape(kvw, n // KEY_TILE, KEY_TILE).transpose(1, 0, 2)
    n_heads = att_width // ATT_HEAD_DIM
    gw2 = n_heads * Q_TILE
    qa = lambda h: pl.BlockSpec((h, Q_TILE), lambda b, p: (0, b * nq + p))
    qb = lambda h: pl.BlockSpec((h, Q_TILE), lambda b, p: (0, b * nq + nq - 1 - p))
    outb = pl.BlockSpec((1, 1, Q_TILE, att_width), lambda b, p: (b, p, 0, 0))
    att_a, att_b = pl.pallas_call(
        functools.partial(_prompt_attention_kernel, n_sel=n_sel, n_units=n_units),
        grid=(bsz, n_pairs),
        in_specs=[qa(idx_width), qb(idx_width), qa(att_width), qb(att_width), qa(IDX_HEADS), qb(IDX_HEADS),
                  pl.BlockSpec((seq, IDX_DIM), lambda b, p: (b, 0)),
                  pl.BlockSpec((seq, kvw), lambda b, p: (b, 0)),
                  pl.BlockSpec((kt_per_seq, kvw, KEY_TILE), lambda b, p: (b, 0, 0))],
        out_specs=[outb, outb],
        out_shape=[jax.ShapeDtypeStruct((bsz, n_pairs, Q_TILE, att_width), BF16)] * 2,
        scratch_shapes=[pltpu.VMEM((2, IDX_DIM, IDX_HEADS * Q_TILE), BF16),
                        pltpu.VMEM((2, kvw, gw2), BF16),
                        pltpu.VMEM((2, IDX_HEADS, Q_TILE), F32),
                        pltpu.VMEM((n_units, KEY_TILE, Q_TILE), jnp.int16),
                        pltpu.VMEM((n_units, KEY_TILE, Q_TILE), jnp.int16),
                        pltpu.VMEM((n_units, KEY_TILE, Q_TILE), jnp.int16),
                        pltpu.VMEM((n_units, KEY_TILE, Q_TILE), F32),
                        pltpu.VMEM((n_units, KEY_TILE, gw2), F32),
                        pltpu.VMEM((2, ATT_KV_HEADS, ATT_HEAD_DIM, gw2 // ATT_KV_HEADS), F32)],
        compiler_params=_params("arbitrary", "arbitrary"),
        name="attention_prompt",
    )(iqt, iqt, aqt, aqt, iwt, iwt, ikb, akb, avb_tiles)
    return jnp.concatenate([att_a, att_b[:, ::-1]], axis=1).reshape(n, att_width)


def _prompt_layer(x, lbs, layer, w_main, w_tok, g1, gq_col, gk_row, gi_row, gk_col, gi_col, gn_row, w_out, g2,
                  w_up, w_dn, att_width, idx_width, rec_width, idx_w_scale):
    bsz, seq, d = x.shape
    n = bsz * seq
    xf = x.reshape(n, d)
    tm = _row_tile(seq, 512)
    tps = seq // tm
    kvw = ATT_KV_HEADS * ATT_HEAD_DIM
    row = lambda w: pl.BlockSpec((tm, w), lambda i: (i, 0))
    colb = lambda h: pl.BlockSpec((h, tm), lambda i: (0, i))
    seqb = lambda h: pl.BlockSpec((1, h, tm), lambda i: (i // tps, 0, i % tps))
    akb, ikb, rec4, aqt, iqt, avb, iwt, akt, avt, ikt = pl.pallas_call(
        functools.partial(_inproj_prompt_kernel, idx_w_scale=idx_w_scale),
        grid=(n // tm,),
        in_specs=[row(d), _resident(g1.shape), _resident(w_main.shape), _resident(w_tok.shape),
                  _resident(gk_row.shape), _resident(gi_row.shape), _resident(gq_col.shape),
                  _resident(gk_col.shape), _resident(gi_col.shape)],
        out_specs=[row(kvw), row(IDX_DIM), row(4 * rec_width),
                   colb(att_width), colb(idx_width), colb(kvw), colb(IDX_HEADS),
                   seqb(kvw), seqb(kvw), seqb(IDX_DIM)],
        out_shape=[jax.ShapeDtypeStruct((n, kvw), BF16), jax.ShapeDtypeStruct((n, IDX_DIM), BF16),
                   jax.ShapeDtypeStruct((n, 4 * rec_width), F32),
                   jax.ShapeDtypeStruct((att_width, n), BF16), jax.ShapeDtypeStruct((idx_width, n), BF16),
                   jax.ShapeDtypeStruct((kvw, n), BF16), jax.ShapeDtypeStruct((IDX_HEADS, n), F32),
                   jax.ShapeDtypeStruct((bsz, kvw, seq), F32), jax.ShapeDtypeStruct((bsz, kvw, seq), F32),
                   jax.ShapeDtypeStruct((bsz, IDX_DIM, seq), F32)],
        compiler_params=_params("arbitrary"),
        name="inproj_prompt",
    )(xf, g1, w_main, w_tok, gk_row, gi_row, gq_col, gk_col, gi_col)

    tb = _row_tile(seq, 256)
    n_rec = rec_width // REC_DIM
    tiles = seq // tb
    rec, s_fin = pl.pallas_call(
        functools.partial(_hgrn_prompt_kernel, layer=layer, rec_width=rec_width),
        grid=(bsz, tiles),
        in_specs=[_resident(lbs.shape), pl.BlockSpec((tb, 4 * rec_width), lambda b, t: (b * tiles + t, 0)),
                  _resident(gn_row.shape)],
        out_specs=[pl.BlockSpec((tb, rec_width), lambda b, t: (b * tiles + t, 0)),
                   pl.BlockSpec((1, n_rec, REC_DIM, REC_DIM), lambda b, t: (b, 0, 0, 0))],
        out_shape=[jax.ShapeDtypeStruct((n, rec_width), BF16),
                   jax.ShapeDtypeStruct((bsz, n_rec, REC_DIM, REC_DIM), F32)],
        scratch_shapes=[pltpu.VMEM((n_rec, REC_DIM, REC_DIM), F32)],
        compiler_params=_params("arbitrary", "arbitrary"),
        name="hgrn_prompt",
    )(lbs, rec4, gn_row)

    att = _prompt_attention(iqt, aqt, iwt, ikb, akb, avb, bsz, seq)

    y = _outproj_mlp(xf, att, rec, w_out, g2, w_up, w_dn, "outproj_mlp_prompt")
    to_heads = lambda t: t.reshape(bsz, ATT_KV_HEADS, ATT_HEAD_DIM, seq).transpose(0, 3, 1, 2)
    return y.reshape(bsz, seq, d), to_heads(akt), to_heads(avt), ikt.transpose(0, 2, 1), s_fin


def _sample_layer(x, s0, lbs, layer, cache_k, cache_v, cache_kidx, page_table, w_main, w_qry, g1, gq_row,
                  gk_row, gi_row, gn_row, w_out, g2, w_up, w_dn, att_width, idx_width, rec_width, idx_w_scale):
    bd, t_new, d = x.shape
    assert t_new == 1, "the sample path handles one new token per sequence"
    kvw = ATT_KV_HEADS * ATT_HEAD_DIM
    n_heads = att_width // ATT_HEAD_DIM
    n_rec = rec_width // REC_DIM
    page = cache_k.shape[2]
    n_pages = page_table.shape[1]
    past = n_pages * page
    width = past + V7X_LANES
    tile = 2048 if past % 2048 == 0 else past
    xf = x.reshape(bd, d)

    full = lambda shape: _resident(shape)
    akn, av, ikn, rec4, aq, iq, iw = pl.pallas_call(
        functools.partial(_inproj_sample_kernel, idx_w_scale=idx_w_scale),
        grid=(1,),
        in_specs=[full(xf.shape), full(g1.shape), full(w_main.shape), full(w_qry.shape),
                  full(gk_row.shape), full(gi_row.shape)],
        out_specs=[full((bd, kvw)), full((bd, kvw)), full((bd, IDX_DIM)), full((bd, 4 * rec_width)),
                   full((bd, att_width)), full((bd, idx_width)), full((bd, 128))],
        out_shape=[jax.ShapeDtypeStruct((bd, kvw), F32), jax.ShapeDtypeStruct((bd, kvw), F32),
                   jax.ShapeDtypeStruct((bd, IDX_DIM), F32), jax.ShapeDtypeStruct((bd, 4 * rec_width), F32),
                   jax.ShapeDtypeStruct((bd, att_width), F32), jax.ShapeDtypeStruct((bd, idx_width), F32),
                   jax.ShapeDtypeStruct((bd, 128), F32)],
        compiler_params=_params("arbitrary"),
        name="inproj_sample",
    )(xf, g1, w_main, w_qry, gk_row, gi_row)

    rec, s1 = pl.pallas_call(
        functools.partial(_hgrn_step_kernel, layer=layer, rec_width=rec_width),
        grid=(bd,),
        in_specs=[_resident(lbs.shape), pl.BlockSpec((1, 1, 4 * rec_width), lambda b: (b, 0, 0)),
                  _resident(gn_row.shape), pl.BlockSpec((1, n_rec, REC_DIM, REC_DIM), lambda b: (b, 0, 0, 0))],
        out_specs=[pl.BlockSpec((1, 1, rec_width), lambda b: (b, 0, 0)),
                   pl.BlockSpec((1, n_rec, REC_DIM, REC_DIM), lambda b: (b, 0, 0, 0))],
        out_shape=[jax.ShapeDtypeStruct((bd, 1, rec_width), BF16),
                   jax.ShapeDtypeStruct((bd, n_rec, REC_DIM, REC_DIM), F32)],
        compiler_params=_params("arbitrary"),
        name="hgrn_step",
    )(lbs, rec4.reshape(bd, 1, 4 * rec_width), gn_row, s0)

    kidx_t = cache_kidx.transpose(0, 1, 3, 2)
    k_t = cache_k.transpose(0, 1, 3, 4, 2)
    v_t = cache_v.transpose(0, 1, 3, 4, 2)
    iq3 = iq.reshape(bd, IDX_HEADS, IDX_DIM)
    iw3 = iw[:, 0:IDX_HEADS].reshape(bd, IDX_HEADS, 1)
    ikn3 = ikn.reshape(bd, 1, IDX_DIM)
    scores = pl.pallas_call(
        functools.partial(_sample_scores_kernel, layer=layer, n_pages=n_pages, page=page, tile=tile),
        grid_spec=pltpu.PrefetchScalarGridSpec(
            num_scalar_prefetch=1,
            grid=(bd,),
            in_specs=[pl.BlockSpec((1, IDX_HEADS, IDX_DIM), lambda b, pt: (b, 0, 0)),
                      pl.BlockSpec((1, IDX_HEADS, 1), lambda b, pt: (b, 0, 0)),
                      pl.BlockSpec((1, 1, IDX_DIM), lambda b, pt: (b, 0, 0)),
                      pl.BlockSpec(memory_space=pl.ANY)],
            out_specs=pl.BlockSpec((1, 1, width), lambda b, pt: (b, 0, 0)),
            scratch_shapes=[pltpu.VMEM((2, IDX_DIM, past), F32), pltpu.SemaphoreType.DMA((2,))]),
        out_shape=jax.ShapeDtypeStruct((bd, 1, width), F32),
        compiler_params=_params("arbitrary"),
        name="sample_scores",
    )(page_table, iq3, iw3, ikn3, kidx_t)

    n_sel = min(TOPK_MAX, (past + 1) // 4)
    bias = pl.pallas_call(
        functools.partial(_sample_select_kernel, n_sel=n_sel, past=past),
        grid=(1,),
        in_specs=[full((bd, width))],
        out_specs=full((bd, width)),
        out_shape=jax.ShapeDtypeStruct((bd, width), F32),
        scratch_shapes=[pltpu.VMEM((bd, width), I32)],
        compiler_params=_params("arbitrary"),
        name="sample_select",
    )(scores.reshape(bd, width))

    att_t = pl.pallas_call(
        functools.partial(_sample_attention_kernel, layer=layer, n_pages=n_pages, page=page, tile=tile),
        grid_spec=pltpu.PrefetchScalarGridSpec(
            num_scalar_prefetch=1,
            grid=(bd,),
            in_specs=[pl.BlockSpec((1, n_heads, ATT_HEAD_DIM), lambda b, pt: (b, 0, 0)),
                      pl.BlockSpec((1, ATT_HEAD_DIM), lambda b, pt: (0, 0)),
                      pl.BlockSpec((1, 1, kvw), lambda b, pt: (b, 0, 0)),
                      pl.BlockSpec((1, 1, kvw), lambda b, pt: (b, 0, 0)),
                      pl.BlockSpec((1, 1, width), lambda b, pt: (b, 0, 0)),
                      pl.BlockSpec(memory_space=pl.ANY), pl.BlockSpec(memory_space=pl.ANY)],
            out_specs=pl.BlockSpec((1, kvw, n_heads), lambda b, pt: (b, 0, 0)),
            scratch_shapes=[pltpu.VMEM((2, ATT_KV_HEADS, ATT_HEAD_DIM, past), F32),
                            pltpu.VMEM((2, ATT_KV_HEADS, ATT_HEAD_DIM, past), F32),
                            pltpu.VMEM((n_heads, width), F32),
                            pltpu.SemaphoreType.DMA((2,)), pltpu.SemaphoreType.DMA((2,))]),
        out_shape=jax.ShapeDtypeStruct((bd, kvw, n_heads), F32),
        compiler_params=_params("arbitrary"),
        name="sample_attention",
    )(page_table, aq.reshape(bd, n_heads, ATT_HEAD_DIM), gq_row, akn.reshape(bd, 1, kvw), av.reshape(bd, 1, kvw),
      bias.reshape(bd, 1, width), k_t, v_t)
    att_g = att_t.reshape(bd, ATT_KV_HEADS, ATT_HEAD_DIM, n_heads)
    group = n_heads // ATT_KV_HEADS
    att = jnp.stack([att_g[:, h // group, :, h] for h in range(n_heads)], axis=1)

    y = _outproj_mlp(xf, att.reshape(bd, att_width), rec.reshape(bd, rec_width), w_out, g2, w_up, w_dn,
                     "outproj_mlp_sample")
    return (y.reshape(bd, 1, d), akn.reshape(bd, 1, ATT_KV_HEADS, ATT_HEAD_DIM),
            av.reshape(bd, 1, ATT_KV_HEADS, ATT_HEAD_DIM), ikn.reshape(bd, 1, IDX_DIM), s1)


def kernel(x_prompt, x_sample, cache_k, cache_v, cache_kidx, state_hgrn, page_table, norm1_g, w_in, q_norm_g,
           k_norm_g, idx_k_norm_g, lower_bounds, rec_norm_g, w_out, norm2_g, w_up, w_down):
    depth = w_in.shape[0]
    d_model = x_prompt.shape[-1]
    kv_width = ATT_KV_HEADS * ATT_HEAD_DIM
    idx_width = IDX_HEADS * IDX_DIM
    rec_width = lower_bounds.shape[-1]
    att_width = w_out.shape[1] - rec_width
    assert w_in.shape[-1] == att_width + 2 * kv_width + idx_width + IDX_DIM + IDX_HEADS + 4 * rec_width
    assert rec_norm_g.shape[-1] == REC_DIM and q_norm_g.shape[-1] == ATT_HEAD_DIM
    idx_w_scale = (IDX_HEADS ** -0.5) * (IDX_DIM ** -0.5)
    att_scale = ATT_HEAD_DIM ** -0.5
    n_heads = att_width // ATT_HEAD_DIM
    lbs = lower_bounds.astype(F32)

    yp, ys = x_prompt, x_sample
    outs = [[] for _ in range(8)]
    for l in range(depth):
        w_main, w_tok, w_qry = _layer_weights(w_in[l], att_width, kv_width, idx_width, rec_width)
        g1 = norm1_g[l].reshape(1, d_model)
        g2 = norm2_g[l].reshape(1, d_model)
        gq_row = (q_norm_g[l] * att_scale).reshape(1, ATT_HEAD_DIM)
        gq_col = jnp.tile(q_norm_g[l] * att_scale, n_heads).reshape(att_width, 1)
        gk_row = jnp.tile(k_norm_g[l], ATT_KV_HEADS).reshape(1, kv_width)
        gi_row = idx_k_norm_g[l].reshape(1, IDX_DIM)
        gk_col = gk_row.reshape(kv_width, 1)
        gi_col = gi_row.reshape(IDX_DIM, 1)
        gn_row = rec_norm_g[l].reshape(1, REC_DIM)
        wo, wu, wd = w_out[l].astype(BF16), w_up[l].astype(BF16), w_down[l].astype(BF16)
        yp, kp, vp, ip, sp = _prompt_layer(yp, lbs, l, w_main, w_tok, g1, gq_col, gk_row, gi_row, gk_col, gi_col,
                                           gn_row, wo, g2, wu, wd, att_width, idx_width, rec_width, idx_w_scale)
        ys, k_s, v_s, i_s, s_s = _sample_layer(ys, state_hgrn[l], lbs, l, cache_k, cache_v, cache_kidx,
                                               page_table, w_main, w_qry, g1, gq_row, gk_row, gi_row, gn_row, wo,
                                               g2, wu, wd, att_width, idx_width, rec_width, idx_w_scale)
        for acc, val in zip(outs, (kp, vp, ip, sp, k_s, v_s, i_s, s_s)):
            acc.append(val)
    return (yp, ys) + tuple(jnp.stack(o) for o in outs)
```

```python
import functools

import jax
import jax.numpy as jnp
from jax import lax
from jax.experimental import pallas as pl
from jax.experimental.pallas import tpu as pltpu

F32 = jnp.float32
BF16 = jnp.bfloat16
I32 = jnp.int32

EPS = 1e-6
MASKED = -1e30
INT_MIN = -(2 ** 31)
INT_MAX = 2 ** 31 - 1

ATT_HEAD_DIM = 64
ATT_KV_HEADS = 2
IDX_HEADS = 8
IDX_DIM = 64
REC_DIM = 128
TOPK_MAX = 256
CHUNK = 64

V7X_LANES = 128
V7X_VMEM_LIMIT_BYTES = 56 * 1024 * 1024
Q_TILE = 128
KEY_TILE = 256


def _dot(a, b):
    return jnp.dot(a, b, preferred_element_type=F32)


def _dot_nt(a, b):
    return lax.dot_general(a, b, (((1,), (1,)), ((), ())), preferred_element_type=F32)


def _dot_tn(a, b):
    return lax.dot_general(a, b, (((0,), (0,)), ((), ())), preferred_element_type=F32)


def _params(*semantics):
    return pltpu.CompilerParams(dimension_semantics=semantics, vmem_limit_bytes=V7X_VMEM_LIMIT_BYTES)


def _resident(shape):
    nd = len(shape)
    return pl.BlockSpec(shape, lambda *_: (0,) * nd)


def _rms(x, axis):
    return x * lax.rsqrt(jnp.mean(x * x, axis=axis, keepdims=True) + EPS)


def _sigmoid_pair(x):
    t = jnp.exp(-jnp.abs(x))
    r = 1.0 / (1.0 + t)
    tr = t * r
    pos = x >= 0
    return jnp.where(pos, r, tr), jnp.where(pos, tr, r)


def _silu(x):
    s, _ = _sigmoid_pair(x)
    return x * s


def _order_key(score):
    bits = lax.bitcast_convert_type(score, I32)
    return bits ^ ((bits >> 31) & INT_MAX)


def _lower_bound(lbs_ref, layer):
    lbs = lbs_ref[...]
    e = jnp.exp(lbs - jnp.max(lbs, axis=0, keepdims=True))
    p = e / jnp.sum(e, axis=0, keepdims=True)
    return jnp.sum(p[0:layer + 1, :], axis=0, keepdims=True)


def _head_rms_lanes(x, gain_row):
    sq = x * x
    lane = lax.broadcasted_iota(I32, x.shape, 1)
    lo = lane < ATT_HEAD_DIM
    s_lo = jnp.sum(jnp.where(lo, sq, 0.0), axis=-1, keepdims=True)
    s_hi = jnp.sum(jnp.where(lo, 0.0, sq), axis=-1, keepdims=True)
    inv = jnp.where(lo, lax.rsqrt(s_lo / ATT_HEAD_DIM + EPS), lax.rsqrt(s_hi / ATT_HEAD_DIM + EPS))
    return x * inv * gain_row


def _inproj_prompt_kernel(x_ref, g1_ref, wm_ref, wt_ref, gk_ref, gi_ref, gq_ref, gkc_ref, gic_ref,
                          akb_ref, ikb_ref, rec4_ref, aqt_ref, iqt_ref, avb_ref, iwt_ref, akt_ref, avt_ref, ikt_ref,
                          *, idx_w_scale):
    x = x_ref[...]
    hb = (_rms(x, -1) * g1_ref[...]).astype(BF16)
    y = _dot(hb, wm_ref[...])
    akb_ref[...] = _head_rms_lanes(y[:, 0:128], gk_ref[...]).astype(BF16)
    ikb_ref[...] = (_rms(y[:, 128:128 + IDX_DIM], -1) * gi_ref[...]).astype(BF16)
    rec4_ref[...] = y[:, 256:]

    yt = _dot_nt(wt_ref[...], hb)
    n_q = aqt_ref.shape[0] // ATT_HEAD_DIM
    for h in range(n_q):
        rows = slice(h * ATT_HEAD_DIM, (h + 1) * ATT_HEAD_DIM)
        aqt_ref[rows, :] = (_rms(yt[rows, :], 0) * gq_ref[rows, :]).astype(BF16)
    o = aqt_ref.shape[0]
    iqt_ref[...] = yt[o:o + iqt_ref.shape[0], :].astype(BF16)
    o += iqt_ref.shape[0]
    kvw = ATT_KV_HEADS * ATT_HEAD_DIM
    av_t = yt[o:o + kvw, :]
    avt_ref[0] = av_t
    avb_ref[...] = av_t.astype(BF16)
    o += kvw
    for g in range(ATT_KV_HEADS):
        rows = slice(g * ATT_HEAD_DIM, (g + 1) * ATT_HEAD_DIM)
        akt_ref[0, rows, :] = _rms(yt[o + g * ATT_HEAD_DIM:o + (g + 1) * ATT_HEAD_DIM, :], 0) * gkc_ref[rows, :]
    o += kvw
    ikt_ref[0] = _rms(yt[o:o + IDX_DIM, :], 0) * gic_ref[...]
    o += IDX_DIM
    iwt_ref[...] = yt[o:o + IDX_HEADS, :] * idx_w_scale


def _inproj_sample_kernel(x_ref, g1_ref, wm_ref, wq_ref, gk_ref, gi_ref,
                          akn_ref, av_ref, ikn_ref, rec4_ref, aq_ref, iq_ref, iw_ref, *, idx_w_scale):
    x = x_ref[...]
    hb = (_rms(x, -1) * g1_ref[...]).astype(BF16)
    y = _dot(hb, wm_ref[...])
    akn_ref[...] = _head_rms_lanes(y[:, 0:128], gk_ref[...])
    ikn_ref[...] = _rms(y[:, 128:128 + IDX_DIM], -1) * gi_ref[...]
    rec4_ref[...] = y[:, 256:]
    yq = _dot(hb, wq_ref[...])
    wa = aq_ref.shape[1]
    wi = iq_ref.shape[1]
    aq_ref[...] = yq[:, 0:wa]
    iq_ref[...] = yq[:, wa:wa + wi]
    av_ref[...] = yq[:, wa + wi:wa + wi + 128]
    iw_ref[...] = yq[:, wa + wi + 128:wa + wi + 256] * idx_w_scale


def _hgrn_gates(rq, rf, ri, rg, lb):
    sp, sn = _sigmoid_pair(rf)
    q = _silu(rq) * (REC_DIM ** -0.5)
    log_f = jnp.log(lb + (1.0 - lb) * sp)
    k = (1.0 - lb) * sn
    return q, log_f, k, ri, _silu(rg)


def _hgrn_prompt_kernel(lbs_ref, rec4_ref, gn_ref, rec_ref, sfin_ref, st_ref, *, layer, rec_width):
    n_heads = rec_width // REC_DIM
    n_seq, tb = rec4_ref.shape[0], rec4_ref.shape[1]

    @pl.when(pl.program_id(1) == 0)
    def _():
        st_ref[...] = jnp.zeros_like(st_ref)

    lb = _lower_bound(lbs_ref, layer)
    w = rec_width
    row = lax.broadcasted_iota(I32, (CHUNK, CHUNK), 0)
    col = lax.broadcasted_iota(I32, (CHUNK, CHUNK), 1)
    causal = row >= col
    tri = jnp.where(causal, 1.0, 0.0).astype(BF16)
    gn = gn_ref[...]
    gates = [_hgrn_gates(rec4_ref[sq, :, 0:w], rec4_ref[sq, :, w:2 * w], rec4_ref[sq, :, 2 * w:3 * w],
                         rec4_ref[sq, :, 3 * w:4 * w], lb) for sq in range(n_seq)]

    for c in range(tb // CHUNK):
        rows = slice(c * CHUNK, (c + 1) * CHUNK)
        for sq in range(n_seq):
            q, log_f, k, v, gate = gates[sq]
            gc = log_f[rows, :]
            g1 = gc.astype(BF16)
            r1 = gc - g1.astype(F32)
            g2 = r1.astype(BF16)
            g3 = (r1 - g2.astype(F32)).astype(BF16)
            b = _dot(tri, g1) + _dot(tri, g2) + _dot(tri, g3)
            b_last = b[CHUNK - 1:CHUNK, :]
            b_mid = b[CHUNK // 2 - 1:CHUNK // 2, :]
            qc, kc, vc = q[rows, :], k[rows, :], v[rows, :]
            q_in = (qc * jnp.exp(b - b_mid)).astype(BF16)
            k_in = (kc * jnp.exp(b_mid - b)).astype(BF16)
            q_st = (qc * jnp.exp(b)).astype(BF16)
            k_st = (kc * jnp.exp(b_last - b)).astype(BF16)
            vb = vc.astype(BF16)
            decay = jnp.exp(b_last)
            for h in range(n_heads):
                cols = slice(h * REC_DIM, (h + 1) * REC_DIM)
                a = jnp.where(causal, _dot_nt(q_in[:, cols], k_in[:, cols]), 0.0)
                st = st_ref[sq, h]
                o = _dot(a.astype(BF16), vb[:, cols]) + _dot_nt(q_st[:, cols], st.astype(BF16))
                st_ref[sq, h] = st * decay[:, cols] + _dot_tn(vb[:, cols], k_st[:, cols])
                o = _rms(o, -1) * gn * gate[rows, cols]
                rec_ref[sq, rows, cols] = o.astype(rec_ref.dtype)

    @pl.when(pl.program_id(1) == pl.num_programs(1) - 1)
    def _():
        for sq in range(n_seq):
            for h in range(n_heads):
                sfin_ref[sq, h] = st_ref[sq, h].T


def _row_to_col(x_row):
    n = x_row.shape[1]
    eye = lax.broadcasted_iota(I32, (n, n), 0) == lax.broadcasted_iota(I32, (n, n), 1)
    return jnp.sum(jnp.where(eye, jnp.broadcast_to(x_row, (n, n)), 0.0), axis=1, keepdims=True)


def _col_to_row(x_col):
    n = x_col.shape[0]
    eye = lax.broadcasted_iota(I32, (n, n), 0) == lax.broadcasted_iota(I32, (n, n), 1)
    return jnp.sum(jnp.where(eye, jnp.broadcast_to(x_col, (n, n)), 0.0), axis=0, keepdims=True)


def _hgrn_step_kernel(lbs_ref, rec4_ref, gn_ref, s0_ref, rec_ref, s1_ref, *, layer, rec_width):
    n_heads = rec_width // REC_DIM
    lb = _lower_bound(lbs_ref, layer)
    w = rec_width
    r4 = rec4_ref[0]
    q, log_f, k, v, gate = _hgrn_gates(r4[:, 0:w], r4[:, w:2 * w], r4[:, 2 * w:3 * w], r4[:, 3 * w:4 * w], lb)
    f = jnp.exp(log_f)
    gn = gn_ref[...]
    for h in range(n_heads):
        cols = slice(h * REC_DIM, (h + 1) * REC_DIM)
        s1 = _row_to_col(f[:, cols]) * s0_ref[0, h] + _row_to_col(k[:, cols]) * v[:, cols]
        s1_ref[0, h] = s1
        o = jnp.sum(_row_to_col(q[:, cols]) * s1, axis=0, keepdims=True)
        rec_ref[0, :, cols] = (_rms(o, -1) * gn * gate[:, cols]).astype(rec_ref.dtype)


def _prompt_attention_kernel(iqa_ref, iqb_ref, aqa_ref, aqb_ref, iwa_ref, iwb_ref, ikb_ref, akb_ref, avb_ref,
                             oa_ref, ob_ref, iqs_ref, qpad_ref, iws_ref, keys_ref, bias_ref, logit_ref, acc_ref,
                             *, n_sel, n_units):
    p = pl.program_id(1)
    nq = 2 * pl.num_programs(1)
    j_of = (p, nq - 1 - p)
    n_a = p // 2 + 1
    n_heads = aqa_ref.shape[0] // ATT_HEAD_DIM
    group = n_heads // ATT_KV_HEADS
    half = KEY_TILE // 2
    cw = 2 * Q_TILE
    n_ct = n_heads // 2
    ct_per_group = n_ct // ATT_KV_HEADS

    for s, (iq_ref, aq_ref, iw_ref) in enumerate(((iqa_ref, aqa_ref, iwa_ref), (iqb_ref, aqb_ref, iwb_ref))):
        iqs_ref[s] = jnp.concatenate([iq_ref[h * IDX_DIM:(h + 1) * IDX_DIM, :] for h in range(IDX_HEADS)], axis=1)
        halves = [jnp.concatenate([aq_ref[(g * group + hh) * ATT_HEAD_DIM:(g * group + hh + 1) * ATT_HEAD_DIM, :]
                                   for hh in range(group)], axis=1) for g in range(ATT_KV_HEADS)]
        zero = jnp.zeros_like(halves[0])
        qpad_ref[s] = jnp.concatenate([jnp.concatenate([halves[0], zero], axis=1),
                                       jnp.concatenate([zero, halves[1]], axis=1)], axis=0)
        iws_ref[s] = iw_ref[...]

    def unit(u):
        is_b = u >= n_a
        return is_b, jnp.where(is_b, 1, 0), jnp.where(is_b, u - n_a, u), jnp.where(is_b, j_of[1], j_of[0])

    def key_rows(kt, mc):
        return pl.ds(pl.multiple_of(kt * KEY_TILE + mc * half, half), half)

    row_h = lax.broadcasted_iota(I32, (half, Q_TILE), 0)
    lane_h = lax.broadcasted_iota(I32, (half, Q_TILE), 1)

    for u in range(n_units):
        _, s, kt, j = unit(u)
        iw = iws_ref[s]
        for mc in range(2):
            ik = ikb_ref[key_rows(kt, mc), :]
            sc = jnp.zeros((half, Q_TILE), F32)
            for ct in range(IDX_HEADS // 2):
                d = _dot(ik, iqs_ref[s, :, ct * cw:(ct + 1) * cw])
                for hh in range(2):
                    h = 2 * ct + hh
                    sc = sc + jnp.maximum(d[:, hh * Q_TILE:(hh + 1) * Q_TILE], 0.0) * iw[h:h + 1, :]
            causal = kt * KEY_TILE + mc * half + row_h <= j * Q_TILE + lane_h
            keys_ref[u, mc * half:(mc + 1) * half, :] = jnp.where(causal, _order_key(sc), INT_MIN)

    zero8 = jnp.zeros((8, Q_TILE), I32)

    def count2(pred):
        tot = [zero8, zero8]
        for u in range(n_units):
            is_b, _, kt, _ = unit(u)
            m = pred(keys_ref[u], lambda a, b_: jnp.where(is_b, b_, a), kt)
            part = jnp.sum(jnp.where(m, 1, 0).reshape(KEY_TILE // 8, 8, Q_TILE), axis=0)
            tot[0] = tot[0] + jnp.where(is_b, zero8, part)
            tot[1] = tot[1] + jnp.where(is_b, part, zero8)
        return tuple(jnp.sum(t, axis=0, keepdims=True) for t in tot)

    def bit_step(i, carry):
        pref, at_thr = carry
        cand = tuple(pr | jnp.left_shift(jnp.int32(1), 31 - i) for pr in pref)
        ck = tuple(c ^ INT_MIN for c in cand)
        cnt = count2(lambda kk, pick, kt: kk >= pick(ck[0], ck[1]))
        take = tuple(cnt[t] >= n_sel for t in range(2))
        return (tuple(jnp.where(take[t], cand[t], pref[t]) for t in range(2)),
                tuple(jnp.where(take[t], cnt[t], at_thr[t]) for t in range(2)))

    zero_q = jnp.zeros((1, Q_TILE), I32)
    pref, at_thr = lax.fori_loop(0, 32, bit_step, ((zero_q, zero_q), (zero_q, zero_q)))
    thr = tuple(jnp.maximum(pr ^ INT_MIN, INT_MIN + 1) for pr in pref)

    above = count2(lambda kk, pick, kt: kk > pick(thr[0], thr[1]))
    need = tuple(n_sel - a for a in above)
    any_tie = jnp.max(jnp.maximum(jnp.where(at_thr[0] > n_sel, 1, 0), jnp.where(at_thr[1] > n_sel, 1, 0))) > 0
    pos_bits = max(1, (ikb_ref.shape[0] - 1).bit_length())
    row_k = lax.broadcasted_iota(I32, (KEY_TILE, Q_TILE), 0)

    def last_kept_tie():
        def step(i, pre):
            cand = tuple(pr | jnp.left_shift(jnp.int32(1), pos_bits - 1 - i) for pr in pre)
            cnt = count2(lambda kk, pick, kt: (kk == pick(thr[0], thr[1])) & (kt * KEY_TILE + row_k < pick(cand[0], cand[1])))
            return tuple(jnp.where(cnt[t] < need[t], cand[t], pre[t]) for t in range(2))
        return lax.fori_loop(0, pos_bits, step, (zero_q, zero_q))

    big = jnp.full((1, Q_TILE), INT_MAX, I32)
    last_tie = lax.cond(any_tie, last_kept_tie, lambda: (big, big))

    for u in range(n_units):
        is_b, _, kt, _ = unit(u)
        t_u = jnp.where(is_b, thr[1], thr[0])
        last_u = jnp.where(is_b, last_tie[1], last_tie[0])
        kk = keys_ref[u]
        keep = (kk > t_u) | ((kk == t_u) & (kt * KEY_TILE + row_k <= last_u))
        bias_ref[u] = jnp.where(keep, 0.0, MASKED)

    neg = jnp.full((8, cw), MASKED, F32)
    cmax = [[neg] * n_ct, [neg] * n_ct]
    for u in range(n_units):
        is_b, s, kt, _ = unit(u)
        for mc in range(2):
            k_rows = akb_ref[key_rows(kt, mc), :]
            b_rows = bias_ref[u, mc * half:(mc + 1) * half, :]
            bias2 = jnp.concatenate([b_rows, b_rows], axis=1)
            for ct in range(n_ct):
                logit = _dot(k_rows, qpad_ref[s, :, ct * cw:(ct + 1) * cw]) + bias2
                logit_ref[u, mc * half:(mc + 1) * half, ct * cw:(ct + 1) * cw] = logit
                part = jnp.max(logit.reshape(half // 8, 8, cw), axis=0)
                cmax[0][ct] = jnp.maximum(cmax[0][ct], jnp.where(is_b, neg, part))
                cmax[1][ct] = jnp.maximum(cmax[1][ct], jnp.where(is_b, part, neg))
    cmax = [[jnp.max(c, axis=0, keepdims=True) for c in cm] for cm in cmax]

    acc_ref[...] = jnp.zeros_like(acc_ref)
    zsum = jnp.zeros((8, cw), F32)
    lsum = [[zsum] * n_ct, [zsum] * n_ct]
    for u in range(n_units):
        is_b, s, kt, _ = unit(u)
        for ct in range(n_ct):
            g = ct // ct_per_group
            m_col = jnp.where(is_b, cmax[1][ct], cmax[0][ct])
            contrib = jnp.zeros((ATT_HEAD_DIM, cw), F32)
            for mc in range(2):
                pr = jnp.exp(logit_ref[u, mc * half:(mc + 1) * half, ct * cw:(ct + 1) * cw] - m_col)
                part = jnp.sum(pr.reshape(half // 8, 8, cw), axis=0)
                lsum[0][ct] = lsum[0][ct] + jnp.where(is_b, zsum, part)
                lsum[1][ct] = lsum[1][ct] + jnp.where(is_b, part, zsum)
                v_t = avb_ref[kt, g * ATT_HEAD_DIM:(g + 1) * ATT_HEAD_DIM, mc * half:(mc + 1) * half]
                contrib = contrib + _dot(v_t, pr.astype(BF16))
            c0 = (ct % ct_per_group) * cw
            acc_ref[s, g, :, c0:c0 + cw] = acc_ref[s, g, :, c0:c0 + cw] + contrib

    for s, o_ref in enumerate((oa_ref, ob_ref)):
        for ct in range(n_ct):
            g = ct // ct_per_group
            c0 = (ct % ct_per_group) * cw
            den = jnp.sum(lsum[s][ct], axis=0, keepdims=True)
            out_t = acc_ref[s, g, :, c0:c0 + cw] / den
            blk = jnp.concatenate([out_t[:, 0:Q_TILE], out_t[:, Q_TILE:cw]], axis=0)
            o_ref[0, 0, :, ct * 2 * ATT_HEAD_DIM:(ct + 1) * 2 * ATT_HEAD_DIM] = blk.T.astype(o_ref.dtype)


def _page_copy(pt_ref, src_hbm, layer, buf, sem, seq, slot, page):
    def copy(p):
        dst = buf.at[(slot,) + (slice(None),) * (len(buf.shape) - 2) + (pl.ds(pl.multiple_of(p * page, page), page),)]
        return pltpu.make_async_copy(src_hbm.at[layer, pt_ref[seq, p]], dst, sem.at[slot])
    return copy


def _start_pages(copy, n_pages):
    def body(p, c):
        copy(p).start()
        return c
    lax.fori_loop(0, n_pages, body, 0, unroll=8 if n_pages % 8 == 0 else 1)


def _wait_slot(buf, sem, slot):
    pltpu.make_async_copy(buf.at[1 - slot], buf.at[slot], sem.at[slot]).wait()


def _sample_scores_kernel(pt_ref, iq_ref, iw_ref, ikn_ref, kidx_hbm, sc_ref, buf, sem, *, layer, n_pages, page, tile):
    b = pl.program_id(0)
    nb = pl.num_programs(0)
    slot = b % 2

    @pl.when(b == 0)
    def _():
        _start_pages(_page_copy(pt_ref, kidx_hbm, layer, buf, sem, 0, 0, page), n_pages)

    @pl.when(b + 1 < nb)
    def _():
        _start_pages(_page_copy(pt_ref, kidx_hbm, layer, buf, sem, b + 1, 1 - slot, page), n_pages)

    _wait_slot(buf, sem, slot)

    iq = iq_ref[0]
    iqb = iq.astype(BF16)
    iw = iw_ref[0]
    past = n_pages * page
    for t in range(past // tile):
        kb = buf[slot, :, t * tile:(t + 1) * tile].astype(BF16)
        d = _dot(iqb, kb)
        sc_ref[0, :, t * tile:(t + 1) * tile] = jnp.sum(jnp.maximum(d, 0.0) * iw, axis=0, keepdims=True)
    d_new = jnp.sum(iq * ikn_ref[0], axis=1, keepdims=True)
    s_new = jnp.sum(jnp.maximum(d_new, 0.0) * iw, axis=0, keepdims=True)
    lane = lax.broadcasted_iota(I32, (1, V7X_LANES), 1)
    sc_ref[0, :, past:past + V7X_LANES] = jnp.where(lane == 0, s_new, 0.0)


def _sample_select_kernel(sc_ref, bias_ref, keys_ref, *, n_sel, past):
    rows, width = sc_ref.shape
    col = lax.broadcasted_iota(I32, (rows, width), 1)
    keys_ref[...] = jnp.where(col <= past, _order_key(sc_ref[...]), INT_MIN)

    def count(pred):
        return jnp.sum(pred(keys_ref[...]).astype(I32), axis=1, keepdims=True)

    def bit_step(i, prefix):
        cand = prefix | jnp.left_shift(jnp.int32(1), 31 - i)
        cnt = count(lambda kk: kk >= (cand ^ INT_MIN))
        return jnp.where(cnt >= n_sel, cand, prefix)

    prefix = lax.fori_loop(0, 32, bit_step, jnp.zeros((rows, 1), I32))
    thr = jnp.maximum(prefix ^ INT_MIN, INT_MIN + 1)
    need = n_sel - count(lambda kk: kk > thr)
    pos_bits = max(1, (width - 1).bit_length())

    def pos_step(i, pre):
        cand = pre | jnp.left_shift(jnp.int32(1), pos_bits - 1 - i)
        cnt = count(lambda kk: (kk == thr) & (col < cand))
        return jnp.where(cnt < need, cand, pre)

    last_tie = lax.fori_loop(0, pos_bits, pos_step, jnp.zeros((rows, 1), I32))
    kk = keys_ref[...]
    keep = (kk > thr) | ((kk == thr) & (col <= last_tie))
    bias_ref[...] = jnp.where(keep, 0.0, MASKED)


def _sample_attention_kernel(pt_ref, aq_ref, gq_ref, akn_ref, av_ref, bias_ref, k_hbm, v_hbm, o_ref,
                             kbuf, vbuf, logit_ref, ksem, vsem, *, layer, n_pages, page, tile):
    b = pl.program_id(0)
    nb = pl.num_programs(0)
    slot = b % 2

    def start(seq, s):
        _start_pages(_page_copy(pt_ref, k_hbm, layer, kbuf, ksem, seq, s, page), n_pages)
        _start_pages(_page_copy(pt_ref, v_hbm, layer, vbuf, vsem, seq, s, page), n_pages)

    @pl.when(b == 0)
    def _():
        start(0, 0)

    @pl.when(b + 1 < nb)
    def _():
        start(b + 1, 1 - slot)

    _wait_slot(kbuf, ksem, slot)
    _wait_slot(vbuf, vsem, slot)

    n_heads = aq_ref.shape[1]
    group = n_heads // ATT_KV_HEADS
    kvw = ATT_KV_HEADS * ATT_HEAD_DIM
    past = n_pages * page
    q = _rms(aq_ref[0], -1) * gq_ref[...]
    zeros = jnp.zeros_like(q)
    head = lax.broadcasted_iota(I32, (n_heads, kvw), 0)
    q_pad = jnp.where(head < group, jnp.concatenate([q, zeros], axis=1), jnp.concatenate([zeros, q], axis=1))
    q_pad_b = q_pad.astype(BF16)

    for t in range(past // tile):
        cols = slice(t * tile, (t + 1) * tile)
        kb = kbuf[slot, :, :, cols].reshape(kvw, tile).astype(BF16)
        logit_ref[:, cols] = _dot(q_pad_b, kb) + bias_ref[0, :, cols]
    lane = lax.broadcasted_iota(I32, (n_heads, V7X_LANES), 1)
    l_new = jnp.sum(q_pad * akn_ref[0], axis=1, keepdims=True) + bias_ref[0, :, past:past + 1]
    logit_ref[:, past:past + V7X_LANES] = jnp.where(lane == 0, l_new, MASKED)

    m = jnp.max(logit_ref[...], axis=1, keepdims=True)
    lsum = jnp.zeros((n_heads, 1), F32)
    acc = jnp.zeros((kvw, n_heads), F32)
    for t in range(past // tile):
        cols = slice(t * tile, (t + 1) * tile)
        p = jnp.exp(logit_ref[:, cols] - m)
        lsum = lsum + jnp.sum(p, axis=1, keepdims=True)
        acc = acc + _dot_nt(vbuf[slot, :, :, cols].reshape(kvw, tile).astype(BF16), p.astype(BF16))
    p_new = jnp.exp(logit_ref[:, past:past + 1] - m)
    acc = acc + _row_to_col(av_ref[0]) * _col_to_row(p_new)
    o_ref[0] = acc / _col_to_row(lsum + p_new)


def _outproj_mlp_kernel(x_ref, att_ref, rec_ref, wo_ref, g2_ref, wup_ref, wdn_ref, o_ref, *, ff_tile):
    mix = jnp.concatenate([att_ref[...].astype(BF16), rec_ref[...].astype(BF16)], axis=1)
    y = x_ref[...] + _dot(mix, wo_ref[...])
    h2 = (_rms(y, -1) * g2_ref[...]).astype(BF16)
    acc = y
    for c in range(wup_ref.shape[1] // ff_tile):
        u = jnp.maximum(_dot(h2, wup_ref[:, c * ff_tile:(c + 1) * ff_tile]), 0.0)
        acc = acc + _dot((u * u).astype(BF16), wdn_ref[c * ff_tile:(c + 1) * ff_tile, :])
    o_ref[...] = acc


def _row_tile(n_rows, want):
    return want if n_rows % want == 0 else n_rows


def _layer_weights(w_in, att_width, kv_width, idx_width, rec_width):
    o = [0]
    for wdt in (att_width, kv_width, kv_width, idx_width, IDX_DIM, IDX_HEADS, rec_width, rec_width, rec_width, rec_width):
        o.append(o[-1] + wdt)
    aq, ak, av, iq, ik, iw, rq, rf, ri, rg = (w_in[:, o[i]:o[i + 1]] for i in range(10))
    d = w_in.shape[0]
    z = lambda n: jnp.zeros((d, n), w_in.dtype)
    w_main = jnp.concatenate([ak, ik, z(128 - IDX_DIM), rq, rf, ri, rg], axis=1).astype(BF16)
    w_tok = jnp.concatenate([aq, iq, av, ak, ik, iw, z(16 - IDX_HEADS)], axis=1).T.astype(BF16)
    w_qry = jnp.concatenate([aq, iq, av, iw, z(128 - IDX_HEADS)], axis=1).astype(BF16)
    return w_main, w_tok, w_qry


def _outproj_mlp(x, att, rec, w_out, g2, w_up, w_dn, name):
    n, d = x.shape
    tm = _row_tile(n, 512)
    ff = w_up.shape[1]
    return pl.pallas_call(
        functools.partial(_outproj_mlp_kernel, ff_tile=min(ff, 1024)),
        grid=(n // tm,),
        in_specs=[pl.BlockSpec((tm, d), lambda i: (i, 0)),
                  pl.BlockSpec((tm, att.shape[1]), lambda i: (i, 0)),
                  pl.BlockSpec((tm, rec.shape[1]), lambda i: (i, 0)),
                  _resident(w_out.shape), _resident(g2.shape), _resident(w_up.shape), _resident(w_dn.shape)],
        out_specs=pl.BlockSpec((tm, d), lambda i: (i, 0)),
        out_shape=jax.ShapeDtypeStruct((n, d), F32),
        compiler_params=_params("arbitrary"),
        name=name,
    )(x, att, rec, w_out, g2, w_up, w_dn)


def _prompt_attention(iqt, aqt, iwt, ikb, akb, avb, bsz, seq):
    n = bsz * seq
    kvw = ATT_KV_HEADS * ATT_HEAD_DIM
    att_width, idx_width = aqt.shape[0], iqt.shape[0]
    n_sel = min(TOPK_MAX, seq // 4)
    nq = seq // Q_TILE
    assert KEY_TILE == 2 * Q_TILE and nq % 2 == 0, "query tiles are paired so every step sees nq/2 + 1 key tiles"
    n_pairs = nq // 2
    n_units = n_pairs + 1
    kt_per_seq = seq // KEY_TILE
    avb_tiles = avb.reshape(kvw, n // KEY_TILE, KEY_TILE).transpose(1, 0, 2)
    n_heads = att_width // ATT_HEAD_DIM
    gw2 = n_heads * Q_TILE
    qa = lambda h: pl.BlockSpec((h, Q_TILE), lambda b, p: (0, b * nq + p))
    qb = lambda h: pl.BlockSpec((h, Q_TILE), lambda b, p: (0, b * nq + nq - 1 - p))
    outb = pl.BlockSpec((1, 1, Q_TILE, att_width), lambda b, p: (b, p, 0, 0))
    att_a, att_b = pl.pallas_call(
        functools.partial(_prompt_attention_kernel, n_sel=n_sel, n_units=n_units),
        grid=(bsz, n_pairs),
        in_specs=[qa(idx_width), qb(idx_width), qa(att_width), qb(att_width), qa(IDX_HEADS), qb(IDX_HEADS),
                  pl.BlockSpec((seq, IDX_DIM), lambda b, p: (b, 0)),
                  pl.BlockSpec((seq, kvw), lambda b, p: (b, 0)),
                  pl.BlockSpec((kt_per_seq, kvw, KEY_TILE), lambda b, p: (b, 0, 0))],
        out_specs=[outb, outb],
        out_shape=[jax.ShapeDtypeStruct((bsz, n_pairs, Q_TILE, att_width), BF16)] * 2,
        scratch_shapes=[pltpu.VMEM((2, IDX_DIM, IDX_HEADS * Q_TILE), BF16),
                        pltpu.VMEM((2, kvw, gw2), BF16),
                        pltpu.VMEM((2, IDX_HEADS, Q_TILE), F32),
                        pltpu.VMEM((n_units, KEY_TILE, Q_TILE), I32),
                        pltpu.VMEM((n_units, KEY_TILE, Q_TILE), F32),
                        pltpu.VMEM((n_units, KEY_TILE, gw2), F32),
                        pltpu.VMEM((2, ATT_KV_HEADS, ATT_HEAD_DIM, gw2 // ATT_KV_HEADS), F32)],
        compiler_params=_params("arbitrary", "arbitrary"),
        name="attention_prompt",
    )(iqt, iqt, aqt, aqt, iwt, iwt, ikb, akb, avb_tiles)
    return jnp.concatenate([att_a, att_b[:, ::-1]], axis=1).reshape(n, att_width)


def _prompt_layer(x, lbs, layer, w_main, w_tok, g1, gq_col, gk_row, gi_row, gk_col, gi_col, gn_row, w_out, g2,
                  w_up, w_dn, att_width, idx_width, rec_width, idx_w_scale):
    bsz, seq, d = x.shape
    n = bsz * seq
    xf = x.reshape(n, d)
    tm = _row_tile(seq, 512)
    tps = seq // tm
    kvw = ATT_KV_HEADS * ATT_HEAD_DIM
    row = lambda w: pl.BlockSpec((tm, w), lambda i: (i, 0))
    colb = lambda h: pl.BlockSpec((h, tm), lambda i: (0, i))
    seqb = lambda h: pl.BlockSpec((1, h, tm), lambda i: (i // tps, 0, i % tps))
    akb, ikb, rec4, aqt, iqt, avb, iwt, akt, avt, ikt = pl.pallas_call(
        functools.partial(_inproj_prompt_kernel, idx_w_scale=idx_w_scale),
        grid=(n // tm,),
        in_specs=[row(d), _resident(g1.shape), _resident(w_main.shape), _resident(w_tok.shape),
                  _resident(gk_row.shape), _resident(gi_row.shape), _resident(gq_col.shape),
                  _resident(gk_col.shape), _resident(gi_col.shape)],
        out_specs=[row(kvw), row(IDX_DIM), row(4 * rec_width),
                   colb(att_width), colb(idx_width), colb(kvw), colb(IDX_HEADS),
                   seqb(kvw), seqb(kvw), seqb(IDX_DIM)],
        out_shape=[jax.ShapeDtypeStruct((n, kvw), BF16), jax.ShapeDtypeStruct((n, IDX_DIM), BF16),
                   jax.ShapeDtypeStruct((n, 4 * rec_width), F32),
                   jax.ShapeDtypeStruct((att_width, n), BF16), jax.ShapeDtypeStruct((idx_width, n), BF16),
                   jax.ShapeDtypeStruct((kvw, n), BF16), jax.ShapeDtypeStruct((IDX_HEADS, n), F32),
                   jax.ShapeDtypeStruct((bsz, kvw, seq), F32), jax.ShapeDtypeStruct((bsz, kvw, seq), F32),
                   jax.ShapeDtypeStruct((bsz, IDX_DIM, seq), F32)],
        compiler_params=_params("arbitrary"),
        name="inproj_prompt",
    )(xf, g1, w_main, w_tok, gk_row, gi_row, gq_col, gk_col, gi_col)

    tb = _row_tile(seq, 256)
    n_rec = rec_width // REC_DIM
    tiles = seq // tb
    gs = 2 if bsz % 2 == 0 else 1
    rec, s_fin = pl.pallas_call(
        functools.partial(_hgrn_prompt_kernel, layer=layer, rec_width=rec_width),
        grid=(bsz // gs, tiles),
        in_specs=[_resident(lbs.shape), pl.BlockSpec((gs, tb, 4 * rec_width), lambda b, t: (b, t, 0)),
                  _resident(gn_row.shape)],
        out_specs=[pl.BlockSpec((gs, tb, rec_width), lambda b, t: (b, t, 0)),
                   pl.BlockSpec((gs, n_rec, REC_DIM, REC_DIM), lambda b, t: (b, 0, 0, 0))],
        out_shape=[jax.ShapeDtypeStruct((bsz, seq, rec_width), BF16),
                   jax.ShapeDtypeStruct((bsz, n_rec, REC_DIM, REC_DIM), F32)],
        scratch_shapes=[pltpu.VMEM((gs, n_rec, REC_DIM, REC_DIM), F32)],
        compiler_params=_params("arbitrary", "arbitrary"),
        name="hgrn_prompt",
    )(lbs, rec4.reshape(bsz, seq, 4 * rec_width), gn_row)
    rec = rec.reshape(n, rec_width)

    att = _prompt_attention(iqt, aqt, iwt, ikb, akb, avb, bsz, seq)

    y = _outproj_mlp(xf, att, rec, w_out, g2, w_up, w_dn, "outproj_mlp_prompt")
    to_heads = lambda t: t.reshape(bsz, ATT_KV_HEADS, ATT_HEAD_DIM, seq).transpose(0, 3, 1, 2)
    return y.reshape(bsz, seq, d), to_heads(akt), to_heads(avt), ikt.transpose(0, 2, 1), s_fin


def _sample_layer(x, s0, lbs, layer, cache_k, cache_v, cache_kidx, page_table, w_main, w_qry, g1, gq_row,
                  gk_row, gi_row, gn_row, w_out, g2, w_up, w_dn, att_width, idx_width, rec_width, idx_w_scale):
    bd, t_new, d = x.shape
    assert t_new == 1, "the sample path handles one new token per sequence"
    kvw = ATT_KV_HEADS * ATT_HEAD_DIM
    n_heads = att_width // ATT_HEAD_DIM
    n_rec = rec_width // REC_DIM
    page = cache_k.shape[2]
    n_pages = page_table.shape[1]
    past = n_pages * page
    width = past + V7X_LANES
    tile = 2048 if past % 2048 == 0 else past
    xf = x.reshape(bd, d)

    full = lambda shape: _resident(shape)
    akn, av, ikn, rec4, aq, iq, iw = pl.pallas_call(
        functools.partial(_inproj_sample_kernel, idx_w_scale=idx_w_scale),
        grid=(1,),
        in_specs=[full(xf.shape), full(g1.shape), full(w_main.shape), full(w_qry.shape),
                  full(gk_row.shape), full(gi_row.shape)],
        out_specs=[full((bd, kvw)), full((bd, kvw)), full((bd, IDX_DIM)), full((bd, 4 * rec_width)),
                   full((bd, att_width)), full((bd, idx_width)), full((bd, 128))],
        out_shape=[jax.ShapeDtypeStruct((bd, kvw), F32), jax.ShapeDtypeStruct((bd, kvw), F32),
                   jax.ShapeDtypeStruct((bd, IDX_DIM), F32), jax.ShapeDtypeStruct((bd, 4 * rec_width), F32),
                   jax.ShapeDtypeStruct((bd, att_width), F32), jax.ShapeDtypeStruct((bd, idx_width), F32),
                   jax.ShapeDtypeStruct((bd, 128), F32)],
        compiler_params=_params("arbitrary"),
        name="inproj_sample",
    )(xf, g1, w_main, w_qry, gk_row, gi_row)

    rec, s1 = pl.pallas_call(
        functools.partial(_hgrn_step_kernel, layer=layer, rec_width=rec_width),
        grid=(bd,),
        in_specs=[_resident(lbs.shape), pl.BlockSpec((1, 1, 4 * rec_width), lambda b: (b, 0, 0)),
                  _resident(gn_row.shape), pl.BlockSpec((1, n_rec, REC_DIM, REC_DIM), lambda b: (b, 0, 0, 0))],
        out_specs=[pl.BlockSpec((1, 1, rec_width), lambda b: (b, 0, 0)),
                   pl.BlockSpec((1, n_rec, REC_DIM, REC_DIM), lambda b: (b, 0, 0, 0))],
        out_shape=[jax.ShapeDtypeStruct((bd, 1, rec_width), BF16),
                   jax.ShapeDtypeStruct((bd, n_rec, REC_DIM, REC_DIM), F32)],
        compiler_params=_params("arbitrary"),
        name="hgrn_step",
    )(lbs, rec4.reshape(bd, 1, 4 * rec_width), gn_row, s0)

    kidx_t = cache_kidx.transpose(0, 1, 3, 2)
    k_t = cache_k.transpose(0, 1, 3, 4, 2)
    v_t = cache_v.transpose(0, 1, 3, 4, 2)
    iq3 = iq.reshape(bd, IDX_HEADS, IDX_DIM)
    iw3 = iw[:, 0:IDX_HEADS].reshape(bd, IDX_HEADS, 1)
    ikn3 = ikn.reshape(bd, 1, IDX_DIM)
    scores = pl.pallas_call(
        functools.partial(_sample_scores_kernel, layer=layer, n_pages=n_pages, page=page, tile=tile),
        grid_spec=pltpu.PrefetchScalarGridSpec(
            num_scalar_prefetch=1,
            grid=(bd,),
            in_specs=[pl.BlockSpec((1, IDX_HEADS, IDX_DIM), lambda b, pt: (b, 0, 0)),
                      pl.BlockSpec((1, IDX_HEADS, 1), lambda b, pt: (b, 0, 0)),
                      pl.BlockSpec((1, 1, IDX_DIM), lambda b, pt: (b, 0, 0)),
                      pl.BlockSpec(memory_space=pl.ANY)],
            out_specs=pl.BlockSpec((1, 1, width), lambda b, pt: (b, 0, 0)),
            scratch_shapes=[pltpu.VMEM((2, IDX_DIM, past), F32), pltpu.SemaphoreType.DMA((2,))]),
        out_shape=jax.ShapeDtypeStruct((bd, 1, width), F32),
        compiler_params=_params("arbitrary"),
        name="sample_scores",
    )(page_table, iq3, iw3, ikn3, kidx_t)

    n_sel = min(TOPK_MAX, (past + 1) // 4)
    bias = pl.pallas_call(
        functools.partial(_sample_select_kernel, n_sel=n_sel, past=past),
        grid=(1,),
        in_specs=[full((bd, width))],
        out_specs=full((bd, width)),
        out_shape=jax.ShapeDtypeStruct((bd, width), F32),
        scratch_shapes=[pltpu.VMEM((bd, width), I32)],
        compiler_params=_params("arbitrary"),
        name="sample_select",
    )(scores.reshape(bd, width))

    att_t = pl.pallas_call(
        functools.partial(_sample_attention_kernel, layer=layer, n_pages=n_pages, page=page, tile=tile),
        grid_spec=pltpu.PrefetchScalarGridSpec(
            num_scalar_prefetch=1,
            grid=(bd,),
            in_specs=[pl.BlockSpec((1, n_heads, ATT_HEAD_DIM), lambda b, pt: (b, 0, 0)),
                      pl.BlockSpec((1, ATT_HEAD_DIM), lambda b, pt: (0, 0)),
                      pl.BlockSpec((1, 1, kvw), lambda b, pt: (b, 0, 0)),
                      pl.BlockSpec((1, 1, kvw), lambda b, pt: (b, 0, 0)),
                      pl.BlockSpec((1, 1, width), lambda b, pt: (b, 0, 0)),
                      pl.BlockSpec(memory_space=pl.ANY), pl.BlockSpec(memory_space=pl.ANY)],
            out_specs=pl.BlockSpec((1, kvw, n_heads), lambda b, pt: (b, 0, 0)),
            scratch_shapes=[pltpu.VMEM((2, ATT_KV_HEADS, ATT_HEAD_DIM, past), F32),
                            pltpu.VMEM((2, ATT_KV_HEADS, ATT_HEAD_DIM, past), F32),
                            pltpu.VMEM((n_heads, width), F32),
                            pltpu.SemaphoreType.DMA((2,)), pltpu.SemaphoreType.DMA((2,))]),
        out_shape=jax.ShapeDtypeStruct((bd, kvw, n_heads), F32),
        compiler_params=_params("arbitrary"),
        name="sample_attention",
    )(page_table, aq.reshape(bd, n_heads, ATT_HEAD_DIM), gq_row, akn.reshape(bd, 1, kvw), av.reshape(bd, 1, kvw),
      bias.reshape(bd, 1, width), k_t, v_t)
    att_g = att_t.reshape(bd, ATT_KV_HEADS, ATT_HEAD_DIM, n_heads)
    group = n_heads // ATT_KV_HEADS
    att = jnp.stack([att_g[:, h // group, :, h] for h in range(n_heads)], axis=1)

    y = _outproj_mlp(xf, att.reshape(bd, att_width), rec.reshape(bd, rec_width), w_out, g2, w_up, w_dn,
                     "outproj_mlp_sample")
    return (y.reshape(bd, 1, d), akn.reshape(bd, 1, ATT_KV_HEADS, ATT_HEAD_DIM),
            av.reshape(bd, 1, ATT_KV_HEADS, ATT_HEAD_DIM), ikn.reshape(bd, 1, IDX_DIM), s1)


def kernel(x_prompt, x_sample, cache_k, cache_v, cache_kidx, state_hgrn, page_table, norm1_g, w_in, q_norm_g,
           k_norm_g, idx_k_norm_g, lower_bounds, rec_norm_g, w_out, norm2_g, w_up, w_down):
    depth = w_in.shape[0]
    d_model = x_prompt.shape[-1]
    kv_width = ATT_KV_HEADS * ATT_HEAD_DIM
    idx_width = IDX_HEADS * IDX_DIM
    rec_width = lower_bounds.shape[-1]
    att_width = w_out.shape[1] - rec_width
    assert w_in.shape[-1] == att_width + 2 * kv_width + idx_width + IDX_DIM + IDX_HEADS + 4 * rec_width
    assert rec_norm_g.shape[-1] == REC_DIM and q_norm_g.shape[-1] == ATT_HEAD_DIM
    idx_w_scale = (IDX_HEADS ** -0.5) * (IDX_DIM ** -0.5)
    att_scale = ATT_HEAD_DIM ** -0.5
    n_heads = att_width // ATT_HEAD_DIM
    lbs = lower_bounds.astype(F32)

    yp, ys = x_prompt, x_sample
    outs = [[] for _ in range(8)]
    for l in range(depth):
        w_main, w_tok, w_qry = _layer_weights(w_in[l], att_width, kv_width, idx_width, rec_width)
        g1 = norm1_g[l].reshape(1, d_model)
        g2 = norm2_g[l].reshape(1, d_model)
        gq_row = (q_norm_g[l] * att_scale).reshape(1, ATT_HEAD_DIM)
        gq_col = jnp.tile(q_norm_g[l] * att_scale, n_heads).reshape(att_width, 1)
        gk_row = jnp.tile(k_norm_g[l], ATT_KV_HEADS).reshape(1, kv_width)
        gi_row = idx_k_norm_g[l].reshape(1, IDX_DIM)
        gk_col = gk_row.reshape(kv_width, 1)
        gi_col = gi_row.reshape(IDX_DIM, 1)
        gn_row = rec_norm_g[l].reshape(1, REC_DIM)
        wo, wu, wd = w_out[l].astype(BF16), w_up[l].astype(BF16), w_down[l].astype(BF16)
        yp, kp, vp, ip, sp = _prompt_layer(yp, lbs, l, w_main, w_tok, g1, gq_col, gk_row, gi_row, gk_col, gi_col,
                                           gn_row, wo, g2, wu, wd, att_width, idx_width, rec_width, idx_w_scale)
        ys, k_s, v_s, i_s, s_s = _sample_layer(ys, state_hgrn[l], lbs, l, cache_k, cache_v, cache_kidx,
                                               page_table, w_main, w_qry, g1, gq_row, gk_row, gi_row, gn_row, wo,
                                               g2, wu, wd, att_width, idx_width, rec_width, idx_w_scale)
        for acc, val in zip(outs, (kp, vp, ip, sp, k_s, v_s, i_s, s_s)):
            acc.append(val)
    return (yp, ys) + tuple(jnp.stack(o) for o in outs)
```

```python
import functools

import jax
import jax.numpy as jnp
import numpy as np
from jax import lax
from jax.experimental import pallas as pl
from jax.experimental.pallas import tpu as pltpu

F32 = jnp.float32
BF16 = jnp.bfloat16
I32 = jnp.int32

EPS = 1e-6
MASKED = -1e30
INT_MIN = -(2 ** 31)
NEG_INF = float("-inf")
INT_MAX = 2 ** 31 - 1

ATT_HEAD_DIM = 64
ATT_KV_HEADS = 2
IDX_HEADS = 8
IDX_DIM = 64
REC_DIM = 128
TOPK_MAX = 256
CHUNK = 64
HGRN_SAFE_SPAN = 60.0

V7X_LANES = 128
V7X_VMEM_LIMIT_BYTES = 56 * 1024 * 1024
Q_TILE = 128
KEY_TILE = 256


def _dot(a, b):
    return jnp.dot(a, b, preferred_element_type=F32)


def _dot_nt(a, b):
    return lax.dot_general(a, b, (((1,), (1,)), ((), ())), preferred_element_type=F32)


def _dot_tn(a, b):
    return lax.dot_general(a, b, (((0,), (0,)), ((), ())), preferred_element_type=F32)


def _params(*semantics):
    return pltpu.CompilerParams(dimension_semantics=semantics, vmem_limit_bytes=V7X_VMEM_LIMIT_BYTES)


def _resident(shape):
    nd = len(shape)
    return pl.BlockSpec(shape, lambda *_: (0,) * nd)


def _rms(x, axis):
    return x * lax.rsqrt(jnp.mean(x * x, axis=axis, keepdims=True) + EPS)


def _sigmoid_pair(x):
    t = jnp.exp(-jnp.abs(x))
    r = 1.0 / (1.0 + t)
    tr = t * r
    pos = x >= 0
    return jnp.where(pos, r, tr), jnp.where(pos, tr, r)


def _silu(x):
    s, _ = _sigmoid_pair(x)
    return x * s


def _radix_candidate(prefix_bits):
    key = prefix_bits ^ INT_MIN
    return lax.bitcast_convert_type(key ^ ((key >> 31) & INT_MAX), F32)


def _lower_bound(lbs_ref, layer):
    lbs = lbs_ref[...]
    e = jnp.exp(lbs - jnp.max(lbs, axis=0, keepdims=True))
    p = e / jnp.sum(e, axis=0, keepdims=True)
    return jnp.sum(p[0:layer + 1, :], axis=0, keepdims=True)


def _head_rms_lanes(x, gain_row):
    sq = x * x
    lane = lax.broadcasted_iota(I32, x.shape, 1)
    lo = lane < ATT_HEAD_DIM
    s_lo = jnp.sum(jnp.where(lo, sq, 0.0), axis=-1, keepdims=True)
    s_hi = jnp.sum(jnp.where(lo, 0.0, sq), axis=-1, keepdims=True)
    inv = jnp.where(lo, lax.rsqrt(s_lo / ATT_HEAD_DIM + EPS), lax.rsqrt(s_hi / ATT_HEAD_DIM + EPS))
    return x * inv * gain_row


def _inproj_prompt_kernel(x_ref, g1_ref, wm_ref, wt_ref, gk_ref, gi_ref, gq_ref, gkc_ref, gic_ref,
                          akb_ref, ikb_ref, rec4_ref, aqt_ref, iqt_ref, avb_ref, iwt_ref, akt_ref, avt_ref, ikt_ref,
                          *, idx_w_scale):
    x = x_ref[...]
    hb = (_rms(x, -1) * g1_ref[...]).astype(BF16)
    y = _dot(hb, wm_ref[...])
    akb_ref[...] = _head_rms_lanes(y[:, 0:128], gk_ref[...]).astype(BF16)
    ikb_ref[...] = (_rms(y[:, 128:128 + IDX_DIM], -1) * gi_ref[...]).astype(BF16)
    rec4_ref[...] = y[:, 256:]

    yt = _dot_nt(wt_ref[...], hb)
    n_q = aqt_ref.shape[0] // ATT_HEAD_DIM
    for h in range(n_q):
        rows = slice(h * ATT_HEAD_DIM, (h + 1) * ATT_HEAD_DIM)
        aqt_ref[rows, :] = (_rms(yt[rows, :], 0) * gq_ref[rows, :]).astype(BF16)
    o = aqt_ref.shape[0]
    iqt_ref[...] = yt[o:o + iqt_ref.shape[0], :].astype(BF16)
    o += iqt_ref.shape[0]
    kvw = ATT_KV_HEADS * ATT_HEAD_DIM
    av_t = yt[o:o + kvw, :]
    avt_ref[0] = av_t
    avb_ref[...] = av_t.astype(BF16)
    o += kvw
    for g in range(ATT_KV_HEADS):
        rows = slice(g * ATT_HEAD_DIM, (g + 1) * ATT_HEAD_DIM)
        akt_ref[0, rows, :] = _rms(yt[o + g * ATT_HEAD_DIM:o + (g + 1) * ATT_HEAD_DIM, :], 0) * gkc_ref[rows, :]
    o += kvw
    ikt_ref[0] = _rms(yt[o:o + IDX_DIM, :], 0) * gic_ref[...]
    o += IDX_DIM
    iwt_ref[...] = yt[o:o + IDX_HEADS, :] * idx_w_scale


def _inproj_sample_kernel(x_ref, g1_ref, wm_ref, wq_ref, gk_ref, gi_ref,
                          akn_ref, av_ref, ikn_ref, rec4_ref, aq_ref, iq_ref, iw_ref, *, idx_w_scale):
    x = x_ref[...]
    hb = (_rms(x, -1) * g1_ref[...]).astype(BF16)
    y = _dot(hb, wm_ref[...])
    akn_ref[...] = _head_rms_lanes(y[:, 0:128], gk_ref[...])
    ikn_ref[...] = _rms(y[:, 128:128 + IDX_DIM], -1) * gi_ref[...]
    rec4_ref[...] = y[:, 256:]
    yq = _dot(hb, wq_ref[...])
    wa = aq_ref.shape[1]
    wi = iq_ref.shape[1]
    aq_ref[...] = yq[:, 0:wa]
    iq_ref[...] = yq[:, wa:wa + wi]
    av_ref[...] = yq[:, wa + wi:wa + wi + 128]
    iw_ref[...] = yq[:, wa + wi + 128:wa + wi + 256] * idx_w_scale


def _hgrn_gates(rq, rf, ri, rg, lb):
    sp, sn = _sigmoid_pair(rf)
    q = _silu(rq) * (REC_DIM ** -0.5)
    log_f = jnp.log(lb + (1.0 - lb) * sp)
    k = (1.0 - lb) * sn
    return q, log_f, k, ri, _silu(rg)


def _hgrn_prompt_kernel(lbs_ref, rec4_ref, gn_ref, rec_ref, sfin_ref, st_ref, bsc_ref, ksc_ref, *, layer, rec_width,
                        direct):
    n_heads = rec_width // REC_DIM
    n_seq, tb = rec4_ref.shape[0], rec4_ref.shape[1]

    @pl.when(pl.program_id(1) == 0)
    def _():
        st_ref[...] = jnp.zeros_like(st_ref)

    lb = _lower_bound(lbs_ref, layer)
    w = rec_width
    row = lax.broadcasted_iota(I32, (CHUNK, CHUNK), 0)
    col = lax.broadcasted_iota(I32, (CHUNK, CHUNK), 1)
    causal = row >= col
    tri = jnp.where(causal, 1.0, 0.0).astype(BF16)
    gn = gn_ref[...]
    gates = [_hgrn_gates(rec4_ref[sq, :, 0:w], rec4_ref[sq, :, w:2 * w], rec4_ref[sq, :, 2 * w:3 * w],
                         rec4_ref[sq, :, 3 * w:4 * w], lb) for sq in range(n_seq)]

    for c in range(tb // CHUNK):
        rows = slice(c * CHUNK, (c + 1) * CHUNK)
        for sq in range(n_seq):
            q, log_f, k, v, gate = gates[sq]
            gc = log_f[rows, :]
            g1 = gc.astype(BF16)
            r1 = gc - g1.astype(F32)
            g2 = r1.astype(BF16)
            g3 = (r1 - g2.astype(F32)).astype(BF16)
            b = _dot(tri, g1) + _dot(tri, g2) + _dot(tri, g3)
            b_last = b[CHUNK - 1:CHUNK, :]
            b_mid = b[CHUNK // 2 - 1:CHUNK // 2, :]
            qc, kc, vc = q[rows, :], k[rows, :], v[rows, :]
            heads = [slice(h * REC_DIM, (h + 1) * REC_DIM) for h in range(n_heads)]

            if direct:
                bsc_ref[...] = b
                ksc_ref[...] = kc

                def col_step(s, acc, b=b, qc=qc):
                    b_s = bsc_ref[pl.ds(s, 1), :]
                    wgt = qc * ksc_ref[pl.ds(s, 1), :] * jnp.exp(jnp.minimum(b - b_s, 0.0))
                    hit = causal & (col == s)
                    return tuple(jnp.where(hit, jnp.sum(wgt[:, hs], axis=1, keepdims=True), a_h)
                                 for hs, a_h in zip(heads, acc))

                a_heads = lax.fori_loop(0, CHUNK, col_step, (jnp.zeros((CHUNK, CHUNK), F32),) * n_heads)
            else:
                q_in = (qc * jnp.exp(b - b_mid)).astype(BF16)
                k_in = (kc * jnp.exp(b_mid - b)).astype(BF16)
                a_heads = tuple(jnp.where(causal, _dot_nt(q_in[:, hs], k_in[:, hs]), 0.0) for hs in heads)
            q_st = (qc * jnp.exp(b)).astype(BF16)
            k_st = (kc * jnp.exp(b_last - b)).astype(BF16)
            vb = vc.astype(BF16)
            decay = jnp.exp(b_last)
            for h in range(n_heads):
                cols = heads[h]
                st = st_ref[sq, h]
                o = _dot(a_heads[h].astype(BF16), vb[:, cols]) + _dot_nt(q_st[:, cols], st.astype(BF16))
                st_ref[sq, h] = st * decay[:, cols] + _dot_tn(vb[:, cols], k_st[:, cols])
                o = _rms(o, -1) * gn * gate[rows, cols]
                rec_ref[sq, rows, cols] = o.astype(rec_ref.dtype)

    @pl.when(pl.program_id(1) == pl.num_programs(1) - 1)
    def _():
        for sq in range(n_seq):
            for h in range(n_heads):
                sfin_ref[sq, h] = st_ref[sq, h].T


def _row_to_col(x_row):
    n = x_row.shape[1]
    eye = lax.broadcasted_iota(I32, (n, n), 0) == lax.broadcasted_iota(I32, (n, n), 1)
    return jnp.sum(jnp.where(eye, jnp.broadcast_to(x_row, (n, n)), 0.0), axis=1, keepdims=True)


def _col_to_row(x_col):
    n = x_col.shape[0]
    eye = lax.broadcasted_iota(I32, (n, n), 0) == lax.broadcasted_iota(I32, (n, n), 1)
    return jnp.sum(jnp.where(eye, jnp.broadcast_to(x_col, (n, n)), 0.0), axis=0, keepdims=True)


def _hgrn_step_kernel(lbs_ref, rec4_ref, gn_ref, s0_ref, rec_ref, s1_ref, *, layer, rec_width):
    n_heads = rec_width // REC_DIM
    lb = _lower_bound(lbs_ref, layer)
    w = rec_width
    r4 = rec4_ref[0]
    q, log_f, k, v, gate = _hgrn_gates(r4[:, 0:w], r4[:, w:2 * w], r4[:, 2 * w:3 * w], r4[:, 3 * w:4 * w], lb)
    f = jnp.exp(log_f)
    gn = gn_ref[...]
    for h in range(n_heads):
        cols = slice(h * REC_DIM, (h + 1) * REC_DIM)
        s1 = _row_to_col(f[:, cols]) * s0_ref[0, h] + _row_to_col(k[:, cols]) * v[:, cols]
        s1_ref[0, h] = s1
        o = jnp.sum(_row_to_col(q[:, cols]) * s1, axis=0, keepdims=True)
        rec_ref[0, :, cols] = (_rms(o, -1) * gn * gate[:, cols]).astype(rec_ref.dtype)


def _prompt_attention_kernel(iqa_ref, iqb_ref, aqa_ref, aqb_ref, iwa_ref, iwb_ref, ikb_ref, akb_ref, avb_ref,
                             oa_ref, ob_ref, iqs_ref, qpad_ref, iws_ref, keys_ref, bias_ref, logit_ref, pv_ref,
                             *, n_sel, n_units):
    p = pl.program_id(1)
    nq = 2 * pl.num_programs(1)
    j_of = (p, nq - 1 - p)
    n_a = p // 2 + 1
    n_heads = aqa_ref.shape[0] // ATT_HEAD_DIM
    group = n_heads // ATT_KV_HEADS
    half = KEY_TILE // 2
    cw = 2 * Q_TILE
    gw = group * Q_TILE
    ct_per_group = gw // cw
    radix_steps = 32

    for s, (iq_ref, aq_ref, iw_ref) in enumerate(((iqa_ref, aqa_ref, iwa_ref), (iqb_ref, aqb_ref, iwb_ref))):
        iqs_ref[s] = jnp.concatenate([iq_ref[h * IDX_DIM:(h + 1) * IDX_DIM, :] for h in range(IDX_HEADS)], axis=1)
        halves = [jnp.concatenate([aq_ref[(g * group + hh) * ATT_HEAD_DIM:(g * group + hh + 1) * ATT_HEAD_DIM, :]
                                   for hh in range(group)], axis=1) for g in range(ATT_KV_HEADS)]
        zero = jnp.zeros_like(halves[0])
        for g in range(ATT_KV_HEADS):
            qpad_ref[s, g] = jnp.concatenate([halves[g] if gg == g else zero for gg in range(ATT_KV_HEADS)], axis=0)
        iws_ref[s] = iw_ref[...]

    def unit(u):
        is_b = u >= n_a
        return is_b, jnp.where(is_b, 1, 0), jnp.where(is_b, u - n_a, u), jnp.where(is_b, j_of[1], j_of[0])

    def key_rows(kt, mc):
        return pl.ds(pl.multiple_of(kt * KEY_TILE + mc * half, half), half)

    row_h = lax.broadcasted_iota(I32, (half, Q_TILE), 0)
    lane_h = lax.broadcasted_iota(I32, (half, Q_TILE), 1)

    for u in range(n_units):
        _, s, kt, j = unit(u)
        iw = iws_ref[s]
        for mc in range(2):
            ik = ikb_ref[key_rows(kt, mc), :]
            sc = jnp.zeros((half, Q_TILE), F32)
            for ct in range(IDX_HEADS // 2):
                d = _dot(ik, iqs_ref[s, :, ct * cw:(ct + 1) * cw])
                for hh in range(2):
                    h = 2 * ct + hh
                    sc = sc + jnp.maximum(d[:, hh * Q_TILE:(hh + 1) * Q_TILE], 0.0) * iw[h:h + 1, :]
            causal = kt * KEY_TILE + mc * half + row_h <= j * Q_TILE + lane_h
            keys_ref[u, mc * half:(mc + 1) * half, :] = jnp.where(causal, sc, NEG_INF)

    n_pieces = n_units * 2 * ATT_KV_HEADS

    def qk_piece(piece):
        u, mc, g = piece // (2 * ATT_KV_HEADS), (piece // ATT_KV_HEADS) % 2, piece % ATT_KV_HEADS
        _, s, kt, _ = unit(u)
        logit_ref[u, mc, g] = _dot(akb_ref[key_rows(kt, mc), :], qpad_ref[s, g])

    zero8 = jnp.zeros((8, Q_TILE), I32)

    def count2(pred):
        tot = [zero8, zero8]
        for u in range(n_units):
            is_b, _, kt, _ = unit(u)
            m = pred(keys_ref[u], lambda a, b_: jnp.where(is_b, b_, a), kt)
            part = jnp.sum(jnp.where(m, 1, 0).reshape(KEY_TILE // 8, 8, Q_TILE), axis=0)
            tot[0] = tot[0] + jnp.where(is_b, zero8, part)
            tot[1] = tot[1] + jnp.where(is_b, part, zero8)
        return tuple(jnp.sum(t, axis=0, keepdims=True) for t in tot)

    def bit_step(i, carry):
        pref, at_thr = carry
        cand = tuple(pr | jnp.left_shift(jnp.int32(1), 31 - i) for pr in pref)
        cf = tuple(_radix_candidate(c) for c in cand)
        cnt = count2(lambda sc, pick, kt: sc >= pick(cf[0], cf[1]))
        take = tuple((cnt[t] >= n_sel) & (cf[t] > NEG_INF) for t in range(2))
        return (tuple(jnp.where(take[t], cand[t], pref[t]) for t in range(2)),
                tuple(jnp.where(take[t], cnt[t], at_thr[t]) for t in range(2)))

    steps_per_trip = 4
    trips = radix_steps // steps_per_trip
    pieces_per_trip = min(n_pieces // trips, steps_per_trip)

    def radix_trip(it, carry):
        for r in range(pieces_per_trip):
            qk_piece(it * pieces_per_trip + r)
        for r in range(steps_per_trip):
            carry = bit_step(it * steps_per_trip + r, carry)
        return carry

    zero_q = jnp.zeros((1, Q_TILE), I32)
    pref, at_thr = lax.fori_loop(0, trips, radix_trip, ((zero_q, zero_q), (zero_q, zero_q)))
    for piece in range(trips * pieces_per_trip, n_pieces):
        qk_piece(piece)
    found = tuple(pr != 0 for pr in pref)
    thr = tuple(jnp.where(found[t], _radix_candidate(pref[t]), NEG_INF) for t in range(2))

    above = count2(lambda sc, pick, kt: sc > pick(thr[0], thr[1]))
    need = tuple(n_sel - a for a in above)
    any_tie = jnp.max(jnp.maximum(jnp.where(at_thr[0] > n_sel, 1, 0), jnp.where(at_thr[1] > n_sel, 1, 0))) > 0
    pos_bits = max(1, (ikb_ref.shape[0] - 1).bit_length())
    row_k = lax.broadcasted_iota(I32, (KEY_TILE, Q_TILE), 0)

    def last_kept_tie():
        def step(i, pre):
            cand = tuple(pr | jnp.left_shift(jnp.int32(1), pos_bits - 1 - i) for pr in pre)
            cnt = count2(lambda sc, pick, kt: (sc == pick(thr[0], thr[1])) & (kt * KEY_TILE + row_k < pick(cand[0], cand[1])))
            return tuple(jnp.where(cnt[t] < need[t], cand[t], pre[t]) for t in range(2))
        return lax.fori_loop(0, pos_bits, step, (zero_q, zero_q))

    big = jnp.full((1, Q_TILE), INT_MAX, I32)
    last_tie = lax.cond(any_tie, last_kept_tie, lambda: (big, big))
    last_tie = tuple(jnp.where(found[t], last_tie[t], -1) for t in range(2))

    for u in range(n_units):
        is_b, _, kt, _ = unit(u)
        t_u = jnp.where(is_b, thr[1], thr[0])
        last_u = jnp.where(is_b, last_tie[1], last_tie[0])
        sc = keys_ref[u]
        keep = (sc > t_u) | ((sc == t_u) & (kt * KEY_TILE + row_k <= last_u))
        bias_ref[u] = jnp.where(keep, 0.0, MASKED)

    neg = jnp.full((8, gw), MASKED, F32)
    cmax = [[neg] * ATT_KV_HEADS, [neg] * ATT_KV_HEADS]
    for u in range(n_units):
        is_b, _, _, _ = unit(u)
        for mc in range(2):
            b_rows = bias_ref[u, mc * half:(mc + 1) * half, :]
            bias_g = jnp.concatenate([b_rows] * group, axis=1)
            for g in range(ATT_KV_HEADS):
                part = jnp.max((logit_ref[u, mc, g] + bias_g).reshape(half // 8, 8, gw), axis=0)
                cmax[0][g] = jnp.maximum(cmax[0][g], jnp.where(is_b, neg, part))
                cmax[1][g] = jnp.maximum(cmax[1][g], jnp.where(is_b, part, neg))
    cmax = [[jnp.max(c, axis=0, keepdims=True) for c in cm] for cm in cmax]

    zsum = jnp.zeros((8, cw), F32)
    lsum = [[[zsum] * ct_per_group for _ in range(ATT_KV_HEADS)] for _ in range(2)]
    for u in range(n_units):
        is_b, _, kt, _ = unit(u)
        for g in range(ATT_KV_HEADS):
            m_g = jnp.where(is_b, cmax[1][g], cmax[0][g])
            for ct in range(ct_per_group):
                cols = slice(ct * cw, (ct + 1) * cw)
                contrib = jnp.zeros((ATT_HEAD_DIM, cw), F32)
                for mc in range(2):
                    b_rows = bias_ref[u, mc * half:(mc + 1) * half, :]
                    bias_c = jnp.concatenate([b_rows] * (cw // Q_TILE), axis=1)
                    pr = jnp.exp(logit_ref[u, mc, g, :, cols] + bias_c - m_g[:, cols])
                    part = jnp.sum(pr.reshape(half // 8, 8, cw), axis=0)
                    lsum[0][g][ct] = lsum[0][g][ct] + jnp.where(is_b, zsum, part)
                    lsum[1][g][ct] = lsum[1][g][ct] + jnp.where(is_b, part, zsum)
                    v_t = avb_ref[kt, g * ATT_HEAD_DIM:(g + 1) * ATT_HEAD_DIM, mc * half:(mc + 1) * half]
                    contrib = contrib + _dot(v_t, pr.astype(BF16))
                pv_ref[u, g, :, cols] = contrib

    max_a = (n_units - 2) // 2 + 1
    for g in range(ATT_KV_HEADS):
        for ct in range(ct_per_group):
            cols = slice(ct * cw, (ct + 1) * cw)
            total = pv_ref[0, g, :, cols]
            for u in range(1, n_units):
                total = total + pv_ref[u, g, :, cols]
            part_a = pv_ref[0, g, :, cols]
            for u in range(1, max_a):
                part_a = part_a + jnp.where(u < n_a, pv_ref[u, g, :, cols], 0.0)
            for s, (o_ref, acc) in enumerate(((oa_ref, part_a), (ob_ref, total - part_a))):
                den = jnp.sum(lsum[s][g][ct], axis=0, keepdims=True)
                out_t = acc / den
                blk = jnp.concatenate([out_t[:, 0:Q_TILE], out_t[:, Q_TILE:cw]], axis=0)
                c_out = (g * ct_per_group + ct) * 2 * ATT_HEAD_DIM
                o_ref[0, 0, :, c_out:c_out + 2 * ATT_HEAD_DIM] = blk.T.astype(o_ref.dtype)


def _page_copy(pt_ref, src_hbm, layer, buf, sem, seq, slot, page):
    def copy(p):
        dst = buf.at[(slot,) + (slice(None),) * (len(buf.shape) - 2) + (pl.ds(pl.multiple_of(p * page, page), page),)]
        return pltpu.make_async_copy(src_hbm.at[layer, pt_ref[seq, p]], dst, sem.at[slot])
    return copy


def _start_pages(copy, n_pages):
    def body(p, c):
        copy(p).start()
        return c
    lax.fori_loop(0, n_pages, body, 0, unroll=8 if n_pages % 8 == 0 else 1)


def _wait_slot(buf, sem, slot):
    pltpu.make_async_copy(buf.at[1 - slot], buf.at[slot], sem.at[slot]).wait()


def _sample_scores_kernel(pt_ref, iq_ref, iw_ref, ikn_ref, kidx_hbm, sc_ref, buf, sem, *, layer, n_pages, page, tile):
    b = pl.program_id(0)
    nb = pl.num_programs(0)
    slot = b % 2

    @pl.when(b == 0)
    def _():
        _start_pages(_page_copy(pt_ref, kidx_hbm, layer, buf, sem, 0, 0, page), n_pages)

    @pl.when(b + 1 < nb)
    def _():
        _start_pages(_page_copy(pt_ref, kidx_hbm, layer, buf, sem, b + 1, 1 - slot, page), n_pages)

    _wait_slot(buf, sem, slot)

    iq = iq_ref[0]
    iqb = iq.astype(BF16)
    iw = iw_ref[0]
    past = n_pages * page
    for t in range(past // tile):
        kb = buf[slot, :, t * tile:(t + 1) * tile].astype(BF16)
        d = _dot(iqb, kb)
        sc_ref[0, :, t * tile:(t + 1) * tile] = jnp.sum(jnp.maximum(d, 0.0) * iw, axis=0, keepdims=True)
    d_new = jnp.sum(iq * ikn_ref[0], axis=1, keepdims=True)
    s_new = jnp.sum(jnp.maximum(d_new, 0.0) * iw, axis=0, keepdims=True)
    lane = lax.broadcasted_iota(I32, (1, V7X_LANES), 1)
    sc_ref[0, :, past:past + V7X_LANES] = jnp.where(lane == 0, s_new, 0.0)


def _sample_select_kernel(sc_ref, bias_ref, keys_ref, *, n_sel, past):
    rows, width = sc_ref.shape
    col = lax.broadcasted_iota(I32, (rows, width), 1)
    keys_ref[...] = jnp.where(col <= past, sc_ref[...], NEG_INF)

    def count(pred):
        return jnp.sum(pred(keys_ref[...]).astype(I32), axis=1, keepdims=True)

    def bit_step(i, carry):
        prefix, at_thr = carry
        cand = prefix | jnp.left_shift(jnp.int32(1), 31 - i)
        cf = _radix_candidate(cand)
        cnt = count(lambda sc: sc >= cf)
        take = (cnt >= n_sel) & (cf > NEG_INF)
        return jnp.where(take, cand, prefix), jnp.where(take, cnt, at_thr)

    zero = jnp.zeros((rows, 1), I32)
    prefix, _ = lax.fori_loop(0, 32, bit_step, (zero, zero))
    found = prefix != 0
    thr = jnp.where(found, _radix_candidate(prefix), NEG_INF)
    need = n_sel - count(lambda sc: sc > thr)
    pos_bits = max(1, (width - 1).bit_length())

    def pos_step(i, pre):
        cand = pre | jnp.left_shift(jnp.int32(1), pos_bits - 1 - i)
        cnt = count(lambda sc: (sc == thr) & (col < cand))
        return jnp.where(cnt < need, cand, pre)

    last_tie = jnp.where(found, lax.fori_loop(0, pos_bits, pos_step, zero), -1)
    sc = keys_ref[...]
    keep = (sc > thr) | ((sc == thr) & (col <= last_tie))
    bias_ref[...] = jnp.where(keep, 0.0, MASKED)


def _sample_attention_kernel(pt_ref, aq_ref, gq_ref, akn_ref, av_ref, bias_ref, k_hbm, v_hbm, o_ref,
                             kbuf, vbuf, logit_ref, ksem, vsem, *, layer, n_pages, page, tile):
    b = pl.program_id(0)
    nb = pl.num_programs(0)
    slot = b % 2

    def start(seq, s):
        _start_pages(_page_copy(pt_ref, k_hbm, layer, kbuf, ksem, seq, s, page), n_pages)
        _start_pages(_page_copy(pt_ref, v_hbm, layer, vbuf, vsem, seq, s, page), n_pages)

    @pl.when(b == 0)
    def _():
        start(0, 0)

    @pl.when(b + 1 < nb)
    def _():
        start(b + 1, 1 - slot)

    _wait_slot(kbuf, ksem, slot)
    _wait_slot(vbuf, vsem, slot)

    n_heads = aq_ref.shape[1]
    group = n_heads // ATT_KV_HEADS
    kvw = ATT_KV_HEADS * ATT_HEAD_DIM
    past = n_pages * page
    q = _rms(aq_ref[0], -1) * gq_ref[...]
    zeros = jnp.zeros_like(q)
    head = lax.broadcasted_iota(I32, (n_heads, kvw), 0)
    q_pad = jnp.where(head < group, jnp.concatenate([q, zeros], axis=1), jnp.concatenate([zeros, q], axis=1))
    q_pad_b = q_pad.astype(BF16)

    for t in range(past // tile):
        cols = slice(t * tile, (t + 1) * tile)
        kb = kbuf[slot, :, :, cols].reshape(kvw, tile).astype(BF16)
        logit_ref[:, cols] = _dot(q_pad_b, kb) + bias_ref[0, :, cols]
    lane = lax.broadcasted_iota(I32, (n_heads, V7X_LANES), 1)
    l_new = jnp.sum(q_pad * akn_ref[0], axis=1, keepdims=True) + bias_ref[0, :, past:past + 1]
    logit_ref[:, past:past + V7X_LANES] = jnp.where(lane == 0, l_new, MASKED)

    m = jnp.max(logit_ref[...], axis=1, keepdims=True)
    lsum = jnp.zeros((n_heads, 1), F32)
    acc = jnp.zeros((kvw, n_heads), F32)
    for t in range(past // tile):
        cols = slice(t * tile, (t + 1) * tile)
        p = jnp.exp(logit_ref[:, cols] - m)
        lsum = lsum + jnp.sum(p, axis=1, keepdims=True)
        acc = acc + _dot_nt(vbuf[slot, :, :, cols].reshape(kvw, tile).astype(BF16), p.astype(BF16))
    p_new = jnp.exp(logit_ref[:, past:past + 1] - m)
    acc = acc + _row_to_col(av_ref[0]) * _col_to_row(p_new)
    o_ref[0] = acc / _col_to_row(lsum + p_new)


def _outproj_mlp_kernel(x_ref, att_ref, rec_ref, wo_ref, g2_ref, wup_ref, wdn_ref, o_ref, *, ff_tile):
    mix = jnp.concatenate([att_ref[...].astype(BF16), rec_ref[...].astype(BF16)], axis=1)
    y = x_ref[...] + _dot(mix, wo_ref[...])
    h2 = (_rms(y, -1) * g2_ref[...]).astype(BF16)
    acc = y
    for c in range(wup_ref.shape[1] // ff_tile):
        u = jnp.maximum(_dot(h2, wup_ref[:, c * ff_tile:(c + 1) * ff_tile]), 0.0)
        acc = acc + _dot((u * u).astype(BF16), wdn_ref[c * ff_tile:(c + 1) * ff_tile, :])
    o_ref[...] = acc


def _row_tile(n_rows, want):
    return want if n_rows % want == 0 else n_rows


def _layer_weights(w_in, att_width, kv_width, idx_width, rec_width):
    o = [0]
    for wdt in (att_width, kv_width, kv_width, idx_width, IDX_DIM, IDX_HEADS, rec_width, rec_width, rec_width, rec_width):
        o.append(o[-1] + wdt)
    aq, ak, av, iq, ik, iw, rq, rf, ri, rg = (w_in[:, o[i]:o[i + 1]] for i in range(10))
    d = w_in.shape[0]
    z = lambda n: jnp.zeros((d, n), w_in.dtype)
    w_main = jnp.concatenate([ak, ik, z(128 - IDX_DIM), rq, rf, ri, rg], axis=1).astype(BF16)
    w_tok = jnp.concatenate([aq, iq, av, ak, ik, iw, z(16 - IDX_HEADS)], axis=1).T.astype(BF16)
    w_qry = jnp.concatenate([aq, iq, av, iw, z(128 - IDX_HEADS)], axis=1).astype(BF16)
    return w_main, w_tok, w_qry


def _outproj_mlp(x, att, rec, w_out, g2, w_up, w_dn, name):
    n, d = x.shape
    tm = _row_tile(n, 512)
    ff = w_up.shape[1]
    return pl.pallas_call(
        functools.partial(_outproj_mlp_kernel, ff_tile=min(ff, 1024)),
        grid=(n // tm,),
        in_specs=[pl.BlockSpec((tm, d), lambda i: (i, 0)),
                  pl.BlockSpec((tm, att.shape[1]), lambda i: (i, 0)),
                  pl.BlockSpec((tm, rec.shape[1]), lambda i: (i, 0)),
                  _resident(w_out.shape), _resident(g2.shape), _resident(w_up.shape), _resident(w_dn.shape)],
        out_specs=pl.BlockSpec((tm, d), lambda i: (i, 0)),
        out_shape=jax.ShapeDtypeStruct((n, d), F32),
        compiler_params=_params("arbitrary"),
        name=name,
    )(x, att, rec, w_out, g2, w_up, w_dn)


def _prompt_attention(iqt, aqt, iwt, ikb, akb, avb, bsz, seq):
    n = bsz * seq
    kvw = ATT_KV_HEADS * ATT_HEAD_DIM
    att_width, idx_width = aqt.shape[0], iqt.shape[0]
    n_sel = min(TOPK_MAX, seq // 4)
    nq = seq // Q_TILE
    assert KEY_TILE == 2 * Q_TILE and nq % 2 == 0, "query tiles are paired so every step sees nq/2 + 1 key tiles"
    n_pairs = nq // 2
    n_units = n_pairs + 1
    kt_per_seq = seq // KEY_TILE
    avb_tiles = avb.reshape(kvw, n // KEY_TILE, KEY_TILE).transpose(1, 0, 2)
    n_heads = att_width // ATT_HEAD_DIM
    gw2 = n_heads * Q_TILE
    qa = lambda h: pl.BlockSpec((h, Q_TILE), lambda b, p: (0, b * nq + p))
    qb = lambda h: pl.BlockSpec((h, Q_TILE), lambda b, p: (0, b * nq + nq - 1 - p))
    outb = pl.BlockSpec((1, 1, Q_TILE, att_width), lambda b, p: (b, p, 0, 0))
    att_a, att_b = pl.pallas_call(
        functools.partial(_prompt_attention_kernel, n_sel=n_sel, n_units=n_units),
        grid=(bsz, n_pairs),
        in_specs=[qa(idx_width), qb(idx_width), qa(att_width), qb(att_width), qa(IDX_HEADS), qb(IDX_HEADS),
                  pl.BlockSpec((seq, IDX_DIM), lambda b, p: (b, 0)),
                  pl.BlockSpec((seq, kvw), lambda b, p: (b, 0)),
                  pl.BlockSpec((kt_per_seq, kvw, KEY_TILE), lambda b, p: (b, 0, 0))],
        out_specs=[outb, outb],
        out_shape=[jax.ShapeDtypeStruct((bsz, n_pairs, Q_TILE, att_width), BF16)] * 2,
        scratch_shapes=[pltpu.VMEM((2, IDX_DIM, IDX_HEADS * Q_TILE), BF16),
                        pltpu.VMEM((2, ATT_KV_HEADS, kvw, gw2 // ATT_KV_HEADS), BF16),
                        pltpu.VMEM((2, IDX_HEADS, Q_TILE), F32),
                        pltpu.VMEM((n_units, KEY_TILE, Q_TILE), F32),
                        pltpu.VMEM((n_units, KEY_TILE, Q_TILE), F32),
                        pltpu.VMEM((n_units, 2, ATT_KV_HEADS, KEY_TILE // 2, gw2 // ATT_KV_HEADS), F32),
                        pltpu.VMEM((n_units, ATT_KV_HEADS, ATT_HEAD_DIM, gw2 // ATT_KV_HEADS), F32)],
        compiler_params=_params("arbitrary", "arbitrary"),
        name="attention_prompt",
    )(iqt, iqt, aqt, aqt, iwt, iwt, ikb, akb, avb_tiles)
    return jnp.concatenate([att_a, att_b[:, ::-1]], axis=1).reshape(n, att_width)


def _prompt_layer(x, lbs, layer, w_main, w_tok, g1, gq_col, gk_row, gi_row, gk_col, gi_col, gn_row, w_out, g2,
                  w_up, w_dn, att_width, idx_width, rec_width, idx_w_scale):
    bsz, seq, d = x.shape
    n = bsz * seq
    xf = x.reshape(n, d)
    tm = _row_tile(seq, 512)
    tps = seq // tm
    kvw = ATT_KV_HEADS * ATT_HEAD_DIM
    row = lambda w: pl.BlockSpec((tm, w), lambda i: (i, 0))
    colb = lambda h: pl.BlockSpec((h, tm), lambda i: (0, i))
    seqb = lambda h: pl.BlockSpec((1, h, tm), lambda i: (i // tps, 0, i % tps))
    akb, ikb, rec4, aqt, iqt, avb, iwt, akt, avt, ikt = pl.pallas_call(
        functools.partial(_inproj_prompt_kernel, idx_w_scale=idx_w_scale),
        grid=(n // tm,),
        in_specs=[row(d), _resident(g1.shape), _resident(w_main.shape), _resident(w_tok.shape),
                  _resident(gk_row.shape), _resident(gi_row.shape), _resident(gq_col.shape),
                  _resident(gk_col.shape), _resident(gi_col.shape)],
        out_specs=[row(kvw), row(IDX_DIM), row(4 * rec_width),
                   colb(att_width), colb(idx_width), colb(kvw), colb(IDX_HEADS),
                   seqb(kvw), seqb(kvw), seqb(IDX_DIM)],
        out_shape=[jax.ShapeDtypeStruct((n, kvw), BF16), jax.ShapeDtypeStruct((n, IDX_DIM), BF16),
                   jax.ShapeDtypeStruct((n, 4 * rec_width), F32),
                   jax.ShapeDtypeStruct((att_width, n), BF16), jax.ShapeDtypeStruct((idx_width, n), BF16),
                   jax.ShapeDtypeStruct((kvw, n), BF16), jax.ShapeDtypeStruct((IDX_HEADS, n), F32),
                   jax.ShapeDtypeStruct((bsz, kvw, seq), F32), jax.ShapeDtypeStruct((bsz, kvw, seq), F32),
                   jax.ShapeDtypeStruct((bsz, IDX_DIM, seq), F32)],
        compiler_params=_params("arbitrary"),
        name="inproj_prompt",
    )(xf, g1, w_main, w_tok, gk_row, gi_row, gq_col, gk_col, gi_col)

    tb = _row_tile(seq, 256)
    n_rec = rec_width // REC_DIM
    tiles = seq // tb
    gs = 2 if bsz % 2 == 0 else 1

    def hgrn(direct):
        return pl.pallas_call(
            functools.partial(_hgrn_prompt_kernel, layer=layer, rec_width=rec_width, direct=direct),
            grid=(bsz // gs, tiles),
            in_specs=[_resident(lbs.shape), pl.BlockSpec((gs, tb, 4 * rec_width), lambda b, t: (b, t, 0)),
                      _resident(gn_row.shape)],
            out_specs=[pl.BlockSpec((gs, tb, rec_width), lambda b, t: (b, t, 0)),
                       pl.BlockSpec((gs, n_rec, REC_DIM, REC_DIM), lambda b, t: (b, 0, 0, 0))],
            out_shape=[jax.ShapeDtypeStruct((bsz, seq, rec_width), BF16),
                       jax.ShapeDtypeStruct((bsz, n_rec, REC_DIM, REC_DIM), F32)],
            scratch_shapes=[pltpu.VMEM((gs, n_rec, REC_DIM, REC_DIM), F32),
                            pltpu.VMEM((CHUNK, rec_width), F32), pltpu.VMEM((CHUNK, rec_width), F32)],
            compiler_params=_params("arbitrary", "arbitrary"),
            name="hgrn_prompt_direct" if direct else "hgrn_prompt",
        )

    lb_min = jnp.min(jnp.cumsum(jax.nn.softmax(lbs, axis=0), axis=0)[layer])
    factorable = lb_min >= float(np.exp(-HGRN_SAFE_SPAN / (CHUNK // 2)))
    rec, s_fin = lax.cond(factorable, hgrn(False), hgrn(True), lbs, rec4.reshape(bsz, seq, 4 * rec_width), gn_row)
    rec = rec.reshape(n, rec_width)

    att = _prompt_attention(iqt, aqt, iwt, ikb, akb, avb, bsz, seq)

    y = _outproj_mlp(xf, att, rec, w_out, g2, w_up, w_dn, "outproj_mlp_prompt")
    to_heads = lambda t: t.reshape(bsz, ATT_KV_HEADS, ATT_HEAD_DIM, seq).transpose(0, 3, 1, 2)
    return y.reshape(bsz, seq, d), to_heads(akt), to_heads(avt), ikt.transpose(0, 2, 1), s_fin


def _sample_layer(x, s0, lbs, layer, cache_k, cache_v, cache_kidx, page_table, w_main, w_qry, g1, gq_row,
                  gk_row, gi_row, gn_row, w_out, g2, w_up, w_dn, att_width, idx_width, rec_width, idx_w_scale):
    bd, t_new, d = x.shape
    assert t_new == 1, "the sample path handles one new token per sequence"
    kvw = ATT_KV_HEADS * ATT_HEAD_DIM
    n_heads = att_width // ATT_HEAD_DIM
    n_rec = rec_width // REC_DIM
    page = cache_k.shape[2]
    n_pages = page_table.shape[1]
    past = n_pages * page
    width = past + V7X_LANES
    tile = 2048 if past % 2048 == 0 else past
    xf = x.reshape(bd, d)

    full = lambda shape: _resident(shape)
    akn, av, ikn, rec4, aq, iq, iw = pl.pallas_call(
        functools.partial(_inproj_sample_kernel, idx_w_scale=idx_w_scale),
        grid=(1,),
        in_specs=[full(xf.shape), full(g1.shape), full(w_main.shape), full(w_qry.shape),
                  full(gk_row.shape), full(gi_row.shape)],
        out_specs=[full((bd, kvw)), full((bd, kvw)), full((bd, IDX_DIM)), full((bd, 4 * rec_width)),
                   full((bd, att_width)), full((bd, idx_width)), full((bd, 128))],
        out_shape=[jax.ShapeDtypeStruct((bd, kvw), F32), jax.ShapeDtypeStruct((bd, kvw), F32),
                   jax.ShapeDtypeStruct((bd, IDX_DIM), F32), jax.ShapeDtypeStruct((bd, 4 * rec_width), F32),
                   jax.ShapeDtypeStruct((bd, att_width), F32), jax.ShapeDtypeStruct((bd, idx_width), F32),
                   jax.ShapeDtypeStruct((bd, 128), F32)],
        compiler_params=_params("arbitrary"),
        name="inproj_sample",
    )(xf, g1, w_main, w_qry, gk_row, gi_row)

    rec, s1 = pl.pallas_call(
        functools.partial(_hgrn_step_kernel, layer=layer, rec_width=rec_width),
        grid=(bd,),
        in_specs=[_resident(lbs.shape), pl.BlockSpec((1, 1, 4 * rec_width), lambda b: (b, 0, 0)),
                  _resident(gn_row.shape), pl.BlockSpec((1, n_rec, REC_DIM, REC_DIM), lambda b: (b, 0, 0, 0))],
        out_specs=[pl.BlockSpec((1, 1, rec_width), lambda b: (b, 0, 0)),
                   pl.BlockSpec((1, n_rec, REC_DIM, REC_DIM), lambda b: (b, 0, 0, 0))],
        out_shape=[jax.ShapeDtypeStruct((bd, 1, rec_width), BF16),
                   jax.ShapeDtypeStruct((bd, n_rec, REC_DIM, REC_DIM), F32)],
        compiler_params=_params("arbitrary"),
        name="hgrn_step",
    )(lbs, rec4.reshape(bd, 1, 4 * rec_width), gn_row, s0)

    kidx_t = cache_kidx.transpose(0, 1, 3, 2)
    k_t = cache_k.transpose(0, 1, 3, 4, 2)
    v_t = cache_v.transpose(0, 1, 3, 4, 2)
    iq3 = iq.reshape(bd, IDX_HEADS, IDX_DIM)
    iw3 = iw[:, 0:IDX_HEADS].reshape(bd, IDX_HEADS, 1)
    ikn3 = ikn.reshape(bd, 1, IDX_DIM)
    scores = pl.pallas_call(
        functools.partial(_sample_scores_kernel, layer=layer, n_pages=n_pages, page=page, tile=tile),
        grid_spec=pltpu.PrefetchScalarGridSpec(
            num_scalar_prefetch=1,
            grid=(bd,),
            in_specs=[pl.BlockSpec((1, IDX_HEADS, IDX_DIM), lambda b, pt: (b, 0, 0)),
                      pl.BlockSpec((1, IDX_HEADS, 1), lambda b, pt: (b, 0, 0)),
                      pl.BlockSpec((1, 1, IDX_DIM), lambda b, pt: (b, 0, 0)),
                      pl.BlockSpec(memory_space=pl.ANY)],
            out_specs=pl.BlockSpec((1, 1, width), lambda b, pt: (b, 0, 0)),
            scratch_shapes=[pltpu.VMEM((2, IDX_DIM, past), F32), pltpu.SemaphoreType.DMA((2,))]),
        out_shape=jax.ShapeDtypeStruct((bd, 1, width), F32),
        compiler_params=_params("arbitrary"),
        name="sample_scores",
    )(page_table, iq3, iw3, ikn3, kidx_t)

    n_sel = min(TOPK_MAX, (past + 1) // 4)
    bias = pl.pallas_call(
        functools.partial(_sample_select_kernel, n_sel=n_sel, past=past),
        grid=(1,),
        in_specs=[full((bd, width))],
        out_specs=full((bd, width)),
        out_shape=jax.ShapeDtypeStruct((bd, width), F32),
        scratch_shapes=[pltpu.VMEM((bd, width), F32)],
        compiler_params=_params("arbitrary"),
        name="sample_select",
    )(scores.reshape(bd, width))

    att_t = pl.pallas_call(
        functools.partial(_sample_attention_kernel, layer=layer, n_pages=n_pages, page=page, tile=tile),
        grid_spec=pltpu.PrefetchScalarGridSpec(
            num_scalar_prefetch=1,
            grid=(bd,),
            in_specs=[pl.BlockSpec((1, n_heads, ATT_HEAD_DIM), lambda b, pt: (b, 0, 0)),
                      pl.BlockSpec((1, ATT_HEAD_DIM), lambda b, pt: (0, 0)),
                      pl.BlockSpec((1, 1, kvw), lambda b, pt: (b, 0, 0)),
                      pl.BlockSpec((1, 1, kvw), lambda b, pt: (b, 0, 0)),
                      pl.BlockSpec((1, 1, width), lambda b, pt: (b, 0, 0)),
                      pl.BlockSpec(memory_space=pl.ANY), pl.BlockSpec(memory_space=pl.ANY)],
            out_specs=pl.BlockSpec((1, kvw, n_heads), lambda b, pt: (b, 0, 0)),
            scratch_shapes=[pltpu.VMEM((2, ATT_KV_HEADS, ATT_HEAD_DIM, past), F32),
                            pltpu.VMEM((2, ATT_KV_HEADS, ATT_HEAD_DIM, past), F32),
                            pltpu.VMEM((n_heads, width), F32),
                            pltpu.SemaphoreType.DMA((2,)), pltpu.SemaphoreType.DMA((2,))]),
        out_shape=jax.ShapeDtypeStruct((bd, kvw, n_heads), F32),
        compiler_params=_params("arbitrary"),
        name="sample_attention",
    )(page_table, aq.reshape(bd, n_heads, ATT_HEAD_DIM), gq_row, akn.reshape(bd, 1, kvw), av.reshape(bd, 1, kvw),
      bias.reshape(bd, 1, width), k_t, v_t)
    att_g = att_t.reshape(bd, ATT_KV_HEADS, ATT_HEAD_DIM, n_heads)
    group = n_heads // ATT_KV_HEADS
    att = jnp.stack([att_g[:, h // group, :, h] for h in range(n_heads)], axis=1)

    y = _outproj_mlp(xf, att.reshape(bd, att_width), rec.reshape(bd, rec_width), w_out, g2, w_up, w_dn,
                     "outproj_mlp_sample")
    return (y.reshape(bd, 1, d), akn.reshape(bd, 1, ATT_KV_HEADS, ATT_HEAD_DIM),
            av.reshape(bd, 1, ATT_KV_HEADS, ATT_HEAD_DIM), ikn.reshape(bd, 1, IDX_DIM), s1)


def kernel(x_prompt, x_sample, cache_k, cache_v, cache_kidx, state_hgrn, page_table, norm1_g, w_in, q_norm_g,
           k_norm_g, idx_k_norm_g, lower_bounds, rec_norm_g, w_out, norm2_g, w_up, w_down):
    depth = w_in.shape[0]
    d_model = x_prompt.shape[-1]
    kv_width = ATT_KV_HEADS * ATT_HEAD_DIM
    idx_width = IDX_HEADS * IDX_DIM
    rec_width = lower_bounds.shape[-1]
    att_width = w_out.shape[1] - rec_width
    assert w_in.shape[-1] == att_width + 2 * kv_width + idx_width + IDX_DIM + IDX_HEADS + 4 * rec_width
    assert rec_norm_g.shape[-1] == REC_DIM and q_norm_g.shape[-1] == ATT_HEAD_DIM
    idx_w_scale = (IDX_HEADS ** -0.5) * (IDX_DIM ** -0.5)
    att_scale = ATT_HEAD_DIM ** -0.5
    n_heads = att_width // ATT_HEAD_DIM
    lbs = lower_bounds.astype(F32)

    yp, ys = x_prompt, x_sample
    outs = [[] for _ in range(8)]
    for l in range(depth):
        w_main, w_tok, w_qry = _layer_weights(w_in[l], att_width, kv_width, idx_width, rec_width)
        g1 = norm1_g[l].reshape(1, d_model)
        g2 = norm2_g[l].reshape(1, d_model)
        gq_row = (q_norm_g[l] * att_scale).reshape(1, ATT_HEAD_DIM)
        gq_col = jnp.tile(q_norm_g[l] * att_scale, n_heads).reshape(att_width, 1)
        gk_row = jnp.tile(k_norm_g[l], ATT_KV_HEADS).reshape(1, kv_width)
        gi_row = idx_k_norm_g[l].reshape(1, IDX_DIM)
        gk_col = gk_row.reshape(kv_width, 1)
        gi_col = gi_row.reshape(IDX_DIM, 1)
        gn_row = rec_norm_g[l].reshape(1, REC_DIM)
        wo, wu, wd = w_out[l].astype(BF16), w_up[l].astype(BF16), w_down[l].astype(BF16)
        yp, kp, vp, ip, sp = _prompt_layer(yp, lbs, l, w_main, w_tok, g1, gq_col, gk_row, gi_row, gk_col, gi_col,
                                           gn_row, wo, g2, wu, wd, att_width, idx_width, rec_width, idx_w_scale)
        ys, k_s, v_s, i_s, s_s = _sample_layer(ys, state_hgrn[l], lbs, l, cache_k, cache_v, cache_kidx,
                                               page_table, w_main, w_qry, g1, gq_row, gk_row, gi_row, gn_row, wo,
                                               g2, wu, wd, att_width, idx_width, rec_width, idx_w_scale)
        for acc, val in zip(outs, (kp, vp, ip, sp, k_s, v_s, i_s, s_s)):
            acc.append(val)
    return (yp, ys) + tuple(jnp.stack(o) for o in outs)
```

```python
import functools

import jax
import jax.numpy as jnp
import numpy as np
from jax import lax
from jax.experimental import pallas as pl
from jax.experimental.pallas import tpu as pltpu

F32 = jnp.float32
BF16 = jnp.bfloat16
I32 = jnp.int32

EPS = 1e-6
MASKED = -1e30
INT_MIN = -(2 ** 31)
NEG_INF = float("-inf")
INT_MAX = 2 ** 31 - 1

ATT_HEAD_DIM = 64
ATT_KV_HEADS = 2
IDX_HEADS = 8
IDX_DIM = 64
REC_DIM = 128
TOPK_MAX = 256
CHUNK = 64
HGRN_SAFE_SPAN = 60.0

V7X_LANES = 128
V7X_VMEM_LIMIT_BYTES = 56 * 1024 * 1024
Q_TILE = 128
KEY_TILE = 256


def _dot(a, b):
    return jnp.dot(a, b, preferred_element_type=F32)


def _dot_nt(a, b):
    return lax.dot_general(a, b, (((1,), (1,)), ((), ())), preferred_element_type=F32)


def _dot_tn(a, b):
    return lax.dot_general(a, b, (((0,), (0,)), ((), ())), preferred_element_type=F32)


def _params(*semantics):
    return pltpu.CompilerParams(dimension_semantics=semantics, vmem_limit_bytes=V7X_VMEM_LIMIT_BYTES)


def _resident(shape):
    nd = len(shape)
    return pl.BlockSpec(shape, lambda *_: (0,) * nd)


def _rms(x, axis):
    return x * lax.rsqrt(jnp.mean(x * x, axis=axis, keepdims=True) + EPS)


def _sigmoid_pair(x):
    t = jnp.exp(-jnp.abs(x))
    r = 1.0 / (1.0 + t)
    tr = t * r
    pos = x >= 0
    return jnp.where(pos, r, tr), jnp.where(pos, tr, r)


def _silu(x):
    s, _ = _sigmoid_pair(x)
    return x * s


def _radix_candidate(prefix_bits):
    key = prefix_bits ^ INT_MIN
    return lax.bitcast_convert_type(key ^ ((key >> 31) & INT_MAX), F32)


def _lower_bound(lbs_ref, layer):
    lbs = lbs_ref[...]
    e = jnp.exp(lbs - jnp.max(lbs, axis=0, keepdims=True))
    p = e / jnp.sum(e, axis=0, keepdims=True)
    return jnp.sum(p[0:layer + 1, :], axis=0, keepdims=True)


def _head_rms_lanes(x, gain_row):
    sq = x * x
    lane = lax.broadcasted_iota(I32, x.shape, 1)
    lo = lane < ATT_HEAD_DIM
    s_lo = jnp.sum(jnp.where(lo, sq, 0.0), axis=-1, keepdims=True)
    s_hi = jnp.sum(jnp.where(lo, 0.0, sq), axis=-1, keepdims=True)
    inv = jnp.where(lo, lax.rsqrt(s_lo / ATT_HEAD_DIM + EPS), lax.rsqrt(s_hi / ATT_HEAD_DIM + EPS))
    return x * inv * gain_row


def _inproj_prompt_kernel(x_ref, g1_ref, wm_ref, wt_ref, gk_ref, gi_ref, gq_ref, gkc_ref, gic_ref,
                          akb_ref, ikb_ref, rec4_ref, aqt_ref, iqt_ref, avb_ref, iwt_ref, akt_ref, avt_ref, ikt_ref,
                          *, idx_w_scale):
    x = x_ref[...]
    hb = (_rms(x, -1) * g1_ref[...]).astype(BF16)
    y = _dot(hb, wm_ref[...])
    akb_ref[...] = _head_rms_lanes(y[:, 0:128], gk_ref[...]).astype(BF16)
    ikb_ref[...] = (_rms(y[:, 128:128 + IDX_DIM], -1) * gi_ref[...]).astype(BF16)
    rec4_ref[...] = y[:, 256:]

    yt = _dot_nt(wt_ref[...], hb)
    n_q = aqt_ref.shape[0] // ATT_HEAD_DIM
    for h in range(n_q):
        rows = slice(h * ATT_HEAD_DIM, (h + 1) * ATT_HEAD_DIM)
        aqt_ref[rows, :] = (_rms(yt[rows, :], 0) * gq_ref[rows, :]).astype(BF16)
    o = aqt_ref.shape[0]
    iqt_ref[...] = yt[o:o + iqt_ref.shape[0], :].astype(BF16)
    o += iqt_ref.shape[0]
    kvw = ATT_KV_HEADS * ATT_HEAD_DIM
    av_t = yt[o:o + kvw, :]
    avt_ref[0] = av_t
    avb_ref[...] = av_t.astype(BF16)
    o += kvw
    for g in range(ATT_KV_HEADS):
        rows = slice(g * ATT_HEAD_DIM, (g + 1) * ATT_HEAD_DIM)
        akt_ref[0, rows, :] = _rms(yt[o + g * ATT_HEAD_DIM:o + (g + 1) * ATT_HEAD_DIM, :], 0) * gkc_ref[rows, :]
    o += kvw
    ikt_ref[0] = _rms(yt[o:o + IDX_DIM, :], 0) * gic_ref[...]
    o += IDX_DIM
    iwt_ref[...] = yt[o:o + IDX_HEADS, :] * idx_w_scale


def _inproj_sample_kernel(x_ref, g1_ref, wm_ref, wq_ref, gk_ref, gi_ref,
                          akn_ref, av_ref, ikn_ref, rec4_ref, aq_ref, iq_ref, iw_ref, *, idx_w_scale):
    x = x_ref[...]
    hb = (_rms(x, -1) * g1_ref[...]).astype(BF16)
    y = _dot(hb, wm_ref[...])
    akn_ref[...] = _head_rms_lanes(y[:, 0:128], gk_ref[...])
    ikn_ref[...] = _rms(y[:, 128:128 + IDX_DIM], -1) * gi_ref[...]
    rec4_ref[...] = y[:, 256:]
    yq = _dot(hb, wq_ref[...])
    wa = aq_ref.shape[1]
    wi = iq_ref.shape[1]
    aq_ref[...] = yq[:, 0:wa]
    iq_ref[...] = yq[:, wa:wa + wi]
    av_ref[...] = yq[:, wa + wi:wa + wi + 128]
    iw_ref[...] = yq[:, wa + wi + 128:wa + wi + 256] * idx_w_scale


def _hgrn_gates(rq, rf, ri, rg, lb):
    sp, sn = _sigmoid_pair(rf)
    q = _silu(rq) * (REC_DIM ** -0.5)
    log_f = jnp.log(lb + (1.0 - lb) * sp)
    k = (1.0 - lb) * sn
    return q, log_f, k, ri, _silu(rg)


def _hgrn_prompt_kernel(lbs_ref, rec4_ref, gn_ref, rec_ref, sfin_ref, st_ref, bsc_ref, ksc_ref, *, layer, rec_width,
                        direct):
    n_heads = rec_width // REC_DIM
    n_seq, tb = rec4_ref.shape[0], rec4_ref.shape[1]

    @pl.when(pl.program_id(1) == 0)
    def _():
        st_ref[...] = jnp.zeros_like(st_ref)

    lb = _lower_bound(lbs_ref, layer)
    w = rec_width
    row = lax.broadcasted_iota(I32, (CHUNK, CHUNK), 0)
    col = lax.broadcasted_iota(I32, (CHUNK, CHUNK), 1)
    causal = row >= col
    tri = jnp.where(causal, 1.0, 0.0).astype(BF16)
    gn = gn_ref[...]
    gates = [_hgrn_gates(rec4_ref[sq, :, 0:w], rec4_ref[sq, :, w:2 * w], rec4_ref[sq, :, 2 * w:3 * w],
                         rec4_ref[sq, :, 3 * w:4 * w], lb) for sq in range(n_seq)]

    for c in range(tb // CHUNK):
        rows = slice(c * CHUNK, (c + 1) * CHUNK)
        for sq in range(n_seq):
            q, log_f, k, v, gate = gates[sq]
            gc = log_f[rows, :]
            g1 = gc.astype(BF16)
            r1 = gc - g1.astype(F32)
            g2 = r1.astype(BF16)
            g3 = (r1 - g2.astype(F32)).astype(BF16)
            b = _dot(tri, g1) + _dot(tri, g2) + _dot(tri, g3)
            b_last = b[CHUNK - 1:CHUNK, :]
            b_mid = b[CHUNK // 2 - 1:CHUNK // 2, :]
            qc, kc, vc = q[rows, :], k[rows, :], v[rows, :]
            heads = [slice(h * REC_DIM, (h + 1) * REC_DIM) for h in range(n_heads)]

            if direct:
                bsc_ref[...] = b
                ksc_ref[...] = kc

                def col_step(s, acc, b=b, qc=qc):
                    b_s = bsc_ref[pl.ds(s, 1), :]
                    wgt = qc * ksc_ref[pl.ds(s, 1), :] * jnp.exp(jnp.minimum(b - b_s, 0.0))
                    hit = causal & (col == s)
                    return tuple(jnp.where(hit, jnp.sum(wgt[:, hs], axis=1, keepdims=True), a_h)
                                 for hs, a_h in zip(heads, acc))

                a_heads = lax.fori_loop(0, CHUNK, col_step, (jnp.zeros((CHUNK, CHUNK), F32),) * n_heads)
            else:
                q_in = (qc * jnp.exp(b - b_mid)).astype(BF16)
                k_in = (kc * jnp.exp(b_mid - b)).astype(BF16)
                a_heads = tuple(jnp.where(causal, _dot_nt(q_in[:, hs], k_in[:, hs]), 0.0) for hs in heads)
            q_st = (qc * jnp.exp(b)).astype(BF16)
            k_st = (kc * jnp.exp(b_last - b)).astype(BF16)
            vb = vc.astype(BF16)
            decay = jnp.exp(b_last)
            for h in range(n_heads):
                cols = heads[h]
                st = st_ref[sq, h]
                o = _dot(a_heads[h].astype(BF16), vb[:, cols]) + _dot_nt(q_st[:, cols], st.astype(BF16))
                st_ref[sq, h] = st * decay[:, cols] + _dot_tn(vb[:, cols], k_st[:, cols])
                o = _rms(o, -1) * gn * gate[rows, cols]
                rec_ref[sq, rows, cols] = o.astype(rec_ref.dtype)

    @pl.when(pl.program_id(1) == pl.num_programs(1) - 1)
    def _():
        for sq in range(n_seq):
            for h in range(n_heads):
                sfin_ref[sq, h] = st_ref[sq, h].T


def _row_to_col(x_row):
    n = x_row.shape[1]
    eye = lax.broadcasted_iota(I32, (n, n), 0) == lax.broadcasted_iota(I32, (n, n), 1)
    return jnp.sum(jnp.where(eye, jnp.broadcast_to(x_row, (n, n)), 0.0), axis=1, keepdims=True)


def _col_to_row(x_col):
    n = x_col.shape[0]
    eye = lax.broadcasted_iota(I32, (n, n), 0) == lax.broadcasted_iota(I32, (n, n), 1)
    return jnp.sum(jnp.where(eye, jnp.broadcast_to(x_col, (n, n)), 0.0), axis=0, keepdims=True)


def _hgrn_step_kernel(lbs_ref, rec4_ref, gn_ref, s0_ref, rec_ref, s1_ref, *, layer, rec_width):
    n_heads = rec_width // REC_DIM
    lb = _lower_bound(lbs_ref, layer)
    w = rec_width
    gn = gn_ref[...]
    for sq in range(rec4_ref.shape[0]):
        r4 = rec4_ref[sq]
        q, log_f, k, v, gate = _hgrn_gates(r4[:, 0:w], r4[:, w:2 * w], r4[:, 2 * w:3 * w], r4[:, 3 * w:4 * w], lb)
        f = jnp.exp(log_f)
        for h in range(n_heads):
            cols = slice(h * REC_DIM, (h + 1) * REC_DIM)
            s1 = _row_to_col(f[:, cols]) * s0_ref[sq, h] + _row_to_col(k[:, cols]) * v[:, cols]
            s1_ref[sq, h] = s1
            o = jnp.sum(_row_to_col(q[:, cols]) * s1, axis=0, keepdims=True)
            rec_ref[sq, :, cols] = (_rms(o, -1) * gn * gate[:, cols]).astype(rec_ref.dtype)


def _prompt_attention_kernel(iqa_ref, iqb_ref, aqa_ref, aqb_ref, iwa_ref, iwb_ref, ikb_ref, akb_ref, avb_ref,
                             oa_ref, ob_ref, iqs_ref, qpad_ref, iws_ref, keys_ref, bias_ref, logit_ref, pv_ref,
                             *, n_sel, n_units):
    p = pl.program_id(1)
    nq = 2 * pl.num_programs(1)
    j_of = (p, nq - 1 - p)
    n_a = p // 2 + 1
    n_heads = aqa_ref.shape[0] // ATT_HEAD_DIM
    group = n_heads // ATT_KV_HEADS
    half = KEY_TILE // 2
    cw = 2 * Q_TILE
    gw = group * Q_TILE
    ct_per_group = gw // cw
    radix_steps = 32

    for s, (iq_ref, aq_ref, iw_ref) in enumerate(((iqa_ref, aqa_ref, iwa_ref), (iqb_ref, aqb_ref, iwb_ref))):
        iqs_ref[s] = jnp.concatenate([iq_ref[h * IDX_DIM:(h + 1) * IDX_DIM, :] for h in range(IDX_HEADS)], axis=1)
        halves = [jnp.concatenate([aq_ref[(g * group + hh) * ATT_HEAD_DIM:(g * group + hh + 1) * ATT_HEAD_DIM, :]
                                   for hh in range(group)], axis=1) for g in range(ATT_KV_HEADS)]
        zero = jnp.zeros_like(halves[0])
        for g in range(ATT_KV_HEADS):
            qpad_ref[s, g] = jnp.concatenate([halves[g] if gg == g else zero for gg in range(ATT_KV_HEADS)], axis=0)
        iws_ref[s] = iw_ref[...]

    def unit(u):
        is_b = u >= n_a
        return is_b, jnp.where(is_b, 1, 0), jnp.where(is_b, u - n_a, u), jnp.where(is_b, j_of[1], j_of[0])

    def key_rows(kt, mc):
        return pl.ds(pl.multiple_of(kt * KEY_TILE + mc * half, half), half)

    row_h = lax.broadcasted_iota(I32, (half, Q_TILE), 0)
    lane_h = lax.broadcasted_iota(I32, (half, Q_TILE), 1)

    for u in range(n_units):
        _, s, kt, j = unit(u)
        iw = iws_ref[s]
        for mc in range(2):
            ik = ikb_ref[key_rows(kt, mc), :]
            sc = jnp.zeros((half, Q_TILE), F32)
            for ct in range(IDX_HEADS // 2):
                d = _dot(ik, iqs_ref[s, :, ct * cw:(ct + 1) * cw])
                for hh in range(2):
                    h = 2 * ct + hh
                    sc = sc + jnp.maximum(d[:, hh * Q_TILE:(hh + 1) * Q_TILE], 0.0) * iw[h:h + 1, :]
            causal = kt * KEY_TILE + mc * half + row_h <= j * Q_TILE + lane_h
            keys_ref[u, mc * half:(mc + 1) * half, :] = jnp.where(causal, sc, NEG_INF)

    n_pieces = n_units * 2 * ATT_KV_HEADS

    def qk_piece(piece):
        u, mc, g = piece // (2 * ATT_KV_HEADS), (piece // ATT_KV_HEADS) % 2, piece % ATT_KV_HEADS
        _, s, kt, _ = unit(u)
        logit_ref[u, mc, g] = _dot(akb_ref[key_rows(kt, mc), :], qpad_ref[s, g])

    zero8 = jnp.zeros((8, Q_TILE), I32)

    def count2(pred):
        tot = [zero8, zero8]
        for u in range(n_units):
            is_b, _, kt, _ = unit(u)
            m = pred(keys_ref[u], lambda a, b_: jnp.where(is_b, b_, a), kt)
            part = jnp.sum(jnp.where(m, 1, 0).reshape(KEY_TILE // 8, 8, Q_TILE), axis=0)
            tot[0] = tot[0] + jnp.where(is_b, zero8, part)
            tot[1] = tot[1] + jnp.where(is_b, part, zero8)
        return tuple(jnp.sum(t, axis=0, keepdims=True) for t in tot)

    def bit_step(i, carry):
        pref, at_thr = carry
        cand = tuple(pr | jnp.left_shift(jnp.int32(1), 31 - i) for pr in pref)
        cf = tuple(_radix_candidate(c) for c in cand)
        cnt = count2(lambda sc, pick, kt: sc >= pick(cf[0], cf[1]))
        take = tuple((cnt[t] >= n_sel) & (cf[t] > NEG_INF) for t in range(2))
        return (tuple(jnp.where(take[t], cand[t], pref[t]) for t in range(2)),
                tuple(jnp.where(take[t], cnt[t], at_thr[t]) for t in range(2)))

    steps_per_trip = 4
    trips = radix_steps // steps_per_trip
    pieces_per_trip = min(n_pieces // trips, steps_per_trip)

    def radix_trip(it, carry):
        for r in range(pieces_per_trip):
            qk_piece(it * pieces_per_trip + r)
        for r in range(steps_per_trip):
            carry = bit_step(it * steps_per_trip + r, carry)
        return carry

    zero_q = jnp.zeros((1, Q_TILE), I32)
    pref, at_thr = lax.fori_loop(0, trips, radix_trip, ((zero_q, zero_q), (zero_q, zero_q)))
    for piece in range(trips * pieces_per_trip, n_pieces):
        qk_piece(piece)
    found = tuple(pr != 0 for pr in pref)
    thr = tuple(jnp.where(found[t], _radix_candidate(pref[t]), NEG_INF) for t in range(2))

    above = count2(lambda sc, pick, kt: sc > pick(thr[0], thr[1]))
    need = tuple(n_sel - a for a in above)
    any_tie = jnp.max(jnp.maximum(jnp.where(at_thr[0] > n_sel, 1, 0), jnp.where(at_thr[1] > n_sel, 1, 0))) > 0
    pos_bits = max(1, (ikb_ref.shape[0] - 1).bit_length())
    row_k = lax.broadcasted_iota(I32, (KEY_TILE, Q_TILE), 0)

    def last_kept_tie():
        def step(i, pre):
            cand = tuple(pr | jnp.left_shift(jnp.int32(1), pos_bits - 1 - i) for pr in pre)
            cnt = count2(lambda sc, pick, kt: (sc == pick(thr[0], thr[1])) & (kt * KEY_TILE + row_k < pick(cand[0], cand[1])))
            return tuple(jnp.where(cnt[t] < need[t], cand[t], pre[t]) for t in range(2))
        return lax.fori_loop(0, pos_bits, step, (zero_q, zero_q))

    big = jnp.full((1, Q_TILE), INT_MAX, I32)
    last_tie = lax.cond(any_tie, last_kept_tie, lambda: (big, big))
    last_tie = tuple(jnp.where(found[t], last_tie[t], -1) for t in range(2))

    for u in range(n_units):
        is_b, _, kt, _ = unit(u)
        t_u = jnp.where(is_b, thr[1], thr[0])
        last_u = jnp.where(is_b, last_tie[1], last_tie[0])
        sc = keys_ref[u]
        keep = (sc > t_u) | ((sc == t_u) & (kt * KEY_TILE + row_k <= last_u))
        bias_ref[u] = jnp.where(keep, 0.0, MASKED)

    neg = jnp.full((8, gw), MASKED, F32)
    cmax = [[neg] * ATT_KV_HEADS, [neg] * ATT_KV_HEADS]
    for u in range(n_units):
        is_b, _, _, _ = unit(u)
        for mc in range(2):
            b_rows = bias_ref[u, mc * half:(mc + 1) * half, :]
            bias_g = jnp.concatenate([b_rows] * group, axis=1)
            for g in range(ATT_KV_HEADS):
                part = jnp.max((logit_ref[u, mc, g] + bias_g).reshape(half // 8, 8, gw), axis=0)
                cmax[0][g] = jnp.maximum(cmax[0][g], jnp.where(is_b, neg, part))
                cmax[1][g] = jnp.maximum(cmax[1][g], jnp.where(is_b, part, neg))
    cmax = [[jnp.max(c, axis=0, keepdims=True) for c in cm] for cm in cmax]

    zsum = jnp.zeros((8, cw), F32)
    lsum = [[[zsum] * ct_per_group for _ in range(ATT_KV_HEADS)] for _ in range(2)]
    for u in range(n_units):
        is_b, _, kt, _ = unit(u)
        for g in range(ATT_KV_HEADS):
            m_g = jnp.where(is_b, cmax[1][g], cmax[0][g])
            for ct in range(ct_per_group):
                cols = slice(ct * cw, (ct + 1) * cw)
                contrib = jnp.zeros((ATT_HEAD_DIM, cw), F32)
                for mc in range(2):
                    b_rows = bias_ref[u, mc * half:(mc + 1) * half, :]
                    bias_c = jnp.concatenate([b_rows] * (cw // Q_TILE), axis=1)
                    pr = jnp.exp(logit_ref[u, mc, g, :, cols] + bias_c - m_g[:, cols])
                    part = jnp.sum(pr.reshape(half // 8, 8, cw), axis=0)
                    lsum[0][g][ct] = lsum[0][g][ct] + jnp.where(is_b, zsum, part)
                    lsum[1][g][ct] = lsum[1][g][ct] + jnp.where(is_b, part, zsum)
                    v_t = avb_ref[kt, g * ATT_HEAD_DIM:(g + 1) * ATT_HEAD_DIM, mc * half:(mc + 1) * half]
                    contrib = contrib + _dot(v_t, pr.astype(BF16))
                pv_ref[u, g, :, cols] = contrib

    max_a = (n_units - 2) // 2 + 1
    for g in range(ATT_KV_HEADS):
        for ct in range(ct_per_group):
            cols = slice(ct * cw, (ct + 1) * cw)
            total = pv_ref[0, g, :, cols]
            for u in range(1, n_units):
                total = total + pv_ref[u, g, :, cols]
            part_a = pv_ref[0, g, :, cols]
            for u in range(1, max_a):
                part_a = part_a + jnp.where(u < n_a, pv_ref[u, g, :, cols], 0.0)
            for s, (o_ref, acc) in enumerate(((oa_ref, part_a), (ob_ref, total - part_a))):
                den = jnp.sum(lsum[s][g][ct], axis=0, keepdims=True)
                out_t = acc / den
                blk = jnp.concatenate([out_t[:, 0:Q_TILE], out_t[:, Q_TILE:cw]], axis=0)
                c_out = (g * ct_per_group + ct) * 2 * ATT_HEAD_DIM
                o_ref[0, 0, :, c_out:c_out + 2 * ATT_HEAD_DIM] = blk.T.astype(o_ref.dtype)


def _page_copy(pt_ref, src_hbm, layer, buf, sem, seq, slot, page):
    def copy(p):
        dst = buf.at[(slot,) + (slice(None),) * (len(buf.shape) - 2) + (pl.ds(pl.multiple_of(p * page, page), page),)]
        return pltpu.make_async_copy(src_hbm.at[layer, pt_ref[seq, p]], dst, sem.at[slot])
    return copy


def _start_pages(copy, n_pages):
    def body(p, c):
        copy(p).start()
        return c
    lax.fori_loop(0, n_pages, body, 0, unroll=8 if n_pages % 8 == 0 else 1)


def _wait_slot(buf, sem, slot):
    pltpu.make_async_copy(buf.at[1 - slot], buf.at[slot], sem.at[slot]).wait()


def _sample_scores_kernel(pt_ref, iq_ref, iw_ref, ikn_ref, kidx_hbm, sc_ref, buf, sem, *, layer, n_pages, page, tile):
    b = pl.program_id(0)
    nb = pl.num_programs(0)
    slot = b % 2

    @pl.when(b == 0)
    def _():
        _start_pages(_page_copy(pt_ref, kidx_hbm, layer, buf, sem, 0, 0, page), n_pages)

    @pl.when(b + 1 < nb)
    def _():
        _start_pages(_page_copy(pt_ref, kidx_hbm, layer, buf, sem, b + 1, 1 - slot, page), n_pages)

    _wait_slot(buf, sem, slot)

    iq = iq_ref[0]
    iqb = iq.astype(BF16)
    iw = iw_ref[0]
    past = n_pages * page
    for t in range(past // tile):
        kb = buf[slot, :, t * tile:(t + 1) * tile].astype(BF16)
        d = _dot(iqb, kb)
        sc_ref[0, :, t * tile:(t + 1) * tile] = jnp.sum(jnp.maximum(d, 0.0) * iw, axis=0, keepdims=True)
    d_new = jnp.sum(iq * ikn_ref[0], axis=1, keepdims=True)
    s_new = jnp.sum(jnp.maximum(d_new, 0.0) * iw, axis=0, keepdims=True)
    lane = lax.broadcasted_iota(I32, (1, V7X_LANES), 1)
    sc_ref[0, :, past:past + V7X_LANES] = jnp.where(lane == 0, s_new, 0.0)


def _sample_select_kernel(sc_ref, bias_ref, keys_ref, *, n_sel, past):
    rows, width = sc_ref.shape
    col = lax.broadcasted_iota(I32, (rows, width), 1)
    keys_ref[...] = jnp.where(col <= past, sc_ref[...], NEG_INF)

    def count(pred):
        return jnp.sum(pred(keys_ref[...]).astype(I32), axis=1, keepdims=True)

    def bit_step(i, carry):
        prefix, at_thr = carry
        cand = prefix | jnp.left_shift(jnp.int32(1), 31 - i)
        cf = _radix_candidate(cand)
        cnt = count(lambda sc: sc >= cf)
        take = (cnt >= n_sel) & (cf > NEG_INF)
        return jnp.where(take, cand, prefix), jnp.where(take, cnt, at_thr)

    zero = jnp.zeros((rows, 1), I32)
    prefix, _ = lax.fori_loop(0, 32, bit_step, (zero, zero))
    found = prefix != 0
    thr = jnp.where(found, _radix_candidate(prefix), NEG_INF)
    need = n_sel - count(lambda sc: sc > thr)
    pos_bits = max(1, (width - 1).bit_length())

    def pos_step(i, pre):
        cand = pre | jnp.left_shift(jnp.int32(1), pos_bits - 1 - i)
        cnt = count(lambda sc: (sc == thr) & (col < cand))
        return jnp.where(cnt < need, cand, pre)

    last_tie = jnp.where(found, lax.fori_loop(0, pos_bits, pos_step, zero), -1)
    sc = keys_ref[...]
    keep = (sc > thr) | ((sc == thr) & (col <= last_tie))
    bias_ref[...] = jnp.where(keep, 0.0, MASKED)


def _sample_attention_kernel(pt_ref, aq_ref, gq_ref, akn_ref, av_ref, bias_ref, k_hbm, v_hbm, o_ref,
                             kbuf, vbuf, logit_ref, ksem, vsem, *, layer, n_pages, page, tile):
    b = pl.program_id(0)
    nb = pl.num_programs(0)
    slot = b % 2

    def start(seq, s):
        _start_pages(_page_copy(pt_ref, k_hbm, layer, kbuf, ksem, seq, s, page), n_pages)
        _start_pages(_page_copy(pt_ref, v_hbm, layer, vbuf, vsem, seq, s, page), n_pages)

    @pl.when(b == 0)
    def _():
        start(0, 0)

    @pl.when(b + 1 < nb)
    def _():
        start(b + 1, 1 - slot)

    _wait_slot(kbuf, ksem, slot)
    _wait_slot(vbuf, vsem, slot)

    n_heads = aq_ref.shape[1]
    group = n_heads // ATT_KV_HEADS
    kvw = ATT_KV_HEADS * ATT_HEAD_DIM
    past = n_pages * page
    q = _rms(aq_ref[0], -1) * gq_ref[...]
    zeros = jnp.zeros_like(q)
    head = lax.broadcasted_iota(I32, (n_heads, kvw), 0)
    q_pad = jnp.where(head < group, jnp.concatenate([q, zeros], axis=1), jnp.concatenate([zeros, q], axis=1))
    q_pad_b = q_pad.astype(BF16)

    for t in range(past // tile):
        cols = slice(t * tile, (t + 1) * tile)
        kb = kbuf[slot, :, :, cols].reshape(kvw, tile).astype(BF16)
        logit_ref[:, cols] = _dot(q_pad_b, kb) + bias_ref[0, :, cols]
    lane = lax.broadcasted_iota(I32, (n_heads, V7X_LANES), 1)
    l_new = jnp.sum(q_pad * akn_ref[0], axis=1, keepdims=True) + bias_ref[0, :, past:past + 1]
    logit_ref[:, past:past + V7X_LANES] = jnp.where(lane == 0, l_new, MASKED)

    m = jnp.max(logit_ref[...], axis=1, keepdims=True)
    lsum = jnp.zeros((n_heads, 1), F32)
    acc = jnp.zeros((kvw, n_heads), F32)
    for t in range(past // tile):
        cols = slice(t * tile, (t + 1) * tile)
        p = jnp.exp(logit_ref[:, cols] - m)
        lsum = lsum + jnp.sum(p, axis=1, keepdims=True)
        acc = acc + _dot_nt(vbuf[slot, :, :, cols].reshape(kvw, tile).astype(BF16), p.astype(BF16))
    p_new = jnp.exp(logit_ref[:, past:past + 1] - m)
    acc = acc + _row_to_col(av_ref[0]) * _col_to_row(p_new)
    o_ref[0] = acc / _col_to_row(lsum + p_new)


def _outproj_mlp_kernel(x_ref, atta_ref, attb_ref, rec_ref, wo_ref, g2_ref, wup_ref, wdn_ref, o_ref, *, ff_tile, tiles_per_seq):
    n_t = atta_ref.shape[1]
    att = jnp.concatenate([atta_ref[0, k] for k in range(n_t)], axis=0)
    if tiles_per_seq:
        late = jnp.concatenate([attb_ref[0, n_t - 1 - k] for k in range(n_t)], axis=0)
        att = jnp.where(pl.program_id(0) % tiles_per_seq < tiles_per_seq // 2, att, late)
    mix = jnp.concatenate([att.astype(BF16), rec_ref[...].astype(BF16)], axis=1)
    y = x_ref[...] + _dot(mix, wo_ref[...])
    h2 = (_rms(y, -1) * g2_ref[...]).astype(BF16)
    acc = y
    for c in range(wup_ref.shape[1] // ff_tile):
        u = jnp.maximum(_dot(h2, wup_ref[:, c * ff_tile:(c + 1) * ff_tile]), 0.0)
        acc = acc + _dot((u * u).astype(BF16), wdn_ref[c * ff_tile:(c + 1) * ff_tile, :])
    o_ref[...] = acc


def _row_tile(n_rows, want):
    return want if n_rows % want == 0 else n_rows


def _layer_weights(w_in, att_width, kv_width, idx_width, rec_width):
    o = [0]
    for wdt in (att_width, kv_width, kv_width, idx_width, IDX_DIM, IDX_HEADS, rec_width, rec_width, rec_width, rec_width):
        o.append(o[-1] + wdt)
    aq, ak, av, iq, ik, iw, rq, rf, ri, rg = (w_in[:, o[i]:o[i + 1]] for i in range(10))
    d = w_in.shape[0]
    z = lambda n: jnp.zeros((d, n), w_in.dtype)
    w_main = jnp.concatenate([ak, ik, z(128 - IDX_DIM), rq, rf, ri, rg], axis=1).astype(BF16)
    w_tok = jnp.concatenate([aq, iq, av, ak, ik, iw, z(16 - IDX_HEADS)], axis=1).T.astype(BF16)
    w_qry = jnp.concatenate([aq, iq, av, iw, z(128 - IDX_HEADS)], axis=1).astype(BF16)
    return w_main, w_tok, w_qry


def _outproj_mlp(x, att_a, att_b, rec, w_out, g2, w_up, w_dn, name):
    n, d = x.shape
    tm = _row_tile(n, 512)
    ff = w_up.shape[1]
    width = att_a.shape[-1]
    if att_b is None:
        assert att_a.shape[:3] == (1, 1, n)
        tps, att_b = 0, att_a
        spec_a = spec_b = pl.BlockSpec((1, 1, tm, width), lambda i: (0, 0, i, 0))
    else:
        rows = att_a.shape[2]
        n_t = tm // rows
        tps = att_a.shape[1] * 2 * rows // tm
        assert tm % rows == 0 and tps % 2 == 0 and tps * tm == 2 * att_a.shape[1] * rows
        half = tps // 2
        spec_a = pl.BlockSpec((1, n_t, rows, width), lambda i: (i // tps, jnp.minimum(i % tps, half - 1), 0, 0))
        spec_b = pl.BlockSpec((1, n_t, rows, width), lambda i: (i // tps, jnp.clip(tps - 1 - i % tps, 0, half - 1), 0, 0))
    return pl.pallas_call(
        functools.partial(_outproj_mlp_kernel, ff_tile=min(ff, 1024), tiles_per_seq=tps),
        grid=(n // tm,),
        in_specs=[pl.BlockSpec((tm, d), lambda i: (i, 0)), spec_a, spec_b,
                  pl.BlockSpec((tm, rec.shape[1]), lambda i: (i, 0)),
                  _resident(w_out.shape), _resident(g2.shape), _resident(w_up.shape), _resident(w_dn.shape)],
        out_specs=pl.BlockSpec((tm, d), lambda i: (i, 0)),
        out_shape=jax.ShapeDtypeStruct((n, d), F32),
        compiler_params=_params("arbitrary"),
        name=name,
    )(x, att_a, att_b, rec, w_out, g2, w_up, w_dn)


def _prompt_attention(iqt, aqt, iwt, ikb, akb, avb, bsz, seq):
    n = bsz * seq
    kvw = ATT_KV_HEADS * ATT_HEAD_DIM
    att_width, idx_width = aqt.shape[0], iqt.shape[0]
    n_sel = min(TOPK_MAX, seq // 4)
    nq = seq // Q_TILE
    assert KEY_TILE == 2 * Q_TILE and nq % 2 == 0, "query tiles are paired so every step sees nq/2 + 1 key tiles"
    n_pairs = nq // 2
    n_units = n_pairs + 1
    kt_per_seq = seq // KEY_TILE
    avb_tiles = avb.reshape(kvw, n // KEY_TILE, KEY_TILE).transpose(1, 0, 2)
    n_heads = att_width // ATT_HEAD_DIM
    gw2 = n_heads * Q_TILE
    qa = lambda h: pl.BlockSpec((h, Q_TILE), lambda b, p: (0, b * nq + p))
    qb = lambda h: pl.BlockSpec((h, Q_TILE), lambda b, p: (0, b * nq + nq - 1 - p))
    outb = pl.BlockSpec((1, 1, Q_TILE, att_width), lambda b, p: (b, p, 0, 0))
    return pl.pallas_call(
        functools.partial(_prompt_attention_kernel, n_sel=n_sel, n_units=n_units),
        grid=(bsz, n_pairs),
        in_specs=[qa(idx_width), qb(idx_width), qa(att_width), qb(att_width), qa(IDX_HEADS), qb(IDX_HEADS),
                  pl.BlockSpec((seq, IDX_DIM), lambda b, p: (b, 0)),
                  pl.BlockSpec((seq, kvw), lambda b, p: (b, 0)),
                  pl.BlockSpec((kt_per_seq, kvw, KEY_TILE), lambda b, p: (b, 0, 0))],
        out_specs=[outb, outb],
        out_shape=[jax.ShapeDtypeStruct((bsz, n_pairs, Q_TILE, att_width), BF16)] * 2,
        scratch_shapes=[pltpu.VMEM((2, IDX_DIM, IDX_HEADS * Q_TILE), BF16),
                        pltpu.VMEM((2, ATT_KV_HEADS, kvw, gw2 // ATT_KV_HEADS), BF16),
                        pltpu.VMEM((2, IDX_HEADS, Q_TILE), F32),
                        pltpu.VMEM((n_units, KEY_TILE, Q_TILE), F32),
                        pltpu.VMEM((n_units, KEY_TILE, Q_TILE), F32),
                        pltpu.VMEM((n_units, 2, ATT_KV_HEADS, KEY_TILE // 2, gw2 // ATT_KV_HEADS), F32),
                        pltpu.VMEM((n_units, ATT_KV_HEADS, ATT_HEAD_DIM, gw2 // ATT_KV_HEADS), F32)],
        compiler_params=_params("arbitrary", "arbitrary"),
        name="attention_prompt",
    )(iqt, iqt, aqt, aqt, iwt, iwt, ikb, akb, avb_tiles)


def _prompt_layer(x, lbs, layer, w_main, w_tok, g1, gq_col, gk_row, gi_row, gk_col, gi_col, gn_row, w_out, g2,
                  w_up, w_dn, att_width, idx_width, rec_width, idx_w_scale):
    bsz, seq, d = x.shape
    n = bsz * seq
    xf = x.reshape(n, d)
    tm = _row_tile(seq, 512)
    tps = seq // tm
    kvw = ATT_KV_HEADS * ATT_HEAD_DIM
    row = lambda w: pl.BlockSpec((tm, w), lambda i: (i, 0))
    colb = lambda h: pl.BlockSpec((h, tm), lambda i: (0, i))
    seqb = lambda h: pl.BlockSpec((1, h, tm), lambda i: (i // tps, 0, i % tps))
    akb, ikb, rec4, aqt, iqt, avb, iwt, akt, avt, ikt = pl.pallas_call(
        functools.partial(_inproj_prompt_kernel, idx_w_scale=idx_w_scale),
        grid=(n // tm,),
        in_specs=[row(d), _resident(g1.shape), _resident(w_main.shape), _resident(w_tok.shape),
                  _resident(gk_row.shape), _resident(gi_row.shape), _resident(gq_col.shape),
                  _resident(gk_col.shape), _resident(gi_col.shape)],
        out_specs=[row(kvw), row(IDX_DIM), row(4 * rec_width),
                   colb(att_width), colb(idx_width), colb(kvw), colb(IDX_HEADS),
                   seqb(kvw), seqb(kvw), seqb(IDX_DIM)],
        out_shape=[jax.ShapeDtypeStruct((n, kvw), BF16), jax.ShapeDtypeStruct((n, IDX_DIM), BF16),
                   jax.ShapeDtypeStruct((n, 4 * rec_width), F32),
                   jax.ShapeDtypeStruct((att_width, n), BF16), jax.ShapeDtypeStruct((idx_width, n), BF16),
                   jax.ShapeDtypeStruct((kvw, n), BF16), jax.ShapeDtypeStruct((IDX_HEADS, n), F32),
                   jax.ShapeDtypeStruct((bsz, kvw, seq), F32), jax.ShapeDtypeStruct((bsz, kvw, seq), F32),
                   jax.ShapeDtypeStruct((bsz, IDX_DIM, seq), F32)],
        compiler_params=_params("arbitrary"),
        name="inproj_prompt",
    )(xf, g1, w_main, w_tok, gk_row, gi_row, gq_col, gk_col, gi_col)

    tb = _row_tile(seq, 256)
    n_rec = rec_width // REC_DIM
    tiles = seq // tb
    gs = 2 if bsz % 2 == 0 else 1

    def hgrn(direct):
        return pl.pallas_call(
            functools.partial(_hgrn_prompt_kernel, layer=layer, rec_width=rec_width, direct=direct),
            grid=(bsz // gs, tiles),
            in_specs=[_resident(lbs.shape), pl.BlockSpec((gs, tb, 4 * rec_width), lambda b, t: (b, t, 0)),
                      _resident(gn_row.shape)],
            out_specs=[pl.BlockSpec((gs, tb, rec_width), lambda b, t: (b, t, 0)),
                       pl.BlockSpec((gs, n_rec, REC_DIM, REC_DIM), lambda b, t: (b, 0, 0, 0))],
            out_shape=[jax.ShapeDtypeStruct((bsz, seq, rec_width), BF16),
                       jax.ShapeDtypeStruct((bsz, n_rec, REC_DIM, REC_DIM), F32)],
            scratch_shapes=[pltpu.VMEM((gs, n_rec, REC_DIM, REC_DIM), F32),
                            pltpu.VMEM((CHUNK, rec_width), F32), pltpu.VMEM((CHUNK, rec_width), F32)],
            compiler_params=_params("arbitrary", "arbitrary"),
            name="hgrn_prompt_direct" if direct else "hgrn_prompt",
        )

    lb_min = jnp.min(jnp.cumsum(jax.nn.softmax(lbs, axis=0), axis=0)[layer])
    factorable = lb_min >= float(np.exp(-HGRN_SAFE_SPAN / (CHUNK // 2)))
    rec, s_fin = lax.cond(factorable, hgrn(False), hgrn(True), lbs, rec4.reshape(bsz, seq, 4 * rec_width), gn_row)
    rec = rec.reshape(n, rec_width)

    att_a, att_b = _prompt_attention(iqt, aqt, iwt, ikb, akb, avb, bsz, seq)
    if (seq // tm) % 2:
        att_a = jnp.concatenate([att_a, att_b[:, ::-1]], axis=1).reshape(1, 1, n, att_width)
        att_b = None

    y = _outproj_mlp(xf, att_a, att_b, rec, w_out, g2, w_up, w_dn, "outproj_mlp_prompt")
    to_heads = lambda t: t.reshape(bsz, ATT_KV_HEADS, ATT_HEAD_DIM, seq).transpose(0, 3, 1, 2)
    return y.reshape(bsz, seq, d), to_heads(akt), to_heads(avt), ikt.transpose(0, 2, 1), s_fin


def _sample_layer(x, s0, lbs, layer, cache_k, cache_v, cache_kidx, page_table, w_main, w_qry, g1, gq_row,
                  gk_row, gi_row, gn_row, w_out, g2, w_up, w_dn, att_width, idx_width, rec_width, idx_w_scale):
    bd, t_new, d = x.shape
    assert t_new == 1, "the sample path handles one new token per sequence"
    kvw = ATT_KV_HEADS * ATT_HEAD_DIM
    n_heads = att_width // ATT_HEAD_DIM
    n_rec = rec_width // REC_DIM
    page = cache_k.shape[2]
    n_pages = page_table.shape[1]
    past = n_pages * page
    width = past + V7X_LANES
    tile = 2048 if past % 2048 == 0 else past
    xf = x.reshape(bd, d)

    full = lambda shape: _resident(shape)
    akn, av, ikn, rec4, aq, iq, iw = pl.pallas_call(
        functools.partial(_inproj_sample_kernel, idx_w_scale=idx_w_scale),
        grid=(1,),
        in_specs=[full(xf.shape), full(g1.shape), full(w_main.shape), full(w_qry.shape),
                  full(gk_row.shape), full(gi_row.shape)],
        out_specs=[full((bd, kvw)), full((bd, kvw)), full((bd, IDX_DIM)), full((bd, 4 * rec_width)),
                   full((bd, att_width)), full((bd, idx_width)), full((bd, 128))],
        out_shape=[jax.ShapeDtypeStruct((bd, kvw), F32), jax.ShapeDtypeStruct((bd, kvw), F32),
                   jax.ShapeDtypeStruct((bd, IDX_DIM), F32), jax.ShapeDtypeStruct((bd, 4 * rec_width), F32),
                   jax.ShapeDtypeStruct((bd, att_width), F32), jax.ShapeDtypeStruct((bd, idx_width), F32),
                   jax.ShapeDtypeStruct((bd, 128), F32)],
        compiler_params=_params("arbitrary"),
        name="inproj_sample",
    )(xf, g1, w_main, w_qry, gk_row, gi_row)

    sb = 4 if bd % 4 == 0 else 1
    rec, s1 = pl.pallas_call(
        functools.partial(_hgrn_step_kernel, layer=layer, rec_width=rec_width),
        grid=(bd // sb,),
        in_specs=[_resident(lbs.shape), pl.BlockSpec((sb, 1, 4 * rec_width), lambda b: (b, 0, 0)),
                  _resident(gn_row.shape), pl.BlockSpec((sb, n_rec, REC_DIM, REC_DIM), lambda b: (b, 0, 0, 0))],
        out_specs=[pl.BlockSpec((sb, 1, rec_width), lambda b: (b, 0, 0)),
                   pl.BlockSpec((sb, n_rec, REC_DIM, REC_DIM), lambda b: (b, 0, 0, 0))],
        out_shape=[jax.ShapeDtypeStruct((bd, 1, rec_width), BF16),
                   jax.ShapeDtypeStruct((bd, n_rec, REC_DIM, REC_DIM), F32)],
        compiler_params=_params("arbitrary"),
        name="hgrn_step",
    )(lbs, rec4.reshape(bd, 1, 4 * rec_width), gn_row, s0)

    kidx_t = cache_kidx.transpose(0, 1, 3, 2)
    k_t = cache_k.transpose(0, 1, 3, 4, 2)
    v_t = cache_v.transpose(0, 1, 3, 4, 2)
    iq3 = iq.reshape(bd, IDX_HEADS, IDX_DIM)
    iw3 = iw[:, 0:IDX_HEADS].reshape(bd, IDX_HEADS, 1)
    ikn3 = ikn.reshape(bd, 1, IDX_DIM)
    scores = pl.pallas_call(
        functools.partial(_sample_scores_kernel, layer=layer, n_pages=n_pages, page=page, tile=tile),
        grid_spec=pltpu.PrefetchScalarGridSpec(
            num_scalar_prefetch=1,
            grid=(bd,),
            in_specs=[pl.BlockSpec((1, IDX_HEADS, IDX_DIM), lambda b, pt: (b, 0, 0)),
                      pl.BlockSpec((1, IDX_HEADS, 1), lambda b, pt: (b, 0, 0)),
                      pl.BlockSpec((1, 1, IDX_DIM), lambda b, pt: (b, 0, 0)),
                      pl.BlockSpec(memory_space=pl.ANY)],
            out_specs=pl.BlockSpec((1, 1, width), lambda b, pt: (b, 0, 0)),
            scratch_shapes=[pltpu.VMEM((2, IDX_DIM, past), F32), pltpu.SemaphoreType.DMA((2,))]),
        out_shape=jax.ShapeDtypeStruct((bd, 1, width), F32),
        compiler_params=_params("arbitrary"),
        name="sample_scores",
    )(page_table, iq3, iw3, ikn3, kidx_t)

    n_sel = min(TOPK_MAX, (past + 1) // 4)
    bias = pl.pallas_call(
        functools.partial(_sample_select_kernel, n_sel=n_sel, past=past),
        grid=(1,),
        in_specs=[full((bd, width))],
        out_specs=full((bd, width)),
        out_shape=jax.ShapeDtypeStruct((bd, width), F32),
        scratch_shapes=[pltpu.VMEM((bd, width), F32)],
        compiler_params=_params("arbitrary"),
        name="sample_select",
    )(scores.reshape(bd, width))

    att_t = pl.pallas_call(
        functools.partial(_sample_attention_kernel, layer=layer, n_pages=n_pages, page=page, tile=tile),
        grid_spec=pltpu.PrefetchScalarGridSpec(
            num_scalar_prefetch=1,
            grid=(bd,),
            in_specs=[pl.BlockSpec((1, n_heads, ATT_HEAD_DIM), lambda b, pt: (b, 0, 0)),
                      pl.BlockSpec((1, ATT_HEAD_DIM), lambda b, pt: (0, 0)),
                      pl.BlockSpec((1, 1, kvw), lambda b, pt: (b, 0, 0)),
                      pl.BlockSpec((1, 1, kvw), lambda b, pt: (b, 0, 0)),
                      pl.BlockSpec((1, 1, width), lambda b, pt: (b, 0, 0)),
                      pl.BlockSpec(memory_space=pl.ANY), pl.BlockSpec(memory_space=pl.ANY)],
            out_specs=pl.BlockSpec((1, kvw, n_heads), lambda b, pt: (b, 0, 0)),
            scratch_shapes=[pltpu.VMEM((2, ATT_KV_HEADS, ATT_HEAD_DIM, past), F32),
                            pltpu.VMEM((2, ATT_KV_HEADS, ATT_HEAD_DIM, past), F32),
                            pltpu.VMEM((n_heads, width), F32),
                            pltpu.SemaphoreType.DMA((2,)), pltpu.SemaphoreType.DMA((2,))]),
        out_shape=jax.ShapeDtypeStruct((bd, kvw, n_heads), F32),
        compiler_params=_params("arbitrary"),
        name="sample_attention",
    )(page_table, aq.reshape(bd, n_heads, ATT_HEAD_DIM), gq_row, akn.reshape(bd, 1, kvw), av.reshape(bd, 1, kvw),
      bias.reshape(bd, 1, width), k_t, v_t)
    att_g = att_t.reshape(bd, ATT_KV_HEADS, ATT_HEAD_DIM, n_heads)
    group = n_heads // ATT_KV_HEADS
    att = jnp.stack([att_g[:, h // group, :, h] for h in range(n_heads)], axis=1)

    y = _outproj_mlp(xf, att.reshape(1, 1, bd, att_width), None, rec.reshape(bd, rec_width), w_out, g2, w_up, w_dn,
                     "outproj_mlp_sample")
    return (y.reshape(bd, 1, d), akn.reshape(bd, 1, ATT_KV_HEADS, ATT_HEAD_DIM),
            av.reshape(bd, 1, ATT_KV_HEADS, ATT_HEAD_DIM), ikn.reshape(bd, 1, IDX_DIM), s1)


def kernel(x_prompt, x_sample, cache_k, cache_v, cache_kidx, state_hgrn, page_table, norm1_g, w_in, q_norm_g,
           k_norm_g, idx_k_norm_g, lower_bounds, rec_norm_g, w_out, norm2_g, w_up, w_down):
    depth = w_in.shape[0]
    d_model = x_prompt.shape[-1]
    kv_width = ATT_KV_HEADS * ATT_HEAD_DIM
    idx_width = IDX_HEADS * IDX_DIM
    rec_width = lower_bounds.shape[-1]
    att_width = w_out.shape[1] - rec_width
    assert w_in.shape[-1] == att_width + 2 * kv_width + idx_width + IDX_DIM + IDX_HEADS + 4 * rec_width
    assert rec_norm_g.shape[-1] == REC_DIM and q_norm_g.shape[-1] == ATT_HEAD_DIM
    idx_w_scale = (IDX_HEADS ** -0.5) * (IDX_DIM ** -0.5)
    att_scale = ATT_HEAD_DIM ** -0.5
    n_heads = att_width // ATT_HEAD_DIM
    lbs = lower_bounds.astype(F32)

    yp, ys = x_prompt, x_sample
    outs = [[] for _ in range(8)]
    for l in range(depth):
        w_main, w_tok, w_qry = _layer_weights(w_in[l], att_width, kv_width, idx_width, rec_width)
        g1 = norm1_g[l].reshape(1, d_model)
        g2 = norm2_g[l].reshape(1, d_model)
        gq_row = (q_norm_g[l] * att_scale).reshape(1, ATT_HEAD_DIM)
        gq_col = jnp.tile(q_norm_g[l] * att_scale, n_heads).reshape(att_width, 1)
        gk_row = jnp.tile(k_norm_g[l], ATT_KV_HEADS).reshape(1, kv_width)
        gi_row = idx_k_norm_g[l].reshape(1, IDX_DIM)
        gk_col = gk_row.reshape(kv_width, 1)
        gi_col = gi_row.reshape(IDX_DIM, 1)
        gn_row = rec_norm_g[l].reshape(1, REC_DIM)
        wo, wu, wd = w_out[l].astype(BF16), w_up[l].astype(BF16), w_down[l].astype(BF16)
        yp, kp, vp, ip, sp = _prompt_layer(yp, lbs, l, w_main, w_tok, g1, gq_col, gk_row, gi_row, gk_col, gi_col,
                                           gn_row, wo, g2, wu, wd, att_width, idx_width, rec_width, idx_w_scale)
        ys, k_s, v_s, i_s, s_s = _sample_layer(ys, state_hgrn[l], lbs, l, cache_k, cache_v, cache_kidx,
                                               page_table, w_main, w_qry, g1, gq_row, gk_row, gi_row, gn_row, wo,
                                               g2, wu, wd, att_width, idx_width, rec_width, idx_w_scale)
        for acc, val in zip(outs, (kp, vp, ip, sp, k_s, v_s, i_s, s_s)):
            acc.append(val)
    return (yp, ys) + tuple(jnp.stack(o) for o in outs)
```

```python
import functools

import jax
import jax.numpy as jnp
import numpy as np
from jax import lax
from jax.experimental import pallas as pl
from jax.experimental.pallas import tpu as pltpu

F32 = jnp.float32
BF16 = jnp.bfloat16
I32 = jnp.int32

EPS = 1e-6
MASKED = -1e30
INT_MIN = -(2 ** 31)
NEG_INF = float("-inf")
INT_MAX = 2 ** 31 - 1

ATT_HEAD_DIM = 64
ATT_KV_HEADS = 2
IDX_HEADS = 8
IDX_DIM = 64
REC_DIM = 128
TOPK_MAX = 256
CHUNK = 64
HGRN_SAFE_SPAN = 60.0

V7X_LANES = 128
V7X_VMEM_LIMIT_BYTES = 56 * 1024 * 1024
Q_TILE = 128
KEY_TILE = 256


def _dot(a, b):
    return jnp.dot(a, b, preferred_element_type=F32)


def _dot_nt(a, b):
    return lax.dot_general(a, b, (((1,), (1,)), ((), ())), preferred_element_type=F32)


def _dot_tn(a, b):
    return lax.dot_general(a, b, (((0,), (0,)), ((), ())), preferred_element_type=F32)


def _params(*semantics):
    return pltpu.CompilerParams(dimension_semantics=semantics, vmem_limit_bytes=V7X_VMEM_LIMIT_BYTES)


def _resident(shape):
    nd = len(shape)
    return pl.BlockSpec(shape, lambda *_: (0,) * nd)


def _rms(x, axis):
    return x * lax.rsqrt(jnp.mean(x * x, axis=axis, keepdims=True) + EPS)


def _sigmoid_pair(x):
    t = jnp.exp(-jnp.abs(x))
    r = 1.0 / (1.0 + t)
    tr = t * r
    pos = x >= 0
    return jnp.where(pos, r, tr), jnp.where(pos, tr, r)


def _silu(x):
    s, _ = _sigmoid_pair(x)
    return x * s


def _radix_candidate(prefix_bits):
    key = prefix_bits ^ INT_MIN
    return lax.bitcast_convert_type(key ^ ((key >> 31) & INT_MAX), F32)


def _lower_bound(lbs_ref, layer):
    lbs = lbs_ref[...]
    e = jnp.exp(lbs - jnp.max(lbs, axis=0, keepdims=True))
    p = e / jnp.sum(e, axis=0, keepdims=True)
    return jnp.sum(p[0:layer + 1, :], axis=0, keepdims=True)


def _head_rms_lanes(x, gain_row):
    sq = x * x
    lane = lax.broadcasted_iota(I32, x.shape, 1)
    lo = lane < ATT_HEAD_DIM
    s_lo = jnp.sum(jnp.where(lo, sq, 0.0), axis=-1, keepdims=True)
    s_hi = jnp.sum(jnp.where(lo, 0.0, sq), axis=-1, keepdims=True)
    inv = jnp.where(lo, lax.rsqrt(s_lo / ATT_HEAD_DIM + EPS), lax.rsqrt(s_hi / ATT_HEAD_DIM + EPS))
    return x * inv * gain_row


def _inproj_prompt_kernel(x_ref, g1_ref, wm_ref, wt_ref, gk_ref, gi_ref, gq_ref, gkc_ref, gic_ref,
                          akb_ref, ikb_ref, rec4_ref, aqt_ref, iqt_ref, avb_ref, iwt_ref, akt_ref, avt_ref, ikt_ref,
                          *, idx_w_scale):
    x = x_ref[...]
    hb = (_rms(x, -1) * g1_ref[...]).astype(BF16)
    y = _dot(hb, wm_ref[...])
    akb_ref[...] = _head_rms_lanes(y[:, 0:128], gk_ref[...]).astype(BF16)
    ikb_ref[...] = (_rms(y[:, 128:128 + IDX_DIM], -1) * gi_ref[...]).astype(BF16)
    rec4_ref[...] = y[:, 256:]

    yt = _dot_nt(wt_ref[...], hb)
    n_q = aqt_ref.shape[0] // ATT_HEAD_DIM
    for h in range(n_q):
        rows = slice(h * ATT_HEAD_DIM, (h + 1) * ATT_HEAD_DIM)
        aqt_ref[rows, :] = (_rms(yt[rows, :], 0) * gq_ref[rows, :]).astype(BF16)
    o = aqt_ref.shape[0]
    iqt_ref[...] = yt[o:o + iqt_ref.shape[0], :].astype(BF16)
    o += iqt_ref.shape[0]
    kvw = ATT_KV_HEADS * ATT_HEAD_DIM
    av_t = yt[o:o + kvw, :]
    avt_ref[0] = av_t
    avb_ref[...] = av_t.astype(BF16)
    o += kvw
    for g in range(ATT_KV_HEADS):
        rows = slice(g * ATT_HEAD_DIM, (g + 1) * ATT_HEAD_DIM)
        akt_ref[0, rows, :] = _rms(yt[o + g * ATT_HEAD_DIM:o + (g + 1) * ATT_HEAD_DIM, :], 0) * gkc_ref[rows, :]
    o += kvw
    ikt_ref[0] = _rms(yt[o:o + IDX_DIM, :], 0) * gic_ref[...]
    o += IDX_DIM
    iwt_ref[...] = yt[o:o + IDX_HEADS, :] * idx_w_scale


def _inproj_sample_kernel(x_ref, g1_ref, wm_ref, wq_ref, gk_ref, gi_ref,
                          akn_ref, av_ref, ikn_ref, rec4_ref, aq_ref, iq_ref, iw_ref, *, idx_w_scale):
    x = x_ref[...]
    hb = (_rms(x, -1) * g1_ref[...]).astype(BF16)
    y = _dot(hb, wm_ref[...])
    akn_ref[...] = _head_rms_lanes(y[:, 0:128], gk_ref[...])
    ikn_ref[...] = _rms(y[:, 128:128 + IDX_DIM], -1) * gi_ref[...]
    rec4_ref[...] = y[:, 256:]
    yq = _dot(hb, wq_ref[...])
    wa = aq_ref.shape[1]
    wi = iq_ref.shape[1]
    aq_ref[...] = yq[:, 0:wa]
    iq_ref[...] = yq[:, wa:wa + wi]
    av_ref[...] = yq[:, wa + wi:wa + wi + 128]
    iw_ref[...] = yq[:, wa + wi + 128:wa + wi + 256] * idx_w_scale


def _hgrn_gates(rq, rf, ri, rg, lb):
    sp, sn = _sigmoid_pair(rf)
    q = _silu(rq) * (REC_DIM ** -0.5)
    log_f = jnp.log(lb + (1.0 - lb) * sp)
    k = (1.0 - lb) * sn
    return q, log_f, k, ri, _silu(rg)


def _hgrn_prompt_kernel(lbs_ref, rec4_ref, gn_ref, rec_ref, sfin_ref, st_ref, bsc_ref, ksc_ref, *, layer, rec_width,
                        direct):
    n_heads = rec_width // REC_DIM
    n_seq, tb = rec4_ref.shape[0], rec4_ref.shape[1]

    @pl.when(pl.program_id(1) == 0)
    def _():
        st_ref[...] = jnp.zeros_like(st_ref)

    lb = _lower_bound(lbs_ref, layer)
    w = rec_width
    row = lax.broadcasted_iota(I32, (CHUNK, CHUNK), 0)
    col = lax.broadcasted_iota(I32, (CHUNK, CHUNK), 1)
    causal = row >= col
    tri = jnp.where(causal, 1.0, 0.0).astype(BF16)
    gn = gn_ref[...]
    gates = [_hgrn_gates(rec4_ref[sq, :, 0:w], rec4_ref[sq, :, w:2 * w], rec4_ref[sq, :, 2 * w:3 * w],
                         rec4_ref[sq, :, 3 * w:4 * w], lb) for sq in range(n_seq)]

    for c in range(tb // CHUNK):
        rows = slice(c * CHUNK, (c + 1) * CHUNK)
        for sq in range(n_seq):
            q, log_f, k, v, gate = gates[sq]
            gc = log_f[rows, :]
            g1 = gc.astype(BF16)
            r1 = gc - g1.astype(F32)
            g2 = r1.astype(BF16)
            g3 = (r1 - g2.astype(F32)).astype(BF16)
            b = _dot(tri, g1) + _dot(tri, g2) + _dot(tri, g3)
            b_last = b[CHUNK - 1:CHUNK, :]
            b_mid = b[CHUNK // 2 - 1:CHUNK // 2, :]
            qc, kc, vc = q[rows, :], k[rows, :], v[rows, :]
            heads = [slice(h * REC_DIM, (h + 1) * REC_DIM) for h in range(n_heads)]

            if direct:
                bsc_ref[...] = b
                ksc_ref[...] = kc

                def col_step(s, acc, b=b, qc=qc):
                    b_s = bsc_ref[pl.ds(s, 1), :]
                    wgt = qc * ksc_ref[pl.ds(s, 1), :] * jnp.exp(jnp.minimum(b - b_s, 0.0))
                    hit = causal & (col == s)
                    return tuple(jnp.where(hit, jnp.sum(wgt[:, hs], axis=1, keepdims=True), a_h)
                                 for hs, a_h in zip(heads, acc))

                a_heads = lax.fori_loop(0, CHUNK, col_step, (jnp.zeros((CHUNK, CHUNK), F32),) * n_heads)
            else:
                q_in = (qc * jnp.exp(b - b_mid)).astype(BF16)
                k_in = (kc * jnp.exp(b_mid - b)).astype(BF16)
                a_heads = tuple(jnp.where(causal, _dot_nt(q_in[:, hs], k_in[:, hs]), 0.0) for hs in heads)
            q_st = (qc * jnp.exp(b)).astype(BF16)
            k_st = (kc * jnp.exp(b_last - b)).astype(BF16)
            vb = vc.astype(BF16)
            decay = jnp.exp(b_last)
            for h in range(n_heads):
                cols = heads[h]
                st = st_ref[sq, h]
                o = _dot(a_heads[h].astype(BF16), vb[:, cols]) + _dot_nt(q_st[:, cols], st.astype(BF16))
                st_ref[sq, h] = st * decay[:, cols] + _dot_tn(vb[:, cols], k_st[:, cols])
                o = _rms(o, -1) * gn * gate[rows, cols]
                rec_ref[sq, rows, cols] = o.astype(rec_ref.dtype)

    @pl.when(pl.program_id(1) == pl.num_programs(1) - 1)
    def _():
        for sq in range(n_seq):
            for h in range(n_heads):
                sfin_ref[sq, h] = st_ref[sq, h].T


def _row_to_col(x_row):
    n = x_row.shape[1]
    eye = lax.broadcasted_iota(I32, (n, n), 0) == lax.broadcasted_iota(I32, (n, n), 1)
    return jnp.sum(jnp.where(eye, jnp.broadcast_to(x_row, (n, n)), 0.0), axis=1, keepdims=True)


def _col_to_row(x_col):
    n = x_col.shape[0]
    eye = lax.broadcasted_iota(I32, (n, n), 0) == lax.broadcasted_iota(I32, (n, n), 1)
    return jnp.sum(jnp.where(eye, jnp.broadcast_to(x_col, (n, n)), 0.0), axis=0, keepdims=True)


def _hgrn_step_kernel(lbs_ref, rec4_ref, gn_ref, s0_ref, rec_ref, s1_ref, *, layer, rec_width):
    n_heads = rec_width // REC_DIM
    lb = _lower_bound(lbs_ref, layer)
    w = rec_width
    gn = gn_ref[...]
    for sq in range(rec4_ref.shape[0]):
        r4 = rec4_ref[sq]
        q, log_f, k, v, gate = _hgrn_gates(r4[:, 0:w], r4[:, w:2 * w], r4[:, 2 * w:3 * w], r4[:, 3 * w:4 * w], lb)
        f = jnp.exp(log_f)
        for h in range(n_heads):
            cols = slice(h * REC_DIM, (h + 1) * REC_DIM)
            s1 = _row_to_col(f[:, cols]) * s0_ref[sq, h] + _row_to_col(k[:, cols]) * v[:, cols]
            s1_ref[sq, h] = s1
            o = jnp.sum(_row_to_col(q[:, cols]) * s1, axis=0, keepdims=True)
            rec_ref[sq, :, cols] = (_rms(o, -1) * gn * gate[:, cols]).astype(rec_ref.dtype)


def _prompt_attention_kernel(iqa_ref, iqb_ref, aqa_ref, aqb_ref, iwa_ref, iwb_ref, ikb_ref, akb_ref, avb_ref,
                             oa_ref, ob_ref, iqs_ref, qpad_ref, iws_ref, keys_ref, planes_ref, bias_ref, logit_ref, pv_ref,
                             *, n_sel, n_units):
    p = pl.program_id(1)
    nq = 2 * pl.num_programs(1)
    j_of = (p, nq - 1 - p)
    n_a = p // 2 + 1
    n_heads = aqa_ref.shape[0] // ATT_HEAD_DIM
    group = n_heads // ATT_KV_HEADS
    half = KEY_TILE // 2
    cw = 2 * Q_TILE
    gw = group * Q_TILE
    ct_per_group = gw // cw
    radix_steps = 32

    for s, (iq_ref, aq_ref, iw_ref) in enumerate(((iqa_ref, aqa_ref, iwa_ref), (iqb_ref, aqb_ref, iwb_ref))):
        iqs_ref[s] = jnp.concatenate([iq_ref[h * IDX_DIM:(h + 1) * IDX_DIM, :] for h in range(IDX_HEADS)], axis=1)
        halves = [jnp.concatenate([aq_ref[(g * group + hh) * ATT_HEAD_DIM:(g * group + hh + 1) * ATT_HEAD_DIM, :]
                                   for hh in range(group)], axis=1) for g in range(ATT_KV_HEADS)]
        zero = jnp.zeros_like(halves[0])
        for g in range(ATT_KV_HEADS):
            qpad_ref[s, g] = jnp.concatenate([halves[g] if gg == g else zero for gg in range(ATT_KV_HEADS)], axis=0)
        iws_ref[s] = iw_ref[...]

    def unit(u):
        is_b = u >= n_a
        return is_b, jnp.where(is_b, 1, 0), jnp.where(is_b, u - n_a, u), jnp.where(is_b, j_of[1], j_of[0])

    def key_rows(kt, mc):
        return pl.ds(pl.multiple_of(kt * KEY_TILE + mc * half, half), half)

    row_h = lax.broadcasted_iota(I32, (half, Q_TILE), 0)
    lane_h = lax.broadcasted_iota(I32, (half, Q_TILE), 1)

    for u in range(n_units):
        _, s, kt, j = unit(u)
        iw = iws_ref[s]
        for mc in range(2):
            ik = ikb_ref[key_rows(kt, mc), :]
            sc = jnp.zeros((half, Q_TILE), F32)
            for ct in range(IDX_HEADS // 2):
                d = _dot(ik, iqs_ref[s, :, ct * cw:(ct + 1) * cw])
                for hh in range(2):
                    h = 2 * ct + hh
                    sc = sc + jnp.maximum(d[:, hh * Q_TILE:(hh + 1) * Q_TILE], 0.0) * iw[h:h + 1, :]
            causal = kt * KEY_TILE + mc * half + row_h <= j * Q_TILE + lane_h
            keys_ref[u, mc * half:(mc + 1) * half, :] = jnp.where(causal, sc, NEG_INF)

    n_pieces = n_units * 2 * ATT_KV_HEADS

    def qk_piece(piece):
        u, mc, g = piece // (2 * ATT_KV_HEADS), (piece // ATT_KV_HEADS) % 2, piece % ATT_KV_HEADS
        _, s, kt, _ = unit(u)
        logit_ref[u, mc, g] = _dot(akb_ref[key_rows(kt, mc), :], qpad_ref[s, g])

    srl = lax.shift_right_logical
    for u in range(n_units):
        sc = keys_ref[u]
        bits = lax.bitcast_convert_type(sc, I32)
        okey = jnp.where(sc > NEG_INF, bits ^ ((bits >> 31) & INT_MAX) ^ INT_MIN, 0)
        w = [okey[8 * v:8 * v + 8, :] for v in range(KEY_TILE // 8)]
        j, m = 16, 0x0000FFFF
        while j:
            for k in range(32):
                if not k & j:
                    t = (w[k] ^ srl(w[k + j], jnp.int32(j))) & jnp.int32(m if m < 2 ** 31 else m - 2 ** 32)
                    w[k] = w[k] ^ t
                    w[k + j] = w[k + j] ^ (t << j)
            j >>= 1
            m = (m ^ (m << j)) & 0xFFFFFFFF
        for i in range(32):
            planes_ref[u, i] = w[i]
        for piece in range(u * 2 * ATT_KV_HEADS, (u + 1) * 2 * ATT_KV_HEADS):
            qk_piece(piece)

    zero8 = jnp.zeros((8, Q_TILE), I32)
    zero_q = jnp.zeros((1, Q_TILE), I32)
    max_a = (n_units - 2) // 2 + 1

    def count2(pred):
        tot = [zero8, zero8]
        for u in range(n_units):
            is_b, _, kt, _ = unit(u)
            m = pred(keys_ref[u], lambda a, b_: jnp.where(is_b, b_, a), kt)
            part = jnp.sum(jnp.where(m, 1, 0).reshape(KEY_TILE // 8, 8, Q_TILE), axis=0)
            tot[0] = tot[0] + jnp.where(is_b, zero8, part)
            tot[1] = tot[1] + jnp.where(is_b, part, zero8)
        return tuple(jnp.sum(t, axis=0, keepdims=True) for t in tot)

    def sliced_step(i, carry):
        alive, need, chosen = carry
        ones = [alive[u] & planes_ref[u, i] for u in range(n_units)]
        pcs = [lax.population_count(o) for o in ones]
        total, part_a = pcs[0], pcs[0]
        for u in range(1, n_units):
            total = total + pcs[u]
            if u < max_a:
                part_a = part_a + jnp.where(u < n_a, pcs[u], 0)
        c1 = (jnp.sum(part_a, axis=0, keepdims=True), jnp.sum(total - part_a, axis=0, keepdims=True))
        take = tuple(jnp.where(c1[t] >= need[t], 1, 0) for t in range(2))
        bit = jnp.left_shift(jnp.int32(1), 31 - i)
        new_alive = []
        for u in range(n_units):
            is_b, _, _, _ = unit(u)
            new_alive.append(jnp.where(jnp.where(is_b, take[1], take[0]) != 0, ones[u], alive[u] ^ ones[u]))
        return (tuple(new_alive), tuple(need[t] - (1 - take[t]) * c1[t] for t in range(2)),
                tuple(chosen[t] | (take[t] * bit) for t in range(2)))

    full = jnp.full((8, Q_TILE), -1, I32)
    want = jnp.full((1, Q_TILE), n_sel, I32)
    _, _, guess = lax.fori_loop(0, 32, sliced_step, ((full,) * n_units, (want, want), (zero_q, zero_q)))

    g_found = tuple(gs != 0 for gs in guess)
    g_thr = tuple(jnp.where(g_found[t], _radix_candidate(guess[t]), NEG_INF) for t in range(2))
    g_at = count2(lambda sc, pick, kt: sc >= pick(g_thr[0], g_thr[1]))
    g_above = count2(lambda sc, pick, kt: sc > pick(g_thr[0], g_thr[1]))
    bad = [jnp.where(g_found[t] & ((g_at[t] < n_sel) | (g_above[t] >= n_sel)), 1, 0) for t in range(2)]
    confirmed = jnp.max(jnp.maximum(bad[0], bad[1])) == 0

    def bit_step(i, carry):
        pref, at_thr = carry
        cand = tuple(pr | jnp.left_shift(jnp.int32(1), 31 - i) for pr in pref)
        cf = tuple(_radix_candidate(c) for c in cand)
        cnt = count2(lambda sc, pick, kt: sc >= pick(cf[0], cf[1]))
        take = tuple((cnt[t] >= n_sel) & (cf[t] > NEG_INF) for t in range(2))
        return (tuple(jnp.where(take[t], cand[t], pref[t]) for t in range(2)),
                tuple(jnp.where(take[t], cnt[t], at_thr[t]) for t in range(2)))

    def float_search():
        pref, at_thr = lax.fori_loop(0, 32, bit_step, ((zero_q, zero_q), (zero_q, zero_q)))
        thr = tuple(jnp.where(pref[t] != 0, _radix_candidate(pref[t]), NEG_INF) for t in range(2))
        return pref, at_thr, count2(lambda sc, pick, kt: sc > pick(thr[0], thr[1]))

    at_fast = tuple(jnp.where(g_found[t], g_at[t], 0) for t in range(2))
    pref, at_thr, above = lax.cond(confirmed, lambda: (guess, at_fast, g_above), float_search)
    found = tuple(pr != 0 for pr in pref)
    thr = tuple(jnp.where(found[t], _radix_candidate(pref[t]), NEG_INF) for t in range(2))

    need = tuple(n_sel - a for a in above)
    any_tie = jnp.max(jnp.maximum(jnp.where(at_thr[0] > n_sel, 1, 0), jnp.where(at_thr[1] > n_sel, 1, 0))) > 0
    pos_bits = max(1, (ikb_ref.shape[0] - 1).bit_length())
    row_k = lax.broadcasted_iota(I32, (KEY_TILE, Q_TILE), 0)

    def last_kept_tie():
        def step(i, pre):
            cand = tuple(pr | jnp.left_shift(jnp.int32(1), pos_bits - 1 - i) for pr in pre)
            cnt = count2(lambda sc, pick, kt: (sc == pick(thr[0], thr[1])) & (kt * KEY_TILE + row_k < pick(cand[0], cand[1])))
            return tuple(jnp.where(cnt[t] < need[t], cand[t], pre[t]) for t in range(2))
        return lax.fori_loop(0, pos_bits, step, (zero_q, zero_q))

    big = jnp.full((1, Q_TILE), INT_MAX, I32)
    last_tie = lax.cond(any_tie, last_kept_tie, lambda: (big, big))
    last_tie = tuple(jnp.where(found[t], last_tie[t], -1) for t in range(2))

    for u in range(n_units):
        is_b, _, kt, _ = unit(u)
        t_u = jnp.where(is_b, thr[1], thr[0])
        last_u = jnp.where(is_b, last_tie[1], last_tie[0])
        sc = keys_ref[u]
        keep = (sc > t_u) | ((sc == t_u) & (kt * KEY_TILE + row_k <= last_u))
        bias_ref[u] = jnp.where(keep, 0.0, MASKED)

    neg = jnp.full((8, gw), MASKED, F32)
    cmax = [[neg] * ATT_KV_HEADS, [neg] * ATT_KV_HEADS]
    for u in range(n_units):
        is_b, _, _, _ = unit(u)
        for mc in range(2):
            b_rows = bias_ref[u, mc * half:(mc + 1) * half, :]
            bias_g = jnp.concatenate([b_rows] * group, axis=1)
            for g in range(ATT_KV_HEADS):
                part = jnp.max((logit_ref[u, mc, g] + bias_g).reshape(half // 8, 8, gw), axis=0)
                cmax[0][g] = jnp.maximum(cmax[0][g], jnp.where(is_b, neg, part))
                cmax[1][g] = jnp.maximum(cmax[1][g], jnp.where(is_b, part, neg))
    cmax = [[jnp.max(c, axis=0, keepdims=True) for c in cm] for cm in cmax]

    zsum = jnp.zeros((8, cw), F32)
    lsum = [[[zsum] * ct_per_group for _ in range(ATT_KV_HEADS)] for _ in range(2)]
    for u in range(n_units):
        is_b, _, kt, _ = unit(u)
        for g in range(ATT_KV_HEADS):
            m_g = jnp.where(is_b, cmax[1][g], cmax[0][g])
            for ct in range(ct_per_group):
                cols = slice(ct * cw, (ct + 1) * cw)
                contrib = jnp.zeros((ATT_HEAD_DIM, cw), F32)
                for mc in range(2):
                    b_rows = bias_ref[u, mc * half:(mc + 1) * half, :]
                    bias_c = jnp.concatenate([b_rows] * (cw // Q_TILE), axis=1)
                    pr = jnp.exp(logit_ref[u, mc, g, :, cols] + bias_c - m_g[:, cols])
                    part = jnp.sum(pr.reshape(half // 8, 8, cw), axis=0)
                    lsum[0][g][ct] = lsum[0][g][ct] + jnp.where(is_b, zsum, part)
                    lsum[1][g][ct] = lsum[1][g][ct] + jnp.where(is_b, part, zsum)
                    v_t = avb_ref[kt, g * ATT_HEAD_DIM:(g + 1) * ATT_HEAD_DIM, mc * half:(mc + 1) * half]
                    contrib = contrib + _dot(v_t, pr.astype(BF16))
                pv_ref[u, g, :, cols] = contrib

    for g in range(ATT_KV_HEADS):
        for ct in range(ct_per_group):
            cols = slice(ct * cw, (ct + 1) * cw)
            total = pv_ref[0, g, :, cols]
            for u in range(1, n_units):
                total = total + pv_ref[u, g, :, cols]
            part_a = pv_ref[0, g, :, cols]
            for u in range(1, max_a):
                part_a = part_a + jnp.where(u < n_a, pv_ref[u, g, :, cols], 0.0)
            for s, (o_ref, acc) in enumerate(((oa_ref, part_a), (ob_ref, total - part_a))):
                den = jnp.sum(lsum[s][g][ct], axis=0, keepdims=True)
                out_t = acc / den
                blk = jnp.concatenate([out_t[:, 0:Q_TILE], out_t[:, Q_TILE:cw]], axis=0)
                c_out = (g * ct_per_group + ct) * 2 * ATT_HEAD_DIM
                o_ref[0, 0, :, c_out:c_out + 2 * ATT_HEAD_DIM] = blk.T.astype(o_ref.dtype)


def _page_copy(pt_ref, src_hbm, layer, buf, sem, seq, slot, page):
    def copy(p):
        dst = buf.at[(slot,) + (slice(None),) * (len(buf.shape) - 2) + (pl.ds(pl.multiple_of(p * page, page), page),)]
        return pltpu.make_async_copy(src_hbm.at[layer, pt_ref[seq, p]], dst, sem.at[slot])
    return copy


def _start_pages(copy, n_pages):
    def body(p, c):
        copy(p).start()
        return c
    lax.fori_loop(0, n_pages, body, 0, unroll=8 if n_pages % 8 == 0 else 1)


def _wait_slot(buf, sem, slot):
    pltpu.make_async_copy(buf.at[1 - slot], buf.at[slot], sem.at[slot]).wait()


def _sample_scores_kernel(pt_ref, iq_ref, iw_ref, ikn_ref, kidx_hbm, sc_ref, buf, sem, *, layer, n_pages, page, tile):
    b = pl.program_id(0)
    nb = pl.num_programs(0)
    slot = b % 2

    @pl.when(b == 0)
    def _():
        _start_pages(_page_copy(pt_ref, kidx_hbm, layer, buf, sem, 0, 0, page), n_pages)

    @pl.when(b + 1 < nb)
    def _():
        _start_pages(_page_copy(pt_ref, kidx_hbm, layer, buf, sem, b + 1, 1 - slot, page), n_pages)

    _wait_slot(buf, sem, slot)

    iq = iq_ref[0]
    iqb = iq.astype(BF16)
    iw = iw_ref[0]
    past = n_pages * page
    for t in range(past // tile):
        kb = buf[slot, :, t * tile:(t + 1) * tile].astype(BF16)
        d = _dot(iqb, kb)
        sc_ref[0, :, t * tile:(t + 1) * tile] = jnp.sum(jnp.maximum(d, 0.0) * iw, axis=0, keepdims=True)
    d_new = jnp.sum(iq * ikn_ref[0], axis=1, keepdims=True)
    s_new = jnp.sum(jnp.maximum(d_new, 0.0) * iw, axis=0, keepdims=True)
    lane = lax.broadcasted_iota(I32, (1, V7X_LANES), 1)
    sc_ref[0, :, past:past + V7X_LANES] = jnp.where(lane == 0, s_new, 0.0)


def _sample_select_kernel(sc_ref, bias_ref, keys_ref, *, n_sel, past):
    rows, width = sc_ref.shape
    col = lax.broadcasted_iota(I32, (rows, width), 1)
    keys_ref[...] = jnp.where(col <= past, sc_ref[...], NEG_INF)

    def count(pred):
        return jnp.sum(pred(keys_ref[...]).astype(I32), axis=1, keepdims=True)

    def bit_step(i, carry):
        prefix, at_thr = carry
        cand = prefix | jnp.left_shift(jnp.int32(1), 31 - i)
        cf = _radix_candidate(cand)
        cnt = count(lambda sc: sc >= cf)
        take = (cnt >= n_sel) & (cf > NEG_INF)
        return jnp.where(take, cand, prefix), jnp.where(take, cnt, at_thr)

    zero = jnp.zeros((rows, 1), I32)
    prefix, _ = lax.fori_loop(0, 32, bit_step, (zero, zero))
    found = prefix != 0
    thr = jnp.where(found, _radix_candidate(prefix), NEG_INF)
    need = n_sel - count(lambda sc: sc > thr)
    pos_bits = max(1, (width - 1).bit_length())

    def pos_step(i, pre):
        cand = pre | jnp.left_shift(jnp.int32(1), pos_bits - 1 - i)
        cnt = count(lambda sc: (sc == thr) & (col < cand))
        return jnp.where(cnt < need, cand, pre)

    last_tie = jnp.where(found, lax.fori_loop(0, pos_bits, pos_step, zero), -1)
    sc = keys_ref[...]
    keep = (sc > thr) | ((sc == thr) & (col <= last_tie))
    bias_ref[...] = jnp.where(keep, 0.0, MASKED)


def _sample_attention_kernel(pt_ref, aq_ref, gq_ref, akn_ref, av_ref, bias_ref, k_hbm, v_hbm, o_ref,
                             kbuf, vbuf, logit_ref, ksem, vsem, *, layer, n_pages, page, tile):
    b = pl.program_id(0)
    nb = pl.num_programs(0)
    slot = b % 2

    def start(seq, s):
        _start_pages(_page_copy(pt_ref, k_hbm, layer, kbuf, ksem, seq, s, page), n_pages)
        _start_pages(_page_copy(pt_ref, v_hbm, layer, vbuf, vsem, seq, s, page), n_pages)

    @pl.when(b == 0)
    def _():
        start(0, 0)

    @pl.when(b + 1 < nb)
    def _():
        start(b + 1, 1 - slot)

    _wait_slot(kbuf, ksem, slot)
    _wait_slot(vbuf, vsem, slot)

    n_heads = aq_ref.shape[1]
    group = n_heads // ATT_KV_HEADS
    kvw = ATT_KV_HEADS * ATT_HEAD_DIM
    past = n_pages * page
    q = _rms(aq_ref[0], -1) * gq_ref[...]
    zeros = jnp.zeros_like(q)
    head = lax.broadcasted_iota(I32, (n_heads, kvw), 0)
    q_pad = jnp.where(head < group, jnp.concatenate([q, zeros], axis=1), jnp.concatenate([zeros, q], axis=1))
    q_pad_b = q_pad.astype(BF16)

    for t in range(past // tile):
        cols = slice(t * tile, (t + 1) * tile)
        kb = kbuf[slot, :, :, cols].reshape(kvw, tile).astype(BF16)
        logit_ref[:, cols] = _dot(q_pad_b, kb) + bias_ref[0, :, cols]
    lane = lax.broadcasted_iota(I32, (n_heads, V7X_LANES), 1)
    l_new = jnp.sum(q_pad * akn_ref[0], axis=1, keepdims=True) + bias_ref[0, :, past:past + 1]
    logit_ref[:, past:past + V7X_LANES] = jnp.where(lane == 0, l_new, MASKED)

    m = jnp.max(logit_ref[...], axis=1, keepdims=True)
    lsum = jnp.zeros((n_heads, 1), F32)
    acc = jnp.zeros((kvw, n_heads), F32)
    for t in range(past // tile):
        cols = slice(t * tile, (t + 1) * tile)
        p = jnp.exp(logit_ref[:, cols] - m)
        lsum = lsum + jnp.sum(p, axis=1, keepdims=True)
        acc = acc + _dot_nt(vbuf[slot, :, :, cols].reshape(kvw, tile).astype(BF16), p.astype(BF16))
    p_new = jnp.exp(logit_ref[:, past:past + 1] - m)
    acc = acc + _row_to_col(av_ref[0]) * _col_to_row(p_new)
    o_ref[0] = acc / _col_to_row(lsum + p_new)


def _outproj_mlp_kernel(x_ref, atta_ref, attb_ref, rec_ref, wo_ref, g2_ref, wup_ref, wdn_ref, o_ref, *, ff_tile, tiles_per_seq):
    n_t = atta_ref.shape[1]
    att = jnp.concatenate([atta_ref[0, k] for k in range(n_t)], axis=0)
    if tiles_per_seq:
        late = jnp.concatenate([attb_ref[0, n_t - 1 - k] for k in range(n_t)], axis=0)
        att = jnp.where(pl.program_id(0) % tiles_per_seq < tiles_per_seq // 2, att, late)
    mix = jnp.concatenate([att.astype(BF16), rec_ref[...].astype(BF16)], axis=1)
    y = x_ref[...] + _dot(mix, wo_ref[...])
    h2 = (_rms(y, -1) * g2_ref[...]).astype(BF16)
    acc = y
    for c in range(wup_ref.shape[1] // ff_tile):
        u = jnp.maximum(_dot(h2, wup_ref[:, c * ff_tile:(c + 1) * ff_tile]), 0.0)
        acc = acc + _dot((u * u).astype(BF16), wdn_ref[c * ff_tile:(c + 1) * ff_tile, :])
    o_ref[...] = acc


def _row_tile(n_rows, want):
    return want if n_rows % want == 0 else n_rows


def _layer_weights(w_in, att_width, kv_width, idx_width, rec_width):
    o = [0]
    for wdt in (att_width, kv_width, kv_width, idx_width, IDX_DIM, IDX_HEADS, rec_width, rec_width, rec_width, rec_width):
        o.append(o[-1] + wdt)
    aq, ak, av, iq, ik, iw, rq, rf, ri, rg = (w_in[:, o[i]:o[i + 1]] for i in range(10))
    d = w_in.shape[0]
    z = lambda n: jnp.zeros((d, n), w_in.dtype)
    w_main = jnp.concatenate([ak, ik, z(128 - IDX_DIM), rq, rf, ri, rg], axis=1).astype(BF16)
    w_tok = jnp.concatenate([aq, iq, av, ak, ik, iw, z(16 - IDX_HEADS)], axis=1).T.astype(BF16)
    w_qry = jnp.concatenate([aq, iq, av, iw, z(128 - IDX_HEADS)], axis=1).astype(BF16)
    return w_main, w_tok, w_qry


def _outproj_mlp(x, att_a, att_b, rec, w_out, g2, w_up, w_dn, name):
    n, d = x.shape
    tm = _row_tile(n, 512)
    ff = w_up.shape[1]
    width = att_a.shape[-1]
    if att_b is None:
        assert att_a.shape[:3] == (1, 1, n)
        tps, att_b = 0, att_a
        spec_a = spec_b = pl.BlockSpec((1, 1, tm, width), lambda i: (0, 0, i, 0))
    else:
        rows = att_a.shape[2]
        n_t = tm // rows
        tps = att_a.shape[1] * 2 * rows // tm
        assert tm % rows == 0 and tps % 2 == 0 and tps * tm == 2 * att_a.shape[1] * rows
        half = tps // 2
        spec_a = pl.BlockSpec((1, n_t, rows, width), lambda i: (i // tps, jnp.minimum(i % tps, half - 1), 0, 0))
        spec_b = pl.BlockSpec((1, n_t, rows, width), lambda i: (i // tps, jnp.clip(tps - 1 - i % tps, 0, half - 1), 0, 0))
    return pl.pallas_call(
        functools.partial(_outproj_mlp_kernel, ff_tile=min(ff, 1024), tiles_per_seq=tps),
        grid=(n // tm,),
        in_specs=[pl.BlockSpec((tm, d), lambda i: (i, 0)), spec_a, spec_b,
                  pl.BlockSpec((tm, rec.shape[1]), lambda i: (i, 0)),
                  _resident(w_out.shape), _resident(g2.shape), _resident(w_up.shape), _resident(w_dn.shape)],
        out_specs=pl.BlockSpec((tm, d), lambda i: (i, 0)),
        out_shape=jax.ShapeDtypeStruct((n, d), F32),
        compiler_params=_params("arbitrary"),
        name=name,
    )(x, att_a, att_b, rec, w_out, g2, w_up, w_dn)


def _prompt_attention(iqt, aqt, iwt, ikb, akb, avb, bsz, seq):
    n = bsz * seq
    kvw = ATT_KV_HEADS * ATT_HEAD_DIM
    att_width, idx_width = aqt.shape[0], iqt.shape[0]
    n_sel = min(TOPK_MAX, seq // 4)
    nq = seq // Q_TILE
    assert KEY_TILE == 2 * Q_TILE and nq % 2 == 0, "query tiles are paired so every step sees nq/2 + 1 key tiles"
    n_pairs = nq // 2
    n_units = n_pairs + 1
    kt_per_seq = seq // KEY_TILE
    avb_tiles = avb.reshape(kvw, n // KEY_TILE, KEY_TILE).transpose(1, 0, 2)
    n_heads = att_width // ATT_HEAD_DIM
    gw2 = n_heads * Q_TILE
    qa = lambda h: pl.BlockSpec((h, Q_TILE), lambda b, p: (0, b * nq + p))
    qb = lambda h: pl.BlockSpec((h, Q_TILE), lambda b, p: (0, b * nq + nq - 1 - p))
    outb = pl.BlockSpec((1, 1, Q_TILE, att_width), lambda b, p: (b, p, 0, 0))
    return pl.pallas_call(
        functools.partial(_prompt_attention_kernel, n_sel=n_sel, n_units=n_units),
        grid=(bsz, n_pairs),
        in_specs=[qa(idx_width), qb(idx_width), qa(att_width), qb(att_width), qa(IDX_HEADS), qb(IDX_HEADS),
                  pl.BlockSpec((seq, IDX_DIM), lambda b, p: (b, 0)),
                  pl.BlockSpec((seq, kvw), lambda b, p: (b, 0)),
                  pl.BlockSpec((kt_per_seq, kvw, KEY_TILE), lambda b, p: (b, 0, 0))],
        out_specs=[outb, outb],
        out_shape=[jax.ShapeDtypeStruct((bsz, n_pairs, Q_TILE, att_width), BF16)] * 2,
        scratch_shapes=[pltpu.VMEM((2, IDX_DIM, IDX_HEADS * Q_TILE), BF16),
                        pltpu.VMEM((2, ATT_KV_HEADS, kvw, gw2 // ATT_KV_HEADS), BF16),
                        pltpu.VMEM((2, IDX_HEADS, Q_TILE), F32),
                        pltpu.VMEM((n_units, KEY_TILE, Q_TILE), F32),
                        pltpu.VMEM((n_units, 32, 8, Q_TILE), I32),
                        pltpu.VMEM((n_units, KEY_TILE, Q_TILE), F32),
                        pltpu.VMEM((n_units, 2, ATT_KV_HEADS, KEY_TILE // 2, gw2 // ATT_KV_HEADS), F32),
                        pltpu.VMEM((n_units, ATT_KV_HEADS, ATT_HEAD_DIM, gw2 // ATT_KV_HEADS), F32)],
        compiler_params=_params("arbitrary", "arbitrary"),
        name="attention_prompt",
    )(iqt, iqt, aqt, aqt, iwt, iwt, ikb, akb, avb_tiles)


def _prompt_layer(x, lbs, layer, w_main, w_tok, g1, gq_col, gk_row, gi_row, gk_col, gi_col, gn_row, w_out, g2,
                  w_up, w_dn, att_width, idx_width, rec_width, idx_w_scale):
    bsz, seq, d = x.shape
    n = bsz * seq
    xf = x.reshape(n, d)
    tm = _row_tile(seq, 512)
    tps = seq // tm
    kvw = ATT_KV_HEADS * ATT_HEAD_DIM
    row = lambda w: pl.BlockSpec((tm, w), lambda i: (i, 0))
    colb = lambda h: pl.BlockSpec((h, tm), lambda i: (0, i))
    seqb = lambda h: pl.BlockSpec((1, h, tm), lambda i: (i // tps, 0, i % tps))
    akb, ikb, rec4, aqt, iqt, avb, iwt, akt, avt, ikt = pl.pallas_call(
        functools.partial(_inproj_prompt_kernel, idx_w_scale=idx_w_scale),
        grid=(n // tm,),
        in_specs=[row(d), _resident(g1.shape), _resident(w_main.shape), _resident(w_tok.shape),
                  _resident(gk_row.shape), _resident(gi_row.shape), _resident(gq_col.shape),
                  _resident(gk_col.shape), _resident(gi_col.shape)],
        out_specs=[row(kvw), row(IDX_DIM), row(4 * rec_width),
                   colb(att_width), colb(idx_width), colb(kvw), colb(IDX_HEADS),
                   seqb(kvw), seqb(kvw), seqb(IDX_DIM)],
        out_shape=[jax.ShapeDtypeStruct((n, kvw), BF16), jax.ShapeDtypeStruct((n, IDX_DIM), BF16),
                   jax.ShapeDtypeStruct((n, 4 * rec_width), F32),
                   jax.ShapeDtypeStruct((att_width, n), BF16), jax.ShapeDtypeStruct((idx_width, n), BF16),
                   jax.ShapeDtypeStruct((kvw, n), BF16), jax.ShapeDtypeStruct((IDX_HEADS, n), F32),
                   jax.ShapeDtypeStruct((bsz, kvw, seq), F32), jax.ShapeDtypeStruct((bsz, kvw, seq), F32),
                   jax.ShapeDtypeStruct((bsz, IDX_DIM, seq), F32)],
        compiler_params=_params("arbitrary"),
        name="inproj_prompt",
    )(xf, g1, w_main, w_tok, gk_row, gi_row, gq_col, gk_col, gi_col)

    tb = _row_tile(seq, 256)
    n_rec = rec_width // REC_DIM
    tiles = seq // tb
    gs = 2 if bsz % 2 == 0 else 1

    def hgrn(direct):
        return pl.pallas_call(
            functools.partial(_hgrn_prompt_kernel, layer=layer, rec_width=rec_width, direct=direct),
            grid=(bsz // gs, tiles),
            in_specs=[_resident(lbs.shape), pl.BlockSpec((gs, tb, 4 * rec_width), lambda b, t: (b, t, 0)),
                      _resident(gn_row.shape)],
            out_specs=[pl.BlockSpec((gs, tb, rec_width), lambda b, t: (b, t, 0)),
                       pl.BlockSpec((gs, n_rec, REC_DIM, REC_DIM), lambda b, t: (b, 0, 0, 0))],
            out_shape=[jax.ShapeDtypeStruct((bsz, seq, rec_width), BF16),
                       jax.ShapeDtypeStruct((bsz, n_rec, REC_DIM, REC_DIM), F32)],
            scratch_shapes=[pltpu.VMEM((gs, n_rec, REC_DIM, REC_DIM), F32),
                            pltpu.VMEM((CHUNK, rec_width), F32), pltpu.VMEM((CHUNK, rec_width), F32)],
            compiler_params=_params("arbitrary", "arbitrary"),
            name="hgrn_prompt_direct" if direct else "hgrn_prompt",
        )

    lb_min = jnp.min(jnp.cumsum(jax.nn.softmax(lbs, axis=0), axis=0)[layer])
    factorable = lb_min >= float(np.exp(-HGRN_SAFE_SPAN / (CHUNK // 2)))
    rec, s_fin = lax.cond(factorable, hgrn(False), hgrn(True), lbs, rec4.reshape(bsz, seq, 4 * rec_width), gn_row)
    rec = rec.reshape(n, rec_width)

    att_a, att_b = _prompt_attention(iqt, aqt, iwt, ikb, akb, avb, bsz, seq)
    if (seq // tm) % 2:
        att_a = jnp.concatenate([att_a, att_b[:, ::-1]], axis=1).reshape(1, 1, n, att_width)
        att_b = None

    y = _outproj_mlp(xf, att_a, att_b, rec, w_out, g2, w_up, w_dn, "outproj_mlp_prompt")
    to_heads = lambda t: t.reshape(bsz, ATT_KV_HEADS, ATT_HEAD_DIM, seq).transpose(0, 3, 1, 2)
    return y.reshape(bsz, seq, d), to_heads(akt), to_heads(avt), ikt.transpose(0, 2, 1), s_fin


def _sample_layer(x, s0, lbs, layer, cache_k, cache_v, cache_kidx, page_table, w_main, w_qry, g1, gq_row,
                  gk_row, gi_row, gn_row, w_out, g2, w_up, w_dn, att_width, idx_width, rec_width, idx_w_scale):
    bd, t_new, d = x.shape
    assert t_new == 1, "the sample path handles one new token per sequence"
    kvw = ATT_KV_HEADS * ATT_HEAD_DIM
    n_heads = att_width // ATT_HEAD_DIM
    n_rec = rec_width // REC_DIM
    page = cache_k.shape[2]
    n_pages = page_table.shape[1]
    past = n_pages * page
    width = past + V7X_LANES
    tile = 2048 if past % 2048 == 0 else past
    xf = x.reshape(bd, d)

    full = lambda shape: _resident(shape)
    akn, av, ikn, rec4, aq, iq, iw = pl.pallas_call(
        functools.partial(_inproj_sample_kernel, idx_w_scale=idx_w_scale),
        grid=(1,),
        in_specs=[full(xf.shape), full(g1.shape), full(w_main.shape), full(w_qry.shape),
                  full(gk_row.shape), full(gi_row.shape)],
        out_specs=[full((bd, kvw)), full((bd, kvw)), full((bd, IDX_DIM)), full((bd, 4 * rec_width)),
                   full((bd, att_width)), full((bd, idx_width)), full((bd, 128))],
        out_shape=[jax.ShapeDtypeStruct((bd, kvw), F32), jax.ShapeDtypeStruct((bd, kvw), F32),
                   jax.ShapeDtypeStruct((bd, IDX_DIM), F32), jax.ShapeDtypeStruct((bd, 4 * rec_width), F32),
                   jax.ShapeDtypeStruct((bd, att_width), F32), jax.ShapeDtypeStruct((bd, idx_width), F32),
                   jax.ShapeDtypeStruct((bd, 128), F32)],
        compiler_params=_params("arbitrary"),
        name="inproj_sample",
    )(xf, g1, w_main, w_qry, gk_row, gi_row)

    sb = 4 if bd % 4 == 0 else 1
    rec, s1 = pl.pallas_call(
        functools.partial(_hgrn_step_kernel, layer=layer, rec_width=rec_width),
        grid=(bd // sb,),
        in_specs=[_resident(lbs.shape), pl.BlockSpec((sb, 1, 4 * rec_width), lambda b: (b, 0, 0)),
                  _resident(gn_row.shape), pl.BlockSpec((sb, n_rec, REC_DIM, REC_DIM), lambda b: (b, 0, 0, 0))],
        out_specs=[pl.BlockSpec((sb, 1, rec_width), lambda b: (b, 0, 0)),
                   pl.BlockSpec((sb, n_rec, REC_DIM, REC_DIM), lambda b: (b, 0, 0, 0))],
        out_shape=[jax.ShapeDtypeStruct((bd, 1, rec_width), BF16),
                   jax.ShapeDtypeStruct((bd, n_rec, REC_DIM, REC_DIM), F32)],
        compiler_params=_params("arbitrary"),
        name="hgrn_step",
    )(lbs, rec4.reshape(bd, 1, 4 * rec_width), gn_row, s0)

    kidx_t = cache_kidx.transpose(0, 1, 3, 2)
    k_t = cache_k.transpose(0, 1, 3, 4, 2)
    v_t = cache_v.transpose(0, 1, 3, 4, 2)
    iq3 = iq.reshape(bd, IDX_HEADS, IDX_DIM)
    iw3 = iw[:, 0:IDX_HEADS].reshape(bd, IDX_HEADS, 1)
    ikn3 = ikn.reshape(bd, 1, IDX_DIM)
    scores = pl.pallas_call(
        functools.partial(_sample_scores_kernel, layer=layer, n_pages=n_pages, page=page, tile=tile),
        grid_spec=pltpu.PrefetchScalarGridSpec(
            num_scalar_prefetch=1,
            grid=(bd,),
            in_specs=[pl.BlockSpec((1, IDX_HEADS, IDX_DIM), lambda b, pt: (b, 0, 0)),
                      pl.BlockSpec((1, IDX_HEADS, 1), lambda b, pt: (b, 0, 0)),
                      pl.BlockSpec((1, 1, IDX_DIM), lambda b, pt: (b, 0, 0)),
                      pl.BlockSpec(memory_space=pl.ANY)],
            out_specs=pl.BlockSpec((1, 1, width), lambda b, pt: (b, 0, 0)),
            scratch_shapes=[pltpu.VMEM((2, IDX_DIM, past), F32), pltpu.SemaphoreType.DMA((2,))]),
        out_shape=jax.ShapeDtypeStruct((bd, 1, width), F32),
        compiler_params=_params("arbitrary"),
        name="sample_scores",
    )(page_table, iq3, iw3, ikn3, kidx_t)

    n_sel = min(TOPK_MAX, (past + 1) // 4)
    bias = pl.pallas_call(
        functools.partial(_sample_select_kernel, n_sel=n_sel, past=past),
        grid=(1,),
        in_specs=[full((bd, width))],
        out_specs=full((bd, width)),
        out_shape=jax.ShapeDtypeStruct((bd, width), F32),
        scratch_shapes=[pltpu.VMEM((bd, width), F32)],
        compiler_params=_params("arbitrary"),
        name="sample_select",
    )(scores.reshape(bd, width))

    att_t = pl.pallas_call(
        functools.partial(_sample_attention_kernel, layer=layer, n_pages=n_pages, page=page, tile=tile),
        grid_spec=pltpu.PrefetchScalarGridSpec(
            num_scalar_prefetch=1,
            grid=(bd,),
            in_specs=[pl.BlockSpec((1, n_heads, ATT_HEAD_DIM), lambda b, pt: (b, 0, 0)),
                      pl.BlockSpec((1, ATT_HEAD_DIM), lambda b, pt: (0, 0)),
                      pl.BlockSpec((1, 1, kvw), lambda b, pt: (b, 0, 0)),
                      pl.BlockSpec((1, 1, kvw), lambda b, pt: (b, 0, 0)),
                      pl.BlockSpec((1, 1, width), lambda b, pt: (b, 0, 0)),
                      pl.BlockSpec(memory_space=pl.ANY), pl.BlockSpec(memory_space=pl.ANY)],
            out_specs=pl.BlockSpec((1, kvw, n_heads), lambda b, pt: (b, 0, 0)),
            scratch_shapes=[pltpu.VMEM((2, ATT_KV_HEADS, ATT_HEAD_DIM, past), F32),
                            pltpu.VMEM((2, ATT_KV_HEADS, ATT_HEAD_DIM, past), F32),
                            pltpu.VMEM((n_heads, width), F32),
                            pltpu.SemaphoreType.DMA((2,)), pltpu.SemaphoreType.DMA((2,))]),
        out_shape=jax.ShapeDtypeStruct((bd, kvw, n_heads), F32),
        compiler_params=_params("arbitrary"),
        name="sample_attention",
    )(page_table, aq.reshape(bd, n_heads, ATT_HEAD_DIM), gq_row, akn.reshape(bd, 1, kvw), av.reshape(bd, 1, kvw),
      bias.reshape(bd, 1, width), k_t, v_t)
    att_g = att_t.reshape(bd, ATT_KV_HEADS, ATT_HEAD_DIM, n_heads)
    group = n_heads // ATT_KV_HEADS
    att = jnp.stack([att_g[:, h // group, :, h] for h in range(n_heads)], axis=1)

    y = _outproj_mlp(xf, att.reshape(1, 1, bd, att_width), None, rec.reshape(bd, rec_width), w_out, g2, w_up, w_dn,
                     "outproj_mlp_sample")
    return (y.reshape(bd, 1, d), akn.reshape(bd, 1, ATT_KV_HEADS, ATT_HEAD_DIM),
            av.reshape(bd, 1, ATT_KV_HEADS, ATT_HEAD_DIM), ikn.reshape(bd, 1, IDX_DIM), s1)


def kernel(x_prompt, x_sample, cache_k, cache_v, cache_kidx, state_hgrn, page_table, norm1_g, w_in, q_norm_g,
           k_norm_g, idx_k_norm_g, lower_bounds, rec_norm_g, w_out, norm2_g, w_up, w_down):
    depth = w_in.shape[0]
    d_model = x_prompt.shape[-1]
    kv_width = ATT_KV_HEADS * ATT_HEAD_DIM
    idx_width = IDX_HEADS * IDX_DIM
    rec_width = lower_bounds.shape[-1]
    att_width = w_out.shape[1] - rec_width
    assert w_in.shape[-1] == att_width + 2 * kv_width + idx_width + IDX_DIM + IDX_HEADS + 4 * rec_width
    assert rec_norm_g.shape[-1] == REC_DIM and q_norm_g.shape[-1] == ATT_HEAD_DIM
    idx_w_scale = (IDX_HEADS ** -0.5) * (IDX_DIM ** -0.5)
    att_scale = ATT_HEAD_DIM ** -0.5
    n_heads = att_width // ATT_HEAD_DIM
    lbs = lower_bounds.astype(F32)

    yp, ys = x_prompt, x_sample
    outs = [[] for _ in range(8)]
    for l in range(depth):
        w_main, w_tok, w_qry = _layer_weights(w_in[l], att_width, kv_width, idx_width, rec_width)
        g1 = norm1_g[l].reshape(1, d_model)
        g2 = norm2_g[l].reshape(1, d_model)
        gq_row = (q_norm_g[l] * att_scale).reshape(1, ATT_HEAD_DIM)
        gq_col = jnp.tile(q_norm_g[l] * att_scale, n_heads).reshape(att_width, 1)
        gk_row = jnp.tile(k_norm_g[l], ATT_KV_HEADS).reshape(1, kv_width)
        gi_row = idx_k_norm_g[l].reshape(1, IDX_DIM)
        gk_col = gk_row.reshape(kv_width, 1)
        gi_col = gi_row.reshape(IDX_DIM, 1)
        gn_row = rec_norm_g[l].reshape(1, REC_DIM)
        wo, wu, wd = w_out[l].astype(BF16), w_up[l].astype(BF16), w_down[l].astype(BF16)
        yp, kp, vp, ip, sp = _prompt_layer(yp, lbs, l, w_main, w_tok, g1, gq_col, gk_row, gi_row, gk_col, gi_col,
                                           gn_row, wo, g2, wu, wd, att_width, idx_width, rec_width, idx_w_scale)
        ys, k_s, v_s, i_s, s_s = _sample_layer(ys, state_hgrn[l], lbs, l, cache_k, cache_v, cache_kidx,
                                               page_table, w_main, w_qry, g1, gq_row, gk_row, gi_row, gn_row, wo,
                                               g2, wu, wd, att_width, idx_width, rec_width, idx_w_scale)
        for acc, val in zip(outs, (kp, vp, ip, sp, k_s, v_s, i_s, s_s)):
            acc.append(val)
    return (yp, ys) + tuple(jnp.stack(o) for o in outs)
```

```python
import functools

import jax
import jax.numpy as jnp
import numpy as np
from jax import lax
from jax.experimental import pallas as pl
from jax.experimental.pallas import tpu as pltpu

F32 = jnp.float32
BF16 = jnp.bfloat16
I32 = jnp.int32

EPS = 1e-6
MASKED = -1e30
DEN_FLOOR = 1e-17
INT_MIN = -(2 ** 31)
NEG_INF = float("-inf")
INT_MAX = 2 ** 31 - 1

ATT_HEAD_DIM = 64
ATT_KV_HEADS = 2
IDX_HEADS = 8
IDX_DIM = 64
REC_DIM = 128
TOPK_MAX = 256
CHUNK = 64
HGRN_SAFE_SPAN = 60.0

V7X_LANES = 128
V7X_VMEM_LIMIT_BYTES = 56 * 1024 * 1024
Q_TILE = 128
KEY_TILE = 256


def _dot(a, b):
    return jnp.dot(a, b, preferred_element_type=F32)


def _dot_nt(a, b):
    return lax.dot_general(a, b, (((1,), (1,)), ((), ())), preferred_element_type=F32)


def _dot_tn(a, b):
    return lax.dot_general(a, b, (((0,), (0,)), ((), ())), preferred_element_type=F32)


def _params(*semantics):
    return pltpu.CompilerParams(dimension_semantics=semantics, vmem_limit_bytes=V7X_VMEM_LIMIT_BYTES)


def _resident(shape):
    nd = len(shape)
    return pl.BlockSpec(shape, lambda *_: (0,) * nd)


def _rms(x, axis):
    return x * lax.rsqrt(jnp.mean(x * x, axis=axis, keepdims=True) + EPS)


def _sigmoid_pair(x):
    t = jnp.exp(-jnp.abs(x))
    r = 1.0 / (1.0 + t)
    tr = t * r
    pos = x >= 0
    return jnp.where(pos, r, tr), jnp.where(pos, tr, r)


def _silu(x):
    s, _ = _sigmoid_pair(x)
    return x * s


def _radix_candidate(prefix_bits):
    key = prefix_bits ^ INT_MIN
    return lax.bitcast_convert_type(key ^ ((key >> 31) & INT_MAX), F32)


def _lower_bound(lbs_ref, layer):
    lbs = lbs_ref[...]
    e = jnp.exp(lbs - jnp.max(lbs, axis=0, keepdims=True))
    p = e / jnp.sum(e, axis=0, keepdims=True)
    return jnp.sum(p[0:layer + 1, :], axis=0, keepdims=True)


def _head_rms_lanes(x, gain_row):
    sq = x * x
    lane = lax.broadcasted_iota(I32, x.shape, 1)
    lo = lane < ATT_HEAD_DIM
    s_lo = jnp.sum(jnp.where(lo, sq, 0.0), axis=-1, keepdims=True)
    s_hi = jnp.sum(jnp.where(lo, 0.0, sq), axis=-1, keepdims=True)
    inv = jnp.where(lo, lax.rsqrt(s_lo / ATT_HEAD_DIM + EPS), lax.rsqrt(s_hi / ATT_HEAD_DIM + EPS))
    return x * inv * gain_row


def _inproj_prompt_kernel(x_ref, g1_ref, wm_ref, wt_ref, gk_ref, gi_ref, gq_ref, gkc_ref, gic_ref,
                          akb_ref, ikb_ref, rec4_ref, aqt_ref, iqt_ref, avb_ref, iwt_ref, akt_ref, avt_ref, ikt_ref,
                          *, idx_w_scale):
    x = x_ref[...]
    hb = (_rms(x, -1) * g1_ref[...]).astype(BF16)
    y = _dot(hb, wm_ref[...])
    akb_ref[...] = _head_rms_lanes(y[:, 0:128], gk_ref[...]).astype(BF16)
    ikb_ref[...] = (_rms(y[:, 128:128 + IDX_DIM], -1) * gi_ref[...]).astype(BF16)
    rec4_ref[...] = y[:, 256:]

    yt = _dot_nt(wt_ref[...], hb)
    n_q = aqt_ref.shape[0] // ATT_HEAD_DIM
    for h in range(n_q):
        rows = slice(h * ATT_HEAD_DIM, (h + 1) * ATT_HEAD_DIM)
        aqt_ref[rows, :] = (_rms(yt[rows, :], 0) * gq_ref[rows, :]).astype(BF16)
    o = aqt_ref.shape[0]
    iqt_ref[...] = yt[o:o + iqt_ref.shape[0], :].astype(BF16)
    o += iqt_ref.shape[0]
    kvw = ATT_KV_HEADS * ATT_HEAD_DIM
    av_t = yt[o:o + kvw, :]
    avt_ref[0] = av_t
    avb_ref[...] = av_t.astype(BF16)
    o += kvw
    for g in range(ATT_KV_HEADS):
        rows = slice(g * ATT_HEAD_DIM, (g + 1) * ATT_HEAD_DIM)
        akt_ref[0, rows, :] = _rms(yt[o + g * ATT_HEAD_DIM:o + (g + 1) * ATT_HEAD_DIM, :], 0) * gkc_ref[rows, :]
    o += kvw
    ikt_ref[0] = _rms(yt[o:o + IDX_DIM, :], 0) * gic_ref[...]
    o += IDX_DIM
    iwt_ref[...] = yt[o:o + IDX_HEADS, :] * idx_w_scale


def _inproj_sample_kernel(x_ref, g1_ref, wm_ref, wq_ref, gk_ref, gi_ref,
                          akn_ref, av_ref, ikn_ref, rec4_ref, aq_ref, iq_ref, iw_ref, *, idx_w_scale):
    x = x_ref[...]
    hb = (_rms(x, -1) * g1_ref[...]).astype(BF16)
    y = _dot(hb, wm_ref[...])
    akn_ref[...] = _head_rms_lanes(y[:, 0:128], gk_ref[...])
    ikn_ref[...] = _rms(y[:, 128:128 + IDX_DIM], -1) * gi_ref[...]
    rec4_ref[...] = y[:, 256:]
    yq = _dot(hb, wq_ref[...])
    wa = aq_ref.shape[1]
    wi = iq_ref.shape[1]
    aq_ref[...] = yq[:, 0:wa]
    iq_ref[...] = yq[:, wa:wa + wi]
    av_ref[...] = yq[:, wa + wi:wa + wi + 128]
    iw_ref[...] = yq[:, wa + wi + 128:wa + wi + 256] * idx_w_scale


def _hgrn_gates(rq, rf, ri, rg, lb):
    sp, sn = _sigmoid_pair(rf)
    q = _silu(rq) * (REC_DIM ** -0.5)
    log_f = jnp.log(lb + (1.0 - lb) * sp)
    k = (1.0 - lb) * sn
    return q, log_f, k, ri, _silu(rg)


def _hgrn_prompt_kernel(lbs_ref, rec4_ref, gn_ref, rec_ref, sfin_ref, st_ref, bsc_ref, ksc_ref, *, layer, rec_width,
                        direct):
    n_heads = rec_width // REC_DIM
    n_seq, tb = rec4_ref.shape[0], rec4_ref.shape[1]

    @pl.when(pl.program_id(1) == 0)
    def _():
        st_ref[...] = jnp.zeros_like(st_ref)

    lb = _lower_bound(lbs_ref, layer)
    w = rec_width
    row = lax.broadcasted_iota(I32, (CHUNK, CHUNK), 0)
    col = lax.broadcasted_iota(I32, (CHUNK, CHUNK), 1)
    causal = row >= col
    tri = jnp.where(causal, 1.0, 0.0).astype(BF16)
    gn = gn_ref[...]
    gates = [_hgrn_gates(rec4_ref[sq, :, 0:w], rec4_ref[sq, :, w:2 * w], rec4_ref[sq, :, 2 * w:3 * w],
                         rec4_ref[sq, :, 3 * w:4 * w], lb) for sq in range(n_seq)]

    for c in range(tb // CHUNK):
        rows = slice(c * CHUNK, (c + 1) * CHUNK)
        for sq in range(n_seq):
            q, log_f, k, v, gate = gates[sq]
            gc = log_f[rows, :]
            g1 = gc.astype(BF16)
            r1 = gc - g1.astype(F32)
            g2 = r1.astype(BF16)
            g3 = (r1 - g2.astype(F32)).astype(BF16)
            b = _dot(tri, g1) + _dot(tri, g2) + _dot(tri, g3)
            b_last = b[CHUNK - 1:CHUNK, :]
            b_mid = b[CHUNK // 2 - 1:CHUNK // 2, :]
            qc, kc, vc = q[rows, :], k[rows, :], v[rows, :]
            heads = [slice(h * REC_DIM, (h + 1) * REC_DIM) for h in range(n_heads)]

            if direct:
                bsc_ref[...] = b
                ksc_ref[...] = kc

                def col_step(s, acc, b=b, qc=qc):
                    b_s = bsc_ref[pl.ds(s, 1), :]
                    wgt = qc * ksc_ref[pl.ds(s, 1), :] * jnp.exp(jnp.minimum(b - b_s, 0.0))
                    hit = causal & (col == s)
                    return tuple(jnp.where(hit, jnp.sum(wgt[:, hs], axis=1, keepdims=True), a_h)
                                 for hs, a_h in zip(heads, acc))

                a_heads = lax.fori_loop(0, CHUNK, col_step, (jnp.zeros((CHUNK, CHUNK), F32),) * n_heads)
            else:
                q_in = (qc * jnp.exp(b - b_mid)).astype(BF16)
                k_in = (kc * jnp.exp(b_mid - b)).astype(BF16)
                a_heads = tuple(jnp.where(causal, _dot_nt(q_in[:, hs], k_in[:, hs]), 0.0) for hs in heads)
            q_st = (qc * jnp.exp(b)).astype(BF16)
            k_st = (kc * jnp.exp(b_last - b)).astype(BF16)
            vb = vc.astype(BF16)
            decay = jnp.exp(b_last)
            for h in range(n_heads):
                cols = heads[h]
                st = st_ref[sq, h]
                o = _dot(a_heads[h].astype(BF16), vb[:, cols]) + _dot_nt(q_st[:, cols], st.astype(BF16))
                st_ref[sq, h] = st * decay[:, cols] + _dot_tn(vb[:, cols], k_st[:, cols])
                o = _rms(o, -1) * gn * gate[rows, cols]
                rec_ref[sq, rows, cols] = o.astype(rec_ref.dtype)

    @pl.when(pl.program_id(1) == pl.num_programs(1) - 1)
    def _():
        for sq in range(n_seq):
            for h in range(n_heads):
                sfin_ref[sq, h] = st_ref[sq, h].T


def _row_to_col(x_row):
    n = x_row.shape[1]
    eye = lax.broadcasted_iota(I32, (n, n), 0) == lax.broadcasted_iota(I32, (n, n), 1)
    return jnp.sum(jnp.where(eye, jnp.broadcast_to(x_row, (n, n)), 0.0), axis=1, keepdims=True)


def _col_to_row(x_col):
    n = x_col.shape[0]
    eye = lax.broadcasted_iota(I32, (n, n), 0) == lax.broadcasted_iota(I32, (n, n), 1)
    return jnp.sum(jnp.where(eye, jnp.broadcast_to(x_col, (n, n)), 0.0), axis=0, keepdims=True)


def _hgrn_step_kernel(lbs_ref, rec4_ref, gn_ref, s0_ref, rec_ref, s1_ref, *, layer, rec_width):
    n_heads = rec_width // REC_DIM
    lb = _lower_bound(lbs_ref, layer)
    w = rec_width
    gn = gn_ref[...]
    for sq in range(rec4_ref.shape[0]):
        r4 = rec4_ref[sq]
        q, log_f, k, v, gate = _hgrn_gates(r4[:, 0:w], r4[:, w:2 * w], r4[:, 2 * w:3 * w], r4[:, 3 * w:4 * w], lb)
        f = jnp.exp(log_f)
        for h in range(n_heads):
            cols = slice(h * REC_DIM, (h + 1) * REC_DIM)
            s1 = _row_to_col(f[:, cols]) * s0_ref[sq, h] + _row_to_col(k[:, cols]) * v[:, cols]
            s1_ref[sq, h] = s1
            o = jnp.sum(_row_to_col(q[:, cols]) * s1, axis=0, keepdims=True)
            rec_ref[sq, :, cols] = (_rms(o, -1) * gn * gate[:, cols]).astype(rec_ref.dtype)


def _prompt_attention_kernel(iqa_ref, iqb_ref, aqa_ref, aqb_ref, iwa_ref, iwb_ref, ikb_ref, akb_ref, avb_ref,
                             oa_ref, ob_ref, iqs_ref, qpad_ref, iws_ref, keys_ref, planes_ref, bias_ref, logit_ref, pv_ref,
                             *, n_sel, n_units):
    p = pl.program_id(1)
    nq = 2 * pl.num_programs(1)
    j_of = (p, nq - 1 - p)
    n_a = p // 2 + 1
    n_heads = aqa_ref.shape[0] // ATT_HEAD_DIM
    group = n_heads // ATT_KV_HEADS
    half = KEY_TILE // 2
    cw = 2 * Q_TILE
    gw = group * Q_TILE
    ct_per_group = gw // cw

    for s, (iq_ref, aq_ref, iw_ref) in enumerate(((iqa_ref, aqa_ref, iwa_ref), (iqb_ref, aqb_ref, iwb_ref))):
        iqs_ref[s] = jnp.concatenate([iq_ref[h * IDX_DIM:(h + 1) * IDX_DIM, :] for h in range(IDX_HEADS)], axis=1)
        halves = [jnp.concatenate([aq_ref[(g * group + hh) * ATT_HEAD_DIM:(g * group + hh + 1) * ATT_HEAD_DIM, :]
                                   for hh in range(group)], axis=1) for g in range(ATT_KV_HEADS)]
        zero = jnp.zeros_like(halves[0])
        for g in range(ATT_KV_HEADS):
            qpad_ref[s, g] = jnp.concatenate([halves[g] if gg == g else zero for gg in range(ATT_KV_HEADS)], axis=0)
        iws_ref[s] = iw_ref[...]

    def unit(u):
        is_b = u >= n_a
        return is_b, jnp.where(is_b, 1, 0), jnp.where(is_b, u - n_a, u), jnp.where(is_b, j_of[1], j_of[0])

    def key_rows(kt, mc):
        return pl.ds(pl.multiple_of(kt * KEY_TILE + mc * half, half), half)

    row_h = lax.broadcasted_iota(I32, (half, Q_TILE), 0)
    lane_h = lax.broadcasted_iota(I32, (half, Q_TILE), 1)

    for u in range(n_units):
        _, s, kt, j = unit(u)
        iw = iws_ref[s]
        for mc in range(2):
            ik = ikb_ref[key_rows(kt, mc), :]
            sc = jnp.zeros((half, Q_TILE), F32)
            for ct in range(IDX_HEADS // 2):
                d = _dot(ik, iqs_ref[s, :, ct * cw:(ct + 1) * cw])
                for hh in range(2):
                    h = 2 * ct + hh
                    sc = sc + jnp.maximum(d[:, hh * Q_TILE:(hh + 1) * Q_TILE], 0.0) * iw[h:h + 1, :]
            causal = kt * KEY_TILE + mc * half + row_h <= j * Q_TILE + lane_h
            keys_ref[u, mc * half:(mc + 1) * half, :] = jnp.where(causal, sc, NEG_INF)

    n_pieces = n_units * 2 * ATT_KV_HEADS

    neg = jnp.full((8, gw), MASKED, F32)
    raw_max = [[neg] * ATT_KV_HEADS, [neg] * ATT_KV_HEADS]

    def qk_piece(piece):
        u, mc, g = piece // (2 * ATT_KV_HEADS), (piece // ATT_KV_HEADS) % 2, piece % ATT_KV_HEADS
        is_b, s, kt, _ = unit(u)
        logit = _dot(akb_ref[key_rows(kt, mc), :], qpad_ref[s, g])
        logit_ref[u, mc, g] = logit
        part = jnp.max(logit.reshape(half // 8, 8, gw), axis=0)
        raw_max[0][g] = jnp.maximum(raw_max[0][g], jnp.where(is_b, neg, part))
        raw_max[1][g] = jnp.maximum(raw_max[1][g], jnp.where(is_b, part, neg))

    srl = lax.shift_right_logical
    for u in range(n_units):
        sc = keys_ref[u]
        bits = lax.bitcast_convert_type(sc, I32)
        okey = bits ^ ((bits >> 31) | INT_MIN)
        w = [okey[8 * v:8 * v + 8, :] for v in range(KEY_TILE // 8)]
        j, m = 16, 0x0000FFFF
        while j:
            for k in range(32):
                if not k & j:
                    t = (w[k] ^ srl(w[k + j], jnp.int32(j))) & jnp.int32(m if m < 2 ** 31 else m - 2 ** 32)
                    w[k] = w[k] ^ t
                    w[k + j] = w[k + j] ^ (t << j)
            j >>= 1
            m = (m ^ (m << j)) & 0xFFFFFFFF
        for i in range(32):
            planes_ref[u, i] = w[i]
        for piece in range(u * 2 * ATT_KV_HEADS, (u + 1) * 2 * ATT_KV_HEADS):
            qk_piece(piece)

    zero8 = jnp.zeros((8, Q_TILE), I32)
    zero_q = jnp.zeros((1, Q_TILE), I32)
    max_a = (n_units - 2) // 2 + 1

    def count2(pred):
        tot = [zero8, zero8]
        for u in range(n_units):
            is_b, _, kt, _ = unit(u)
            m = pred(keys_ref[u], lambda a, b_: jnp.where(is_b, b_, a), kt)
            part = jnp.sum(jnp.where(m, 1, 0).reshape(KEY_TILE // 8, 8, Q_TILE), axis=0)
            tot[0] = tot[0] + jnp.where(is_b, zero8, part)
            tot[1] = tot[1] + jnp.where(is_b, part, zero8)
        return tuple(jnp.sum(t, axis=0, keepdims=True) for t in tot)

    def sliced_step(i, carry):
        alive, need, chosen = carry
        ones = [alive[u] & planes_ref[u, i] for u in range(n_units)]
        pcs = [lax.population_count(o) for o in ones]
        total, part_a = pcs[0], pcs[0]
        for u in range(1, n_units):
            total = total + pcs[u]
            if u < max_a:
                part_a = part_a + jnp.where(u < n_a, pcs[u], 0)
        c1 = (jnp.sum(part_a, axis=0, keepdims=True), jnp.sum(total - part_a, axis=0, keepdims=True))
        take = tuple(jnp.where(c1[t] >= need[t], 1, 0) for t in range(2))
        bit = jnp.left_shift(jnp.int32(1), 31 - i)
        new_alive = []
        for u in range(n_units):
            is_b, _, _, _ = unit(u)
            new_alive.append(jnp.where(jnp.where(is_b, take[1], take[0]) != 0, ones[u], alive[u] ^ ones[u]))
        return (tuple(new_alive), tuple(need[t] - (1 - take[t]) * c1[t] for t in range(2)),
                tuple(chosen[t] | (take[t] * bit) for t in range(2)))

    full = jnp.full((8, Q_TILE), -1, I32)
    want = jnp.full((1, Q_TILE), n_sel, I32)
    _, _, guess = lax.fori_loop(0, 32, sliced_step, ((full,) * n_units, (want, want), (zero_q, zero_q)))

    g_thr = tuple(_radix_candidate(gs) for gs in guess)
    g_found = tuple(gt > NEG_INF for gt in g_thr)
    g_at = count2(lambda sc, pick, kt: sc >= pick(g_thr[0], g_thr[1]))
    g_above = count2(lambda sc, pick, kt: sc > pick(g_thr[0], g_thr[1]))
    bad = [jnp.where(g_found[t] & ((g_at[t] < n_sel) | (g_above[t] >= n_sel)), 1, 0) for t in range(2)]
    confirmed = jnp.max(jnp.maximum(bad[0], bad[1])) == 0

    def bit_step(i, carry):
        pref, at_thr = carry
        cand = tuple(pr | jnp.left_shift(jnp.int32(1), 31 - i) for pr in pref)
        cf = tuple(_radix_candidate(c) for c in cand)
        cnt = count2(lambda sc, pick, kt: sc >= pick(cf[0], cf[1]))
        take = tuple((cnt[t] >= n_sel) & (cf[t] > NEG_INF) for t in range(2))
        return (tuple(jnp.where(take[t], cand[t], pref[t]) for t in range(2)),
                tuple(jnp.where(take[t], cnt[t], at_thr[t]) for t in range(2)))

    def float_search():
        pref, at_thr = lax.fori_loop(0, 32, bit_step, ((zero_q, zero_q), (zero_q, zero_q)))
        thr = tuple(jnp.where(pref[t] != 0, _radix_candidate(pref[t]), NEG_INF) for t in range(2))
        return pref, at_thr, count2(lambda sc, pick, kt: sc > pick(thr[0], thr[1]))

    at_fast = tuple(jnp.where(g_found[t], g_at[t], 0) for t in range(2))
    g_pref = tuple(jnp.where(g_found[t], guess[t], 0) for t in range(2))
    pref, at_thr, above = lax.cond(confirmed, lambda: (g_pref, at_fast, g_above), float_search)
    found = tuple(pr != 0 for pr in pref)
    thr = tuple(jnp.where(found[t], _radix_candidate(pref[t]), NEG_INF) for t in range(2))

    need = tuple(n_sel - a for a in above)
    any_tie = jnp.max(jnp.maximum(jnp.where(at_thr[0] > n_sel, 1, 0), jnp.where(at_thr[1] > n_sel, 1, 0))) > 0
    pos_bits = max(1, (ikb_ref.shape[0] - 1).bit_length())
    row_k = lax.broadcasted_iota(I32, (KEY_TILE, Q_TILE), 0)

    def last_kept_tie():
        def step(i, pre):
            cand = tuple(pr | jnp.left_shift(jnp.int32(1), pos_bits - 1 - i) for pr in pre)
            cnt = count2(lambda sc, pick, kt: (sc == pick(thr[0], thr[1])) & (kt * KEY_TILE + row_k < pick(cand[0], cand[1])))
            return tuple(jnp.where(cnt[t] < need[t], cand[t], pre[t]) for t in range(2))
        return lax.fori_loop(0, pos_bits, step, (zero_q, zero_q))

    big = jnp.full((1, Q_TILE), INT_MAX, I32)
    last_tie = lax.cond(any_tie, last_kept_tie, lambda: (big, big))
    last_tie = tuple(jnp.where(found[t], last_tie[t], -1) for t in range(2))

    for u in range(n_units):
        is_b, _, kt, _ = unit(u)
        t_u = jnp.where(is_b, thr[1], thr[0])
        last_u = jnp.where(is_b, last_tie[1], last_tie[0])
        sc = keys_ref[u]
        keep = (sc > t_u) | ((sc == t_u) & (kt * KEY_TILE + row_k <= last_u))
        bias_ref[u] = jnp.where(keep, 0.0, MASKED)

    def kept_max():
        cmax = [[neg] * ATT_KV_HEADS, [neg] * ATT_KV_HEADS]
        for u in range(n_units):
            is_b, _, _, _ = unit(u)
            for mc in range(2):
                b_rows = bias_ref[u, mc * half:(mc + 1) * half, :]
                bias_g = jnp.concatenate([b_rows] * group, axis=1)
                for g in range(ATT_KV_HEADS):
                    part = jnp.max((logit_ref[u, mc, g] + bias_g).reshape(half // 8, 8, gw), axis=0)
                    cmax[0][g] = jnp.maximum(cmax[0][g], jnp.where(is_b, neg, part))
                    cmax[1][g] = jnp.maximum(cmax[1][g], jnp.where(is_b, part, neg))
        return cmax

    def attend(cmax):
        shift = [[jnp.max(c, axis=0, keepdims=True) for c in cm] for cm in cmax]
        zsum = jnp.zeros((8, cw), F32)
        lsum = [[[zsum] * ct_per_group for _ in range(ATT_KV_HEADS)] for _ in range(2)]
        for u in range(n_units):
            is_b, _, kt, _ = unit(u)
            for g in range(ATT_KV_HEADS):
                m_g = jnp.where(is_b, shift[1][g], shift[0][g])
                for ct in range(ct_per_group):
                    cols = slice(ct * cw, (ct + 1) * cw)
                    contrib = jnp.zeros((ATT_HEAD_DIM, cw), F32)
                    for mc in range(2):
                        b_rows = bias_ref[u, mc * half:(mc + 1) * half, :]
                        bias_c = jnp.concatenate([b_rows] * (cw // Q_TILE), axis=1)
                        pr = jnp.exp(logit_ref[u, mc, g, :, cols] + bias_c - m_g[:, cols])
                        part = jnp.sum(pr.reshape(half // 8, 8, cw), axis=0)
                        lsum[0][g][ct] = lsum[0][g][ct] + jnp.where(is_b, zsum, part)
                        lsum[1][g][ct] = lsum[1][g][ct] + jnp.where(is_b, part, zsum)
                        v_t = avb_ref[kt, g * ATT_HEAD_DIM:(g + 1) * ATT_HEAD_DIM, mc * half:(mc + 1) * half]
                        contrib = contrib + _dot(v_t, pr.astype(BF16))
                    pv_ref[u, g, :, cols] = contrib

        den_min = jnp.full((1, cw), jnp.inf, F32)
        for g in range(ATT_KV_HEADS):
            for ct in range(ct_per_group):
                cols = slice(ct * cw, (ct + 1) * cw)
                total = pv_ref[0, g, :, cols]
                for u in range(1, n_units):
                    total = total + pv_ref[u, g, :, cols]
                part_a = pv_ref[0, g, :, cols]
                for u in range(1, max_a):
                    part_a = part_a + jnp.where(u < n_a, pv_ref[u, g, :, cols], 0.0)
                for s, (o_ref, acc) in enumerate(((oa_ref, part_a), (ob_ref, total - part_a))):
                    den = jnp.sum(lsum[s][g][ct], axis=0, keepdims=True)
                    den_min = jnp.minimum(den_min, den)
                    out_t = acc / den
                    blk = jnp.concatenate([out_t[:, 0:Q_TILE], out_t[:, Q_TILE:cw]], axis=0)
                    c_out = (g * ct_per_group + ct) * 2 * ATT_HEAD_DIM
                    o_ref[0, 0, :, c_out:c_out + 2 * ATT_HEAD_DIM] = blk.T.astype(o_ref.dtype)
        return jnp.min(den_min)

    den_min = attend(raw_max)

    @pl.when(jnp.logical_not(den_min >= DEN_FLOOR))
    def _():
        attend(kept_max())


def _page_copy(pt_ref, src_hbm, layer, buf, sem, seq, slot, page):
    def copy(p):
        dst = buf.at[(slot,) + (slice(None),) * (len(buf.shape) - 2) + (pl.ds(pl.multiple_of(p * page, page), page),)]
        return pltpu.make_async_copy(src_hbm.at[layer, pt_ref[seq, p]], dst, sem.at[slot])
    return copy


def _start_pages(copy, n_pages):
    def body(p, c):
        copy(p).start()
        return c
    lax.fori_loop(0, n_pages, body, 0, unroll=8 if n_pages % 8 == 0 else 1)


def _wait_slot(buf, sem, slot):
    pltpu.make_async_copy(buf.at[1 - slot], buf.at[slot], sem.at[slot]).wait()


def _sample_scores_kernel(pt_ref, iq_ref, iw_ref, ikn_ref, kidx_hbm, sc_ref, buf, sem, *, layer, n_pages, page, tile):
    b = pl.program_id(0)
    nb = pl.num_programs(0)
    slot = b % 2

    @pl.when(b == 0)
    def _():
        _start_pages(_page_copy(pt_ref, kidx_hbm, layer, buf, sem, 0, 0, page), n_pages)

    @pl.when(b + 1 < nb)
    def _():
        _start_pages(_page_copy(pt_ref, kidx_hbm, layer, buf, sem, b + 1, 1 - slot, page), n_pages)

    _wait_slot(buf, sem, slot)

    iq = iq_ref[0]
    iqb = iq.astype(BF16)
    iw = iw_ref[0]
    past = n_pages * page
    for t in range(past // tile):
        kb = buf[slot, :, t * tile:(t + 1) * tile].astype(BF16)
        d = _dot(iqb, kb)
        sc_ref[0, :, t * tile:(t + 1) * tile] = jnp.sum(jnp.maximum(d, 0.0) * iw, axis=0, keepdims=True)
    d_new = jnp.sum(iq * ikn_ref[0], axis=1, keepdims=True)
    s_new = jnp.sum(jnp.maximum(d_new, 0.0) * iw, axis=0, keepdims=True)
    lane = lax.broadcasted_iota(I32, (1, V7X_LANES), 1)
    sc_ref[0, :, past:past + V7X_LANES] = jnp.where(lane == 0, s_new, 0.0)


def _sample_select_kernel(sc_ref, bias_ref, keys_ref, *, n_sel, past):
    rows, width = sc_ref.shape
    col = lax.broadcasted_iota(I32, (rows, width), 1)
    keys_ref[...] = jnp.where(col <= past, sc_ref[...], NEG_INF)

    def count(pred):
        return jnp.sum(pred(keys_ref[...]).astype(I32), axis=1, keepdims=True)

    def bit_step(i, carry):
        prefix, at_thr = carry
        cand = prefix | jnp.left_shift(jnp.int32(1), 31 - i)
        cf = _radix_candidate(cand)
        cnt = count(lambda sc: sc >= cf)
        take = (cnt >= n_sel) & (cf > NEG_INF)
        return jnp.where(take, cand, prefix), jnp.where(take, cnt, at_thr)

    zero = jnp.zeros((rows, 1), I32)
    prefix, _ = lax.fori_loop(0, 32, bit_step, (zero, zero))
    found = prefix != 0
    thr = jnp.where(found, _radix_candidate(prefix), NEG_INF)
    need = n_sel - count(lambda sc: sc > thr)
    pos_bits = max(1, (width - 1).bit_length())

    def pos_step(i, pre):
        cand = pre | jnp.left_shift(jnp.int32(1), pos_bits - 1 - i)
        cnt = count(lambda sc: (sc == thr) & (col < cand))
        return jnp.where(cnt < need, cand, pre)

    last_tie = jnp.where(found, lax.fori_loop(0, pos_bits, pos_step, zero), -1)
    sc = keys_ref[...]
    keep = (sc > thr) | ((sc == thr) & (col <= last_tie))
    bias_ref[...] = jnp.where(keep, 0.0, MASKED)


def _sample_attention_kernel(pt_ref, aq_ref, gq_ref, akn_ref, av_ref, bias_ref, k_hbm, v_hbm, o_ref,
                             kbuf, vbuf, logit_ref, ksem, vsem, *, layer, n_pages, page, tile):
    b = pl.program_id(0)
    nb = pl.num_programs(0)
    slot = b % 2

    def start(seq, s):
        _start_pages(_page_copy(pt_ref, k_hbm, layer, kbuf, ksem, seq, s, page), n_pages)
        _start_pages(_page_copy(pt_ref, v_hbm, layer, vbuf, vsem, seq, s, page), n_pages)

    @pl.when(b == 0)
    def _():
        start(0, 0)

    @pl.when(b + 1 < nb)
    def _():
        start(b + 1, 1 - slot)

    _wait_slot(kbuf, ksem, slot)
    _wait_slot(vbuf, vsem, slot)

    n_heads = aq_ref.shape[1]
    group = n_heads // ATT_KV_HEADS
    kvw = ATT_KV_HEADS * ATT_HEAD_DIM
    past = n_pages * page
    q = _rms(aq_ref[0], -1) * gq_ref[...]
    zeros = jnp.zeros_like(q)
    head = lax.broadcasted_iota(I32, (n_heads, kvw), 0)
    q_pad = jnp.where(head < group, jnp.concatenate([q, zeros], axis=1), jnp.concatenate([zeros, q], axis=1))
    q_pad_b = q_pad.astype(BF16)

    for t in range(past // tile):
        cols = slice(t * tile, (t + 1) * tile)
        kb = kbuf[slot, :, :, cols].reshape(kvw, tile).astype(BF16)
        logit_ref[:, cols] = _dot(q_pad_b, kb) + bias_ref[0, :, cols]
    lane = lax.broadcasted_iota(I32, (n_heads, V7X_LANES), 1)
    l_new = jnp.sum(q_pad * akn_ref[0], axis=1, keepdims=True) + bias_ref[0, :, past:past + 1]
    logit_ref[:, past:past + V7X_LANES] = jnp.where(lane == 0, l_new, MASKED)

    m = jnp.max(logit_ref[...], axis=1, keepdims=True)
    lsum = jnp.zeros((n_heads, 1), F32)
    acc = jnp.zeros((kvw, n_heads), F32)
    for t in range(past // tile):
        cols = slice(t * tile, (t + 1) * tile)
        p = jnp.exp(logit_ref[:, cols] - m)
        lsum = lsum + jnp.sum(p, axis=1, keepdims=True)
        acc = acc + _dot_nt(vbuf[slot, :, :, cols].reshape(kvw, tile).astype(BF16), p.astype(BF16))
    p_new = jnp.exp(logit_ref[:, past:past + 1] - m)
    acc = acc + _row_to_col(av_ref[0]) * _col_to_row(p_new)
    o_ref[0] = acc / _col_to_row(lsum + p_new)


def _outproj_mlp_kernel(x_ref, atta_ref, attb_ref, rec_ref, wo_ref, g2_ref, wup_ref, wdn_ref, o_ref, *, ff_tile, tiles_per_seq):
    n_t = atta_ref.shape[1]
    att = jnp.concatenate([atta_ref[0, k] for k in range(n_t)], axis=0)
    if tiles_per_seq:
        late = jnp.concatenate([attb_ref[0, n_t - 1 - k] for k in range(n_t)], axis=0)
        att = jnp.where(pl.program_id(0) % tiles_per_seq < tiles_per_seq // 2, att, late)
    mix = jnp.concatenate([att.astype(BF16), rec_ref[...].astype(BF16)], axis=1)
    y = x_ref[...] + _dot(mix, wo_ref[...])
    h2 = (_rms(y, -1) * g2_ref[...]).astype(BF16)
    acc = y
    for c in range(wup_ref.shape[1] // ff_tile):
        u = jnp.maximum(_dot(h2, wup_ref[:, c * ff_tile:(c + 1) * ff_tile]), 0.0)
        acc = acc + _dot((u * u).astype(BF16), wdn_ref[c * ff_tile:(c + 1) * ff_tile, :])
    o_ref[...] = acc


def _row_tile(n_rows, want):
    return want if n_rows % want == 0 else n_rows


def _layer_weights(w_in, att_width, kv_width, idx_width, rec_width):
    o = [0]
    for wdt in (att_width, kv_width, kv_width, idx_width, IDX_DIM, IDX_HEADS, rec_width, rec_width, rec_width, rec_width):
        o.append(o[-1] + wdt)
    aq, ak, av, iq, ik, iw, rq, rf, ri, rg = (w_in[:, o[i]:o[i + 1]] for i in range(10))
    d = w_in.shape[0]
    z = lambda n: jnp.zeros((d, n), w_in.dtype)
    w_main = jnp.concatenate([ak, ik, z(128 - IDX_DIM), rq, rf, ri, rg], axis=1).astype(BF16)
    w_tok = jnp.concatenate([aq, iq, av, ak, ik, iw, z(16 - IDX_HEADS)], axis=1).T.astype(BF16)
    w_qry = jnp.concatenate([aq, iq, av, iw, z(128 - IDX_HEADS)], axis=1).astype(BF16)
    return w_main, w_tok, w_qry


def _outproj_mlp(x, att_a, att_b, rec, w_out, g2, w_up, w_dn, name):
    n, d = x.shape
    tm = _row_tile(n, 512)
    ff = w_up.shape[1]
    width = att_a.shape[-1]
    if att_b is None:
        assert att_a.shape[:3] == (1, 1, n)
        tps, att_b = 0, att_a
        spec_a = spec_b = pl.BlockSpec((1, 1, tm, width), lambda i: (0, 0, i, 0))
    else:
        rows = att_a.shape[2]
        n_t = tm // rows
        tps = att_a.shape[1] * 2 * rows // tm
        assert tm % rows == 0 and tps % 2 == 0 and tps * tm == 2 * att_a.shape[1] * rows
        half = tps // 2
        spec_a = pl.BlockSpec((1, n_t, rows, width), lambda i: (i // tps, jnp.minimum(i % tps, half - 1), 0, 0))
        spec_b = pl.BlockSpec((1, n_t, rows, width), lambda i: (i // tps, jnp.clip(tps - 1 - i % tps, 0, half - 1), 0, 0))
    return pl.pallas_call(
        functools.partial(_outproj_mlp_kernel, ff_tile=min(ff, 1024), tiles_per_seq=tps),
        grid=(n // tm,),
        in_specs=[pl.BlockSpec((tm, d), lambda i: (i, 0)), spec_a, spec_b,
                  pl.BlockSpec((tm, rec.shape[1]), lambda i: (i, 0)),
                  _resident(w_out.shape), _resident(g2.shape), _resident(w_up.shape), _resident(w_dn.shape)],
        out_specs=pl.BlockSpec((tm, d), lambda i: (i, 0)),
        out_shape=jax.ShapeDtypeStruct((n, d), F32),
        compiler_params=_params("arbitrary"),
        name=name,
    )(x, att_a, att_b, rec, w_out, g2, w_up, w_dn)


def _prompt_attention(iqt, aqt, iwt, ikb, akb, avb, bsz, seq):
    n = bsz * seq
    kvw = ATT_KV_HEADS * ATT_HEAD_DIM
    att_width, idx_width = aqt.shape[0], iqt.shape[0]
    n_sel = min(TOPK_MAX, seq // 4)
    nq = seq // Q_TILE
    assert KEY_TILE == 2 * Q_TILE and nq % 2 == 0, "query tiles are paired so every step sees nq/2 + 1 key tiles"
    n_pairs = nq // 2
    n_units = n_pairs + 1
    kt_per_seq = seq // KEY_TILE
    avb_tiles = avb.reshape(kvw, n // KEY_TILE, KEY_TILE).transpose(1, 0, 2)
    n_heads = att_width // ATT_HEAD_DIM
    gw2 = n_heads * Q_TILE
    qa = lambda h: pl.BlockSpec((h, Q_TILE), lambda b, p: (0, b * nq + p))
    qb = lambda h: pl.BlockSpec((h, Q_TILE), lambda b, p: (0, b * nq + nq - 1 - p))
    outb = pl.BlockSpec((1, 1, Q_TILE, att_width), lambda b, p: (b, p, 0, 0))
    return pl.pallas_call(
        functools.partial(_prompt_attention_kernel, n_sel=n_sel, n_units=n_units),
        grid=(bsz, n_pairs),
        in_specs=[qa(idx_width), qb(idx_width), qa(att_width), qb(att_width), qa(IDX_HEADS), qb(IDX_HEADS),
                  pl.BlockSpec((seq, IDX_DIM), lambda b, p: (b, 0)),
                  pl.BlockSpec((seq, kvw), lambda b, p: (b, 0)),
                  pl.BlockSpec((kt_per_seq, kvw, KEY_TILE), lambda b, p: (b, 0, 0))],
        out_specs=[outb, outb],
        out_shape=[jax.ShapeDtypeStruct((bsz, n_pairs, Q_TILE, att_width), BF16)] * 2,
        scratch_shapes=[pltpu.VMEM((2, IDX_DIM, IDX_HEADS * Q_TILE), BF16),
                        pltpu.VMEM((2, ATT_KV_HEADS, kvw, gw2 // ATT_KV_HEADS), BF16),
                        pltpu.VMEM((2, IDX_HEADS, Q_TILE), F32),
                        pltpu.VMEM((n_units, KEY_TILE, Q_TILE), F32),
                        pltpu.VMEM((n_units, 32, 8, Q_TILE), I32),
                        pltpu.VMEM((n_units, KEY_TILE, Q_TILE), F32),
                        pltpu.VMEM((n_units, 2, ATT_KV_HEADS, KEY_TILE // 2, gw2 // ATT_KV_HEADS), F32),
                        pltpu.VMEM((n_units, ATT_KV_HEADS, ATT_HEAD_DIM, gw2 // ATT_KV_HEADS), F32)],
        compiler_params=_params("arbitrary", "arbitrary"),
        name="attention_prompt",
    )(iqt, iqt, aqt, aqt, iwt, iwt, ikb, akb, avb_tiles)


def _prompt_layer(x, lbs, layer, w_main, w_tok, g1, gq_col, gk_row, gi_row, gk_col, gi_col, gn_row, w_out, g2,
                  w_up, w_dn, att_width, idx_width, rec_width, idx_w_scale):
    bsz, seq, d = x.shape
    n = bsz * seq
    xf = x.reshape(n, d)
    tm = _row_tile(seq, 512)
    tps = seq // tm
    kvw = ATT_KV_HEADS * ATT_HEAD_DIM
    row = lambda w: pl.BlockSpec((tm, w), lambda i: (i, 0))
    colb = lambda h: pl.BlockSpec((h, tm), lambda i: (0, i))
    seqb = lambda h: pl.BlockSpec((1, h, tm), lambda i: (i // tps, 0, i % tps))
    akb, ikb, rec4, aqt, iqt, avb, iwt, akt, avt, ikt = pl.pallas_call(
        functools.partial(_inproj_prompt_kernel, idx_w_scale=idx_w_scale),
        grid=(n // tm,),
        in_specs=[row(d), _resident(g1.shape), _resident(w_main.shape), _resident(w_tok.shape),
                  _resident(gk_row.shape), _resident(gi_row.shape), _resident(gq_col.shape),
                  _resident(gk_col.shape), _resident(gi_col.shape)],
        out_specs=[row(kvw), row(IDX_DIM), row(4 * rec_width),
                   colb(att_width), colb(idx_width), colb(kvw), colb(IDX_HEADS),
                   seqb(kvw), seqb(kvw), seqb(IDX_DIM)],
        out_shape=[jax.ShapeDtypeStruct((n, kvw), BF16), jax.ShapeDtypeStruct((n, IDX_DIM), BF16),
                   jax.ShapeDtypeStruct((n, 4 * rec_width), F32),
                   jax.ShapeDtypeStruct((att_width, n), BF16), jax.ShapeDtypeStruct((idx_width, n), BF16),
                   jax.ShapeDtypeStruct((kvw, n), BF16), jax.ShapeDtypeStruct((IDX_HEADS, n), F32),
                   jax.ShapeDtypeStruct((bsz, kvw, seq), F32), jax.ShapeDtypeStruct((bsz, kvw, seq), F32),
                   jax.ShapeDtypeStruct((bsz, IDX_DIM, seq), F32)],
        compiler_params=_params("arbitrary"),
        name="inproj_prompt",
    )(xf, g1, w_main, w_tok, gk_row, gi_row, gq_col, gk_col, gi_col)

    tb = _row_tile(seq, 256)
    n_rec = rec_width // REC_DIM
    tiles = seq // tb
    gs = 2 if bsz % 2 == 0 else 1

    def hgrn(direct):
        return pl.pallas_call(
            functools.partial(_hgrn_prompt_kernel, layer=layer, rec_width=rec_width, direct=direct),
            grid=(bsz // gs, tiles),
            in_specs=[_resident(lbs.shape), pl.BlockSpec((gs, tb, 4 * rec_width), lambda b, t: (b, t, 0)),
                      _resident(gn_row.shape)],
            out_specs=[pl.BlockSpec((gs, tb, rec_width), lambda b, t: (b, t, 0)),
                       pl.BlockSpec((gs, n_rec, REC_DIM, REC_DIM), lambda b, t: (b, 0, 0, 0))],
            out_shape=[jax.ShapeDtypeStruct((bsz, seq, rec_width), BF16),
                       jax.ShapeDtypeStruct((bsz, n_rec, REC_DIM, REC_DIM), F32)],
            scratch_shapes=[pltpu.VMEM((gs, n_rec, REC_DIM, REC_DIM), F32),
                            pltpu.VMEM((CHUNK, rec_width), F32), pltpu.VMEM((CHUNK, rec_width), F32)],
            compiler_params=_params("arbitrary", "arbitrary"),
            name="hgrn_prompt_direct" if direct else "hgrn_prompt",
        )

    lb_min = jnp.min(jnp.cumsum(jax.nn.softmax(lbs, axis=0), axis=0)[layer])
    factorable = lb_min >= float(np.exp(-HGRN_SAFE_SPAN / (CHUNK // 2)))
    rec, s_fin = lax.cond(factorable, hgrn(False), hgrn(True), lbs, rec4.reshape(bsz, seq, 4 * rec_width), gn_row)
    rec = rec.reshape(n, rec_width)

    att_a, att_b = _prompt_attention(iqt, aqt, iwt, ikb, akb, avb, bsz, seq)
    if (seq // tm) % 2:
        att_a = jnp.concatenate([att_a, att_b[:, ::-1]], axis=1).reshape(1, 1, n, att_width)
        att_b = None

    y = _outproj_mlp(xf, att_a, att_b, rec, w_out, g2, w_up, w_dn, "outproj_mlp_prompt")
    to_heads = lambda t: t.reshape(bsz, ATT_KV_HEADS, ATT_HEAD_DIM, seq).transpose(0, 3, 1, 2)
    return y.reshape(bsz, seq, d), to_heads(akt), to_heads(avt), ikt.transpose(0, 2, 1), s_fin


def _sample_layer(x, s0, lbs, layer, cache_k, cache_v, cache_kidx, page_table, w_main, w_qry, g1, gq_row,
                  gk_row, gi_row, gn_row, w_out, g2, w_up, w_dn, att_width, idx_width, rec_width, idx_w_scale):
    bd, t_new, d = x.shape
    assert t_new == 1, "the sample path handles one new token per sequence"
    kvw = ATT_KV_HEADS * ATT_HEAD_DIM
    n_heads = att_width // ATT_HEAD_DIM
    n_rec = rec_width // REC_DIM
    page = cache_k.shape[2]
    n_pages = page_table.shape[1]
    past = n_pages * page
    width = past + V7X_LANES
    tile = 2048 if past % 2048 == 0 else past
    xf = x.reshape(bd, d)

    full = lambda shape: _resident(shape)
    akn, av, ikn, rec4, aq, iq, iw = pl.pallas_call(
        functools.partial(_inproj_sample_kernel, idx_w_scale=idx_w_scale),
        grid=(1,),
        in_specs=[full(xf.shape), full(g1.shape), full(w_main.shape), full(w_qry.shape),
                  full(gk_row.shape), full(gi_row.shape)],
        out_specs=[full((bd, kvw)), full((bd, kvw)), full((bd, IDX_DIM)), full((bd, 4 * rec_width)),
                   full((bd, att_width)), full((bd, idx_width)), full((bd, 128))],
        out_shape=[jax.ShapeDtypeStruct((bd, kvw), F32), jax.ShapeDtypeStruct((bd, kvw), F32),
                   jax.ShapeDtypeStruct((bd, IDX_DIM), F32), jax.ShapeDtypeStruct((bd, 4 * rec_width), F32),
                   jax.ShapeDtypeStruct((bd, att_width), F32), jax.ShapeDtypeStruct((bd, idx_width), F32),
                   jax.ShapeDtypeStruct((bd, 128), F32)],
        compiler_params=_params("arbitrary"),
        name="inproj_sample",
    )(xf, g1, w_main, w_qry, gk_row, gi_row)

    sb = 4 if bd % 4 == 0 else 1
    rec, s1 = pl.pallas_call(
        functools.partial(_hgrn_step_kernel, layer=layer, rec_width=rec_width),
        grid=(bd // sb,),
        in_specs=[_resident(lbs.shape), pl.BlockSpec((sb, 1, 4 * rec_width), lambda b: (b, 0, 0)),
                  _resident(gn_row.shape), pl.BlockSpec((sb, n_rec, REC_DIM, REC_DIM), lambda b: (b, 0, 0, 0))],
        out_specs=[pl.BlockSpec((sb, 1, rec_width), lambda b: (b, 0, 0)),
                   pl.BlockSpec((sb, n_rec, REC_DIM, REC_DIM), lambda b: (b, 0, 0, 0))],
        out_shape=[jax.ShapeDtypeStruct((bd, 1, rec_width), BF16),
                   jax.ShapeDtypeStruct((bd, n_rec, REC_DIM, REC_DIM), F32)],
        compiler_params=_params("arbitrary"),
        name="hgrn_step",
    )(lbs, rec4.reshape(bd, 1, 4 * rec_width), gn_row, s0)

    kidx_t = cache_kidx.transpose(0, 1, 3, 2)
    k_t = cache_k.transpose(0, 1, 3, 4, 2)
    v_t = cache_v.transpose(0, 1, 3, 4, 2)
    iq3 = iq.reshape(bd, IDX_HEADS, IDX_DIM)
    iw3 = iw[:, 0:IDX_HEADS].reshape(bd, IDX_HEADS, 1)
    ikn3 = ikn.reshape(bd, 1, IDX_DIM)
    scores = pl.pallas_call(
        functools.partial(_sample_scores_kernel, layer=layer, n_pages=n_pages, page=page, tile=tile),
        grid_spec=pltpu.PrefetchScalarGridSpec(
            num_scalar_prefetch=1,
            grid=(bd,),
            in_specs=[pl.BlockSpec((1, IDX_HEADS, IDX_DIM), lambda b, pt: (b, 0, 0)),
                      pl.BlockSpec((1, IDX_HEADS, 1), lambda b, pt: (b, 0, 0)),
                      pl.BlockSpec((1, 1, IDX_DIM), lambda b, pt: (b, 0, 0)),
                      pl.BlockSpec(memory_space=pl.ANY)],
            out_specs=pl.BlockSpec((1, 1, width), lambda b, pt: (b, 0, 0)),
            scratch_shapes=[pltpu.VMEM((2, IDX_DIM, past), F32), pltpu.SemaphoreType.DMA((2,))]),
        out_shape=jax.ShapeDtypeStruct((bd, 1, width), F32),
        compiler_params=_params("arbitrary"),
        name="sample_scores",
    )(page_table, iq3, iw3, ikn3, kidx_t)

    n_sel = min(TOPK_MAX, (past + 1) // 4)
    bias = pl.pallas_call(
        functools.partial(_sample_select_kernel, n_sel=n_sel, past=past),
        grid=(1,),
        in_specs=[full((bd, width))],
        out_specs=full((bd, width)),
        out_shape=jax.ShapeDtypeStruct((bd, width), F32),
        scratch_shapes=[pltpu.VMEM((bd, width), F32)],
        compiler_params=_params("arbitrary"),
        name="sample_select",
    )(scores.reshape(bd, width))

    att_t = pl.pallas_call(
        functools.partial(_sample_attention_kernel, layer=layer, n_pages=n_pages, page=page, tile=tile),
        grid_spec=pltpu.PrefetchScalarGridSpec(
            num_scalar_prefetch=1,
            grid=(bd,),
            in_specs=[pl.BlockSpec((1, n_heads, ATT_HEAD_DIM), lambda b, pt: (b, 0, 0)),
                      pl.BlockSpec((1, ATT_HEAD_DIM), lambda b, pt: (0, 0)),
                      pl.BlockSpec((1, 1, kvw), lambda b, pt: (b, 0, 0)),
                      pl.BlockSpec((1, 1, kvw), lambda b, pt: (b, 0, 0)),
                      pl.BlockSpec((1, 1, width), lambda b, pt: (b, 0, 0)),
                      pl.BlockSpec(memory_space=pl.ANY), pl.BlockSpec(memory_space=pl.ANY)],
            out_specs=pl.BlockSpec((1, kvw, n_heads), lambda b, pt: (b, 0, 0)),
            scratch_shapes=[pltpu.VMEM((2, ATT_KV_HEADS, ATT_HEAD_DIM, past), F32),
                            pltpu.VMEM((2, ATT_KV_HEADS, ATT_HEAD_DIM, past), F32),
                            pltpu.VMEM((n_heads, width), F32),
                            pltpu.SemaphoreType.DMA((2,)), pltpu.SemaphoreType.DMA((2,))]),
        out_shape=jax.ShapeDtypeStruct((bd, kvw, n_heads), F32),
        compiler_params=_params("arbitrary"),
        name="sample_attention",
    )(page_table, aq.reshape(bd, n_heads, ATT_HEAD_DIM), gq_row, akn.reshape(bd, 1, kvw), av.reshape(bd, 1, kvw),
      bias.reshape(bd, 1, width), k_t, v_t)
    att_g = att_t.reshape(bd, ATT_KV_HEADS, ATT_HEAD_DIM, n_heads)
    group = n_heads // ATT_KV_HEADS
    att = jnp.stack([att_g[:, h // group, :, h] for h in range(n_heads)], axis=1)

    y = _outproj_mlp(xf, att.reshape(1, 1, bd, att_width), None, rec.reshape(bd, rec_width), w_out, g2, w_up, w_dn,
                     "outproj_mlp_sample")
    return (y.reshape(bd, 1, d), akn.reshape(bd, 1, ATT_KV_HEADS, ATT_HEAD_DIM),
            av.reshape(bd, 1, ATT_KV_HEADS, ATT_HEAD_DIM), ikn.reshape(bd, 1, IDX_DIM), s1)


def kernel(x_prompt, x_sample, cache_k, cache_v, cache_kidx, state_hgrn, page_table, norm1_g, w_in, q_norm_g,
           k_norm_g, idx_k_norm_g, lower_bounds, rec_norm_g, w_out, norm2_g, w_up, w_down):
    depth = w_in.shape[0]
    d_model = x_prompt.shape[-1]
    kv_width = ATT_KV_HEADS * ATT_HEAD_DIM
    idx_width = IDX_HEADS * IDX_DIM
    rec_width = lower_bounds.shape[-1]
    att_width = w_out.shape[1] - rec_width
    assert w_in.shape[-1] == att_width + 2 * kv_width + idx_width + IDX_DIM + IDX_HEADS + 4 * rec_width
    assert rec_norm_g.shape[-1] == REC_DIM and q_norm_g.shape[-1] == ATT_HEAD_DIM
    idx_w_scale = (IDX_HEADS ** -0.5) * (IDX_DIM ** -0.5)
    att_scale = ATT_HEAD_DIM ** -0.5
    n_heads = att_width // ATT_HEAD_DIM
    lbs = lower_bounds.astype(F32)

    yp, ys = x_prompt, x_sample
    outs = [[] for _ in range(8)]
    for l in range(depth):
        w_main, w_tok, w_qry = _layer_weights(w_in[l], att_width, kv_width, idx_width, rec_width)
        g1 = norm1_g[l].reshape(1, d_model)
        g2 = norm2_g[l].reshape(1, d_model)
        gq_row = (q_norm_g[l] * att_scale).reshape(1, ATT_HEAD_DIM)
        gq_col = jnp.tile(q_norm_g[l] * att_scale, n_heads).reshape(att_width, 1)
        gk_row = jnp.tile(k_norm_g[l], ATT_KV_HEADS).reshape(1, kv_width)
        gi_row = idx_k_norm_g[l].reshape(1, IDX_DIM)
        gk_col = gk_row.reshape(kv_width, 1)
        gi_col = gi_row.reshape(IDX_DIM, 1)
        gn_row = rec_norm_g[l].reshape(1, REC_DIM)
        wo, wu, wd = w_out[l].astype(BF16), w_up[l].astype(BF16), w_down[l].astype(BF16)
        yp, kp, vp, ip, sp = _prompt_layer(yp, lbs, l, w_main, w_tok, g1, gq_col, gk_row, gi_row, gk_col, gi_col,
                                           gn_row, wo, g2, wu, wd, att_width, idx_width, rec_width, idx_w_scale)
        ys, k_s, v_s, i_s, s_s = _sample_layer(ys, state_hgrn[l], lbs, l, cache_k, cache_v, cache_kidx,
                                               page_table, w_main, w_qry, g1, gq_row, gk_row, gi_row, gn_row, wo,
                                               g2, wu, wd, att_width, idx_width, rec_width, idx_w_scale)
        for acc, val in zip(outs, (kp, vp, ip, sp, k_s, v_s, i_s, s_s)):
            acc.append(val)
    return (yp, ys) + tuple(jnp.stack(o) for o in outs)
```

```python
import functools

import jax
import jax.numpy as jnp
import numpy as np
from jax import lax
from jax.experimental import pallas as pl
from jax.experimental.pallas import tpu as pltpu

F32 = jnp.float32
BF16 = jnp.bfloat16
I32 = jnp.int32

EPS = 1e-6
MASKED = -1e30
DEN_FLOOR = 1e-17
INT_MIN = -(2 ** 31)
NEG_INF = float("-inf")
INT_MAX = 2 ** 31 - 1

ATT_HEAD_DIM = 64
ATT_KV_HEADS = 2
IDX_HEADS = 8
IDX_DIM = 64
REC_DIM = 128
TOPK_MAX = 256
CHUNK = 64
HGRN_SAFE_SPAN = 60.0

V7X_LANES = 128
V7X_VMEM_LIMIT_BYTES = 56 * 1024 * 1024
Q_TILE = 128
KEY_TILE = 256


def _dot(a, b):
    return jnp.dot(a, b, preferred_element_type=F32)


def _dot_nt(a, b):
    return lax.dot_general(a, b, (((1,), (1,)), ((), ())), preferred_element_type=F32)


def _dot_tn(a, b):
    return lax.dot_general(a, b, (((0,), (0,)), ((), ())), preferred_element_type=F32)


def _params(*semantics):
    return pltpu.CompilerParams(dimension_semantics=semantics, vmem_limit_bytes=V7X_VMEM_LIMIT_BYTES)


def _resident(shape):
    nd = len(shape)
    return pl.BlockSpec(shape, lambda *_: (0,) * nd)


def _rms(x, axis):
    return x * lax.rsqrt(jnp.mean(x * x, axis=axis, keepdims=True) + EPS)


def _sigmoid_pair(x):
    t = jnp.exp(-jnp.abs(x))
    r = 1.0 / (1.0 + t)
    tr = t * r
    pos = x >= 0
    return jnp.where(pos, r, tr), jnp.where(pos, tr, r)


def _silu(x):
    s, _ = _sigmoid_pair(x)
    return x * s


def _radix_candidate(prefix_bits):
    key = prefix_bits ^ INT_MIN
    return lax.bitcast_convert_type(key ^ ((key >> 31) & INT_MAX), F32)


def _lower_bound(lbs_ref, layer):
    lbs = lbs_ref[...]
    e = jnp.exp(lbs - jnp.max(lbs, axis=0, keepdims=True))
    p = e / jnp.sum(e, axis=0, keepdims=True)
    return jnp.sum(p[0:layer + 1, :], axis=0, keepdims=True)


def _head_rms_lanes(x, gain_row):
    sq = x * x
    lane = lax.broadcasted_iota(I32, x.shape, 1)
    lo = lane < ATT_HEAD_DIM
    s_lo = jnp.sum(jnp.where(lo, sq, 0.0), axis=-1, keepdims=True)
    s_hi = jnp.sum(jnp.where(lo, 0.0, sq), axis=-1, keepdims=True)
    inv = jnp.where(lo, lax.rsqrt(s_lo / ATT_HEAD_DIM + EPS), lax.rsqrt(s_hi / ATT_HEAD_DIM + EPS))
    return x * inv * gain_row


def _inproj_prompt_kernel(x_ref, g1_ref, wm_ref, wt_ref, gk_ref, gi_ref, gq_ref, gkc_ref, gic_ref,
                          akb_ref, ikb_ref, rec4_ref, aqt_ref, iqt_ref, avb_ref, iwt_ref, akt_ref, avt_ref, ikt_ref,
                          *, idx_w_scale):
    x = x_ref[...]
    hb = (_rms(x, -1) * g1_ref[...]).astype(BF16)
    y = _dot(hb, wm_ref[...])
    akb_ref[...] = _head_rms_lanes(y[:, 0:128], gk_ref[...]).astype(BF16)
    ikb_ref[...] = (_rms(y[:, 128:128 + IDX_DIM], -1) * gi_ref[...]).astype(BF16)
    rec4_ref[...] = y[:, 256:]

    yt = _dot_nt(wt_ref[...], hb)
    n_q = aqt_ref.shape[0] // ATT_HEAD_DIM
    for h in range(n_q):
        rows = slice(h * ATT_HEAD_DIM, (h + 1) * ATT_HEAD_DIM)
        aqt_ref[rows, :] = (_rms(yt[rows, :], 0) * gq_ref[rows, :]).astype(BF16)
    o = aqt_ref.shape[0]
    iqt_ref[...] = yt[o:o + iqt_ref.shape[0], :].astype(BF16)
    o += iqt_ref.shape[0]
    kvw = ATT_KV_HEADS * ATT_HEAD_DIM
    av_t = yt[o:o + kvw, :]
    avt_ref[0] = av_t
    avb_ref[...] = av_t.astype(BF16)
    o += kvw
    for g in range(ATT_KV_HEADS):
        rows = slice(g * ATT_HEAD_DIM, (g + 1) * ATT_HEAD_DIM)
        akt_ref[0, rows, :] = _rms(yt[o + g * ATT_HEAD_DIM:o + (g + 1) * ATT_HEAD_DIM, :], 0) * gkc_ref[rows, :]
    o += kvw
    ikt_ref[0] = _rms(yt[o:o + IDX_DIM, :], 0) * gic_ref[...]
    o += IDX_DIM
    iwt_ref[...] = yt[o:o + IDX_HEADS, :] * idx_w_scale


def _inproj_sample_kernel(x_ref, g1_ref, wm_ref, wq_ref, gk_ref, gi_ref,
                          akn_ref, av_ref, ikn_ref, rec4_ref, aq_ref, iq_ref, iw_ref, *, idx_w_scale):
    x = x_ref[...]
    hb = (_rms(x, -1) * g1_ref[...]).astype(BF16)
    y = _dot(hb, wm_ref[...])
    akn_ref[...] = _head_rms_lanes(y[:, 0:128], gk_ref[...])
    ikn_ref[...] = _rms(y[:, 128:128 + IDX_DIM], -1) * gi_ref[...]
    rec4_ref[...] = y[:, 256:]
    yq = _dot(hb, wq_ref[...])
    wa = aq_ref.shape[1]
    wi = iq_ref.shape[1]
    aq_ref[...] = yq[:, 0:wa]
    iq_ref[...] = yq[:, wa:wa + wi]
    av_ref[...] = yq[:, wa + wi:wa + wi + 128]
    iw_ref[...] = yq[:, wa + wi + 128:wa + wi + 256] * idx_w_scale


def _hgrn_gates(rq, rf, ri, rg, lb):
    sp, sn = _sigmoid_pair(rf)
    q = _silu(rq) * (REC_DIM ** -0.5)
    log_f = jnp.log(lb + (1.0 - lb) * sp)
    k = (1.0 - lb) * sn
    return q, log_f, k, ri, _silu(rg)


def _hgrn_prompt_kernel(lbs_ref, rec4_ref, gn_ref, rec_ref, sfin_ref, st_ref, bsc_ref, ksc_ref, *, layer, rec_width,
                        direct):
    n_heads = rec_width // REC_DIM
    n_seq, tb = rec4_ref.shape[0], rec4_ref.shape[1]

    @pl.when(pl.program_id(1) == 0)
    def _():
        st_ref[...] = jnp.zeros_like(st_ref)

    lb = _lower_bound(lbs_ref, layer)
    w = rec_width
    row = lax.broadcasted_iota(I32, (CHUNK, CHUNK), 0)
    col = lax.broadcasted_iota(I32, (CHUNK, CHUNK), 1)
    causal = row >= col
    tri = jnp.where(causal, 1.0, 0.0).astype(BF16)
    gn = gn_ref[...]
    gates = [_hgrn_gates(rec4_ref[sq, :, 0:w], rec4_ref[sq, :, w:2 * w], rec4_ref[sq, :, 2 * w:3 * w],
                         rec4_ref[sq, :, 3 * w:4 * w], lb) for sq in range(n_seq)]

    for c in range(tb // CHUNK):
        rows = slice(c * CHUNK, (c + 1) * CHUNK)
        for sq in range(n_seq):
            q, log_f, k, v, gate = gates[sq]
            gc = log_f[rows, :]
            g1 = gc.astype(BF16)
            r1 = gc - g1.astype(F32)
            g2 = r1.astype(BF16)
            g3 = (r1 - g2.astype(F32)).astype(BF16)
            b = _dot(tri, g1) + _dot(tri, g2) + _dot(tri, g3)
            b_last = b[CHUNK - 1:CHUNK, :]
            b_mid = b[CHUNK // 2 - 1:CHUNK // 2, :]
            qc, kc, vc = q[rows, :], k[rows, :], v[rows, :]
            heads = [slice(h * REC_DIM, (h + 1) * REC_DIM) for h in range(n_heads)]

            if direct:
                bsc_ref[...] = b
                ksc_ref[...] = kc

                def col_step(s, acc, b=b, qc=qc):
                    b_s = bsc_ref[pl.ds(s, 1), :]
                    wgt = qc * ksc_ref[pl.ds(s, 1), :] * jnp.exp(jnp.minimum(b - b_s, 0.0))
                    hit = causal & (col == s)
                    return tuple(jnp.where(hit, jnp.sum(wgt[:, hs], axis=1, keepdims=True), a_h)
                                 for hs, a_h in zip(heads, acc))

                a_heads = lax.fori_loop(0, CHUNK, col_step, (jnp.zeros((CHUNK, CHUNK), F32),) * n_heads)
            else:
                q_in = (qc * jnp.exp(b - b_mid)).astype(BF16)
                k_in = (kc * jnp.exp(b_mid - b)).astype(BF16)
                a_heads = tuple(jnp.where(causal, _dot_nt(q_in[:, hs], k_in[:, hs]), 0.0) for hs in heads)
            q_st = (qc * jnp.exp(b)).astype(BF16)
            k_st = (kc * jnp.exp(b_last - b)).astype(BF16)
            vb = vc.astype(BF16)
            decay = jnp.exp(b_last)
            for h in range(n_heads):
                cols = heads[h]
                st = st_ref[sq, h]
                o = _dot(a_heads[h].astype(BF16), vb[:, cols]) + _dot_nt(q_st[:, cols], st.astype(BF16))
                st_ref[sq, h] = st * decay[:, cols] + _dot_tn(vb[:, cols], k_st[:, cols])
                o = _rms(o, -1) * gn * gate[rows, cols]
                rec_ref[sq, rows, cols] = o.astype(rec_ref.dtype)

    @pl.when(pl.program_id(1) == pl.num_programs(1) - 1)
    def _():
        for sq in range(n_seq):
            for h in range(n_heads):
                sfin_ref[sq, h] = st_ref[sq, h].T


def _row_to_col(x_row):
    n = x_row.shape[1]
    eye = lax.broadcasted_iota(I32, (n, n), 0) == lax.broadcasted_iota(I32, (n, n), 1)
    return jnp.sum(jnp.where(eye, jnp.broadcast_to(x_row, (n, n)), 0.0), axis=1, keepdims=True)


def _col_to_row(x_col):
    n = x_col.shape[0]
    eye = lax.broadcasted_iota(I32, (n, n), 0) == lax.broadcasted_iota(I32, (n, n), 1)
    return jnp.sum(jnp.where(eye, jnp.broadcast_to(x_col, (n, n)), 0.0), axis=0, keepdims=True)


def _hgrn_step_kernel(lbs_ref, rec4_ref, gn_ref, s0_ref, rec_ref, s1_ref, *, layer, rec_width):
    n_heads = rec_width // REC_DIM
    lb = _lower_bound(lbs_ref, layer)
    w = rec_width
    gn = gn_ref[...]
    for sq in range(rec4_ref.shape[0]):
        r4 = rec4_ref[sq]
        q, log_f, k, v, gate = _hgrn_gates(r4[:, 0:w], r4[:, w:2 * w], r4[:, 2 * w:3 * w], r4[:, 3 * w:4 * w], lb)
        f = jnp.exp(log_f)
        for h in range(n_heads):
            cols = slice(h * REC_DIM, (h + 1) * REC_DIM)
            s1 = _row_to_col(f[:, cols]) * s0_ref[sq, h] + _row_to_col(k[:, cols]) * v[:, cols]
            s1_ref[sq, h] = s1
            o = jnp.sum(_row_to_col(q[:, cols]) * s1, axis=0, keepdims=True)
            rec_ref[sq, :, cols] = (_rms(o, -1) * gn * gate[:, cols]).astype(rec_ref.dtype)


def _prompt_attention_kernel(iqa_ref, iqb_ref, aqa_ref, aqb_ref, iwa_ref, iwb_ref, ikb_ref, akb_ref, avb_ref,
                             oa_ref, ob_ref, iqs_ref, qpad_ref, iws_ref, keys_ref, planes_ref, bias_ref, logit_ref, pv_ref,
                             *, n_sel, n_units):
    p = pl.program_id(1)
    nq = 2 * pl.num_programs(1)
    j_of = (p, nq - 1 - p)
    n_a = p // 2 + 1
    n_heads = aqa_ref.shape[0] // ATT_HEAD_DIM
    group = n_heads // ATT_KV_HEADS
    half = KEY_TILE // 2
    cw = 2 * Q_TILE
    gw = group * Q_TILE
    ct_per_group = gw // cw

    for s, (iq_ref, aq_ref, iw_ref) in enumerate(((iqa_ref, aqa_ref, iwa_ref), (iqb_ref, aqb_ref, iwb_ref))):
        iqs_ref[s] = jnp.concatenate([iq_ref[h * IDX_DIM:(h + 1) * IDX_DIM, :] for h in range(IDX_HEADS)], axis=1)
        halves = [jnp.concatenate([aq_ref[(g * group + hh) * ATT_HEAD_DIM:(g * group + hh + 1) * ATT_HEAD_DIM, :]
                                   for hh in range(group)], axis=1) for g in range(ATT_KV_HEADS)]
        zero = jnp.zeros_like(halves[0])
        for g in range(ATT_KV_HEADS):
            qpad_ref[s, g] = jnp.concatenate([halves[g] if gg == g else zero for gg in range(ATT_KV_HEADS)], axis=0)
        iws_ref[s] = iw_ref[...]

    def unit(u):
        is_b = u >= n_a
        return is_b, jnp.where(is_b, 1, 0), jnp.where(is_b, u - n_a, u), jnp.where(is_b, j_of[1], j_of[0])

    def key_rows(kt, mc):
        return pl.ds(pl.multiple_of(kt * KEY_TILE + mc * half, half), half)

    row_h = lax.broadcasted_iota(I32, (half, Q_TILE), 0)
    lane_h = lax.broadcasted_iota(I32, (half, Q_TILE), 1)

    n_pieces = n_units * 2 * ATT_KV_HEADS

    neg = jnp.full((8, gw), MASKED, F32)
    raw_max = [[neg] * ATT_KV_HEADS, [neg] * ATT_KV_HEADS]

    def qk_piece(piece):
        u, mc, g = piece // (2 * ATT_KV_HEADS), (piece // ATT_KV_HEADS) % 2, piece % ATT_KV_HEADS
        is_b, s, kt, _ = unit(u)
        logit = _dot(akb_ref[key_rows(kt, mc), :], qpad_ref[s, g])
        logit_ref[u, mc, g] = logit
        part = jnp.max(logit.reshape(half // 8, 8, gw), axis=0)
        raw_max[0][g] = jnp.maximum(raw_max[0][g], jnp.where(is_b, neg, part))
        raw_max[1][g] = jnp.maximum(raw_max[1][g], jnp.where(is_b, part, neg))

    srl = lax.shift_right_logical
    for u in range(n_units):
        _, s, kt, j = unit(u)
        iw = iws_ref[s]
        for mc in range(2):
            ik = ikb_ref[key_rows(kt, mc), :]
            sc = jnp.zeros((half, Q_TILE), F32)
            for ct in range(IDX_HEADS // 2):
                d = _dot(ik, iqs_ref[s, :, ct * cw:(ct + 1) * cw])
                for hh in range(2):
                    h = 2 * ct + hh
                    sc = sc + jnp.maximum(d[:, hh * Q_TILE:(hh + 1) * Q_TILE], 0.0) * iw[h:h + 1, :]
            causal = kt * KEY_TILE + mc * half + row_h <= j * Q_TILE + lane_h
            keys_ref[u, mc * half:(mc + 1) * half, :] = jnp.where(causal, sc, NEG_INF)

        bits = lax.bitcast_convert_type(keys_ref[u], I32)
        okey = bits ^ ((bits >> 31) | INT_MIN)
        w = [okey[8 * v:8 * v + 8, :] for v in range(KEY_TILE // 8)]
        span, m = 16, 0x0000FFFF
        while span:
            for k in range(32):
                if not k & span:
                    t = (w[k] ^ srl(w[k + span], jnp.int32(span))) & jnp.int32(m)
                    w[k] = w[k] ^ t
                    w[k + span] = w[k + span] ^ (t << span)
            span >>= 1
            m = (m ^ (m << span)) & 0xFFFFFFFF
        for i in range(32):
            planes_ref[u, i] = w[i]
        for piece in range(u * 2 * ATT_KV_HEADS, (u + 1) * 2 * ATT_KV_HEADS):
            qk_piece(piece)

    zero8 = jnp.zeros((8, Q_TILE), I32)
    zero_q = jnp.zeros((1, Q_TILE), I32)
    max_a = (n_units - 2) // 2 + 1

    def count2(pred):
        tot = [zero8, zero8]
        for u in range(n_units):
            is_b, _, kt, _ = unit(u)
            m = pred(keys_ref[u], lambda a, b_: jnp.where(is_b, b_, a), kt)
            part = jnp.sum(jnp.where(m, 1, 0).reshape(KEY_TILE // 8, 8, Q_TILE), axis=0)
            tot[0] = tot[0] + jnp.where(is_b, zero8, part)
            tot[1] = tot[1] + jnp.where(is_b, part, zero8)
        return tuple(jnp.sum(t, axis=0, keepdims=True) for t in tot)

    def sliced_step(i, carry):
        alive, need, chosen = carry
        ones = [alive[u] & planes_ref[u, i] for u in range(n_units)]
        pcs = [lax.population_count(o) for o in ones]
        total, part_a = pcs[0], pcs[0]
        for u in range(1, n_units):
            total = total + pcs[u]
            if u < max_a:
                part_a = part_a + jnp.where(u < n_a, pcs[u], 0)
        c1 = (jnp.sum(part_a, axis=0, keepdims=True), jnp.sum(total - part_a, axis=0, keepdims=True))
        take = tuple(jnp.where(c1[t] >= need[t], 1, 0) for t in range(2))
        bit = jnp.left_shift(jnp.int32(1), 31 - i)
        new_alive = []
        for u in range(n_units):
            is_b, _, _, _ = unit(u)
            new_alive.append(jnp.where(jnp.where(is_b, take[1], take[0]) != 0, ones[u], alive[u] ^ ones[u]))
        return (tuple(new_alive), tuple(need[t] - (1 - take[t]) * c1[t] for t in range(2)),
                tuple(chosen[t] | (take[t] * bit) for t in range(2)))

    full = jnp.full((8, Q_TILE), -1, I32)
    want = jnp.full((1, Q_TILE), n_sel, I32)
    _, _, guess = lax.fori_loop(0, 32, sliced_step, ((full,) * n_units, (want, want), (zero_q, zero_q)))

    g_thr = tuple(_radix_candidate(gs) for gs in guess)
    g_found = tuple(gt > NEG_INF for gt in g_thr)
    g_at = count2(lambda sc, pick, kt: sc >= pick(g_thr[0], g_thr[1]))
    g_above = count2(lambda sc, pick, kt: sc > pick(g_thr[0], g_thr[1]))
    bad = [jnp.where(g_found[t] & ((g_at[t] < n_sel) | (g_above[t] >= n_sel)), 1, 0) for t in range(2)]
    confirmed = jnp.max(jnp.maximum(bad[0], bad[1])) == 0

    def bit_step(i, carry):
        pref, at_thr = carry
        cand = tuple(pr | jnp.left_shift(jnp.int32(1), 31 - i) for pr in pref)
        cf = tuple(_radix_candidate(c) for c in cand)
        cnt = count2(lambda sc, pick, kt: sc >= pick(cf[0], cf[1]))
        take = tuple((cnt[t] >= n_sel) & (cf[t] > NEG_INF) for t in range(2))
        return (tuple(jnp.where(take[t], cand[t], pref[t]) for t in range(2)),
                tuple(jnp.where(take[t], cnt[t], at_thr[t]) for t in range(2)))

    def float_search():
        pref, at_thr = lax.fori_loop(0, 32, bit_step, ((zero_q, zero_q), (zero_q, zero_q)))
        thr = tuple(jnp.where(pref[t] != 0, _radix_candidate(pref[t]), NEG_INF) for t in range(2))
        return pref, at_thr, count2(lambda sc, pick, kt: sc > pick(thr[0], thr[1]))

    at_fast = tuple(jnp.where(g_found[t], g_at[t], 0) for t in range(2))
    g_pref = tuple(jnp.where(g_found[t], guess[t], 0) for t in range(2))
    pref, at_thr, above = lax.cond(confirmed, lambda: (g_pref, at_fast, g_above), float_search)
    found = tuple(pr != 0 for pr in pref)
    thr = tuple(jnp.where(found[t], _radix_candidate(pref[t]), NEG_INF) for t in range(2))

    need = tuple(n_sel - a for a in above)
    any_tie = jnp.max(jnp.maximum(jnp.where(at_thr[0] > n_sel, 1, 0), jnp.where(at_thr[1] > n_sel, 1, 0))) > 0
    pos_bits = max(1, (ikb_ref.shape[0] - 1).bit_length())
    row_k = lax.broadcasted_iota(I32, (KEY_TILE, Q_TILE), 0)

    def last_kept_tie():
        def step(i, pre):
            cand = tuple(pr | jnp.left_shift(jnp.int32(1), pos_bits - 1 - i) for pr in pre)
            cnt = count2(lambda sc, pick, kt: (sc == pick(thr[0], thr[1])) & (kt * KEY_TILE + row_k < pick(cand[0], cand[1])))
            return tuple(jnp.where(cnt[t] < need[t], cand[t], pre[t]) for t in range(2))
        return lax.fori_loop(0, pos_bits, step, (zero_q, zero_q))

    big = jnp.full((1, Q_TILE), INT_MAX, I32)
    last_tie = lax.cond(any_tie, last_kept_tie, lambda: (big, big))
    last_tie = tuple(jnp.where(found[t], last_tie[t], -1) for t in range(2))

    for u in range(n_units):
        is_b, _, kt, _ = unit(u)
        t_u = jnp.where(is_b, thr[1], thr[0])
        last_u = jnp.where(is_b, last_tie[1], last_tie[0])
        sc = keys_ref[u]
        keep = (sc > t_u) | ((sc == t_u) & (kt * KEY_TILE + row_k <= last_u))
        bias_ref[u] = jnp.where(keep, 0.0, MASKED)

    def kept_max():
        cmax = [[neg] * ATT_KV_HEADS, [neg] * ATT_KV_HEADS]
        for u in range(n_units):
            is_b, _, _, _ = unit(u)
            for mc in range(2):
                b_rows = bias_ref[u, mc * half:(mc + 1) * half, :]
                bias_g = jnp.concatenate([b_rows] * group, axis=1)
                for g in range(ATT_KV_HEADS):
                    part = jnp.max((logit_ref[u, mc, g] + bias_g).reshape(half // 8, 8, gw), axis=0)
                    cmax[0][g] = jnp.maximum(cmax[0][g], jnp.where(is_b, neg, part))
                    cmax[1][g] = jnp.maximum(cmax[1][g], jnp.where(is_b, part, neg))
        return cmax

    def attend(cmax):
        shift = [[jnp.max(c, axis=0, keepdims=True) for c in cm] for cm in cmax]
        zsum = jnp.zeros((8, cw), F32)
        lsum = [[[zsum] * ct_per_group for _ in range(ATT_KV_HEADS)] for _ in range(2)]
        for u in range(n_units):
            is_b, _, kt, _ = unit(u)
            for g in range(ATT_KV_HEADS):
                m_g = jnp.where(is_b, shift[1][g], shift[0][g])
                for ct in range(ct_per_group):
                    cols = slice(ct * cw, (ct + 1) * cw)
                    contrib = jnp.zeros((ATT_HEAD_DIM, cw), F32)
                    for mc in range(2):
                        b_rows = bias_ref[u, mc * half:(mc + 1) * half, :]
                        bias_c = jnp.concatenate([b_rows] * (cw // Q_TILE), axis=1)
                        pr = jnp.exp(logit_ref[u, mc, g, :, cols] + bias_c - m_g[:, cols])
                        part = jnp.sum(pr.reshape(half // 8, 8, cw), axis=0)
                        lsum[0][g][ct] = lsum[0][g][ct] + jnp.where(is_b, zsum, part)
                        lsum[1][g][ct] = lsum[1][g][ct] + jnp.where(is_b, part, zsum)
                        v_t = avb_ref[kt, g * ATT_HEAD_DIM:(g + 1) * ATT_HEAD_DIM, mc * half:(mc + 1) * half]
                        contrib = contrib + _dot(v_t, pr.astype(BF16))
                    pv_ref[u, g, :, cols] = contrib

        den_min = jnp.full((1, cw), jnp.inf, F32)
        for g in range(ATT_KV_HEADS):
            for ct in range(ct_per_group):
                cols = slice(ct * cw, (ct + 1) * cw)
                total = pv_ref[0, g, :, cols]
                for u in range(1, n_units):
                    total = total + pv_ref[u, g, :, cols]
                part_a = pv_ref[0, g, :, cols]
                for u in range(1, max_a):
                    part_a = part_a + jnp.where(u < n_a, pv_ref[u, g, :, cols], 0.0)
                for s, (o_ref, acc) in enumerate(((oa_ref, part_a), (ob_ref, total - part_a))):
                    den = jnp.sum(lsum[s][g][ct], axis=0, keepdims=True)
                    den_min = jnp.minimum(den_min, den)
                    out_t = acc / den
                    blk = jnp.concatenate([out_t[:, 0:Q_TILE], out_t[:, Q_TILE:cw]], axis=0)
                    c_out = (g * ct_per_group + ct) * 2 * ATT_HEAD_DIM
                    o_ref[0, 0, :, c_out:c_out + 2 * ATT_HEAD_DIM] = blk.T.astype(o_ref.dtype)
        return jnp.min(den_min)

    den_min = attend(raw_max)

    @pl.when(jnp.logical_not(den_min >= DEN_FLOOR))
    def _():
        attend(kept_max())


def _page_copy(pt_ref, src_hbm, layer, buf, sem, seq, slot, page):
    def copy(p):
        dst = buf.at[(slot,) + (slice(None),) * (len(buf.shape) - 2) + (pl.ds(pl.multiple_of(p * page, page), page),)]
        return pltpu.make_async_copy(src_hbm.at[layer, pt_ref[seq, p]], dst, sem.at[slot])
    return copy


def _start_pages(copy, n_pages):
    def body(p, c):
        copy(p).start()
        return c
    lax.fori_loop(0, n_pages, body, 0, unroll=8 if n_pages % 8 == 0 else 1)


def _wait_slot(buf, sem, slot):
    pltpu.make_async_copy(buf.at[1 - slot], buf.at[slot], sem.at[slot]).wait()


def _sample_scores_kernel(pt_ref, iq_ref, iw_ref, ikn_ref, kidx_hbm, sc_ref, buf, sem, *, layer, n_pages, page, tile):
    b = pl.program_id(0)
    nb = pl.num_programs(0)
    slot = b % 2

    @pl.when(b == 0)
    def _():
        _start_pages(_page_copy(pt_ref, kidx_hbm, layer, buf, sem, 0, 0, page), n_pages)

    @pl.when(b + 1 < nb)
    def _():
        _start_pages(_page_copy(pt_ref, kidx_hbm, layer, buf, sem, b + 1, 1 - slot, page), n_pages)

    _wait_slot(buf, sem, slot)

    iq = iq_ref[0]
    iqb = iq.astype(BF16)
    iw = iw_ref[0]
    past = n_pages * page
    for t in range(past // tile):
        kb = buf[slot, :, t * tile:(t + 1) * tile].astype(BF16)
        d = _dot(iqb, kb)
        sc_ref[0, :, t * tile:(t + 1) * tile] = jnp.sum(jnp.maximum(d, 0.0) * iw, axis=0, keepdims=True)
    d_new = jnp.sum(iq * ikn_ref[0], axis=1, keepdims=True)
    s_new = jnp.sum(jnp.maximum(d_new, 0.0) * iw, axis=0, keepdims=True)
    lane = lax.broadcasted_iota(I32, (1, V7X_LANES), 1)
    sc_ref[0, :, past:past + V7X_LANES] = jnp.where(lane == 0, s_new, 0.0)


def _sample_select_kernel(sc_ref, bias_ref, keys_ref, *, n_sel, past):
    rows, width = sc_ref.shape
    col = lax.broadcasted_iota(I32, (rows, width), 1)
    keys_ref[...] = jnp.where(col <= past, sc_ref[...], NEG_INF)

    def count(pred):
        return jnp.sum(pred(keys_ref[...]).astype(I32), axis=1, keepdims=True)

    def bit_step(i, carry):
        prefix, at_thr = carry
        cand = prefix | jnp.left_shift(jnp.int32(1), 31 - i)
        cf = _radix_candidate(cand)
        cnt = count(lambda sc: sc >= cf)
        take = (cnt >= n_sel) & (cf > NEG_INF)
        return jnp.where(take, cand, prefix), jnp.where(take, cnt, at_thr)

    zero = jnp.zeros((rows, 1), I32)
    prefix, _ = lax.fori_loop(0, 32, bit_step, (zero, zero))
    found = prefix != 0
    thr = jnp.where(found, _radix_candidate(prefix), NEG_INF)
    need = n_sel - count(lambda sc: sc > thr)
    pos_bits = max(1, (width - 1).bit_length())

    def pos_step(i, pre):
        cand = pre | jnp.left_shift(jnp.int32(1), pos_bits - 1 - i)
        cnt = count(lambda sc: (sc == thr) & (col < cand))
        return jnp.where(cnt < need, cand, pre)

    last_tie = jnp.where(found, lax.fori_loop(0, pos_bits, pos_step, zero), -1)
    sc = keys_ref[...]
    keep = (sc > thr) | ((sc == thr) & (col <= last_tie))
    bias_ref[...] = jnp.where(keep, 0.0, MASKED)


def _sample_attention_kernel(pt_ref, aq_ref, gq_ref, akn_ref, av_ref, bias_ref, k_hbm, v_hbm, o_ref,
                             kbuf, vbuf, logit_ref, ksem, vsem, *, layer, n_pages, page, tile):
    b = pl.program_id(0)
    nb = pl.num_programs(0)
    slot = b % 2

    def start(seq, s):
        _start_pages(_page_copy(pt_ref, k_hbm, layer, kbuf, ksem, seq, s, page), n_pages)
        _start_pages(_page_copy(pt_ref, v_hbm, layer, vbuf, vsem, seq, s, page), n_pages)

    @pl.when(b == 0)
    def _():
        start(0, 0)

    @pl.when(b + 1 < nb)
    def _():
        start(b + 1, 1 - slot)

    _wait_slot(kbuf, ksem, slot)
    _wait_slot(vbuf, vsem, slot)

    n_heads = aq_ref.shape[1]
    group = n_heads // ATT_KV_HEADS
    kvw = ATT_KV_HEADS * ATT_HEAD_DIM
    past = n_pages * page
    q = _rms(aq_ref[0], -1) * gq_ref[...]
    zeros = jnp.zeros_like(q)
    head = lax.broadcasted_iota(I32, (n_heads, kvw), 0)
    q_pad = jnp.where(head < group, jnp.concatenate([q, zeros], axis=1), jnp.concatenate([zeros, q], axis=1))
    q_pad_b = q_pad.astype(BF16)

    for t in range(past // tile):
        cols = slice(t * tile, (t + 1) * tile)
        kb = kbuf[slot, :, :, cols].reshape(kvw, tile).astype(BF16)
        logit_ref[:, cols] = _dot(q_pad_b, kb) + bias_ref[0, :, cols]
    lane = lax.broadcasted_iota(I32, (n_heads, V7X_LANES), 1)
    l_new = jnp.sum(q_pad * akn_ref[0], axis=1, keepdims=True) + bias_ref[0, :, past:past + 1]
    logit_ref[:, past:past + V7X_LANES] = jnp.where(lane == 0, l_new, MASKED)

    m = jnp.max(logit_ref[...], axis=1, keepdims=True)
    lsum = jnp.zeros((n_heads, 1), F32)
    acc = jnp.zeros((kvw, n_heads), F32)
    for t in range(past // tile):
        cols = slice(t * tile, (t + 1) * tile)
        p = jnp.exp(logit_ref[:, cols] - m)
        lsum = lsum + jnp.sum(p, axis=1, keepdims=True)
        acc = acc + _dot_nt(vbuf[slot, :, :, cols].reshape(kvw, tile).astype(BF16), p.astype(BF16))
    p_new = jnp.exp(logit_ref[:, past:past + 1] - m)
    acc = acc + _row_to_col(av_ref[0]) * _col_to_row(p_new)
    o_ref[0] = acc / _col_to_row(lsum + p_new)


def _outproj_mlp_kernel(x_ref, atta_ref, attb_ref, rec_ref, wo_ref, g2_ref, wup_ref, wdn_ref, o_ref, *, ff_tile, tiles_per_seq):
    n_t = atta_ref.shape[1]
    att = jnp.concatenate([atta_ref[0, k] for k in range(n_t)], axis=0)
    if tiles_per_seq:
        late = jnp.concatenate([attb_ref[0, n_t - 1 - k] for k in range(n_t)], axis=0)
        att = jnp.where(pl.program_id(0) % tiles_per_seq < tiles_per_seq // 2, att, late)
    mix = jnp.concatenate([att.astype(BF16), rec_ref[...].astype(BF16)], axis=1)
    y = x_ref[...] + _dot(mix, wo_ref[...])
    h2 = (_rms(y, -1) * g2_ref[...]).astype(BF16)
    acc = y
    for c in range(wup_ref.shape[1] // ff_tile):
        u = jnp.maximum(_dot(h2, wup_ref[:, c * ff_tile:(c + 1) * ff_tile]), 0.0)
        acc = acc + _dot((u * u).astype(BF16), wdn_ref[c * ff_tile:(c + 1) * ff_tile, :])
    o_ref[...] = acc


def _row_tile(n_rows, want):
    return want if n_rows % want == 0 else n_rows


def _layer_weights(w_in, att_width, kv_width, idx_width, rec_width):
    o = [0]
    for wdt in (att_width, kv_width, kv_width, idx_width, IDX_DIM, IDX_HEADS, rec_width, rec_width, rec_width, rec_width):
        o.append(o[-1] + wdt)
    aq, ak, av, iq, ik, iw, rq, rf, ri, rg = (w_in[:, o[i]:o[i + 1]] for i in range(10))
    d = w_in.shape[0]
    z = lambda n: jnp.zeros((d, n), w_in.dtype)
    w_main = jnp.concatenate([ak, ik, z(128 - IDX_DIM), rq, rf, ri, rg], axis=1).astype(BF16)
    w_tok = jnp.concatenate([aq, iq, av, ak, ik, iw, z(16 - IDX_HEADS)], axis=1).T.astype(BF16)
    w_qry = jnp.concatenate([aq, iq, av, iw, z(128 - IDX_HEADS)], axis=1).astype(BF16)
    return w_main, w_tok, w_qry


def _outproj_mlp(x, att_a, att_b, rec, w_out, g2, w_up, w_dn, name):
    n, d = x.shape
    tm = _row_tile(n, 512)
    ff = w_up.shape[1]
    width = att_a.shape[-1]
    if att_b is None:
        assert att_a.shape[:3] == (1, 1, n)
        tps, att_b = 0, att_a
        spec_a = spec_b = pl.BlockSpec((1, 1, tm, width), lambda i: (0, 0, i, 0))
    else:
        rows = att_a.shape[2]
        n_t = tm // rows
        tps = att_a.shape[1] * 2 * rows // tm
        assert tm % rows == 0 and tps % 2 == 0 and tps * tm == 2 * att_a.shape[1] * rows
        half = tps // 2
        spec_a = pl.BlockSpec((1, n_t, rows, width), lambda i: (i // tps, jnp.minimum(i % tps, half - 1), 0, 0))
        spec_b = pl.BlockSpec((1, n_t, rows, width), lambda i: (i // tps, jnp.clip(tps - 1 - i % tps, 0, half - 1), 0, 0))
    return pl.pallas_call(
        functools.partial(_outproj_mlp_kernel, ff_tile=min(ff, 1024), tiles_per_seq=tps),
        grid=(n // tm,),
        in_specs=[pl.BlockSpec((tm, d), lambda i: (i, 0)), spec_a, spec_b,
                  pl.BlockSpec((tm, rec.shape[1]), lambda i: (i, 0)),
                  _resident(w_out.shape), _resident(g2.shape), _resident(w_up.shape), _resident(w_dn.shape)],
        out_specs=pl.BlockSpec((tm, d), lambda i: (i, 0)),
        out_shape=jax.ShapeDtypeStruct((n, d), F32),
        compiler_params=_params("arbitrary"),
        name=name,
    )(x, att_a, att_b, rec, w_out, g2, w_up, w_dn)


def _prompt_attention(iqt, aqt, iwt, ikb, akb, avb, bsz, seq):
    n = bsz * seq
    kvw = ATT_KV_HEADS * ATT_HEAD_DIM
    att_width, idx_width = aqt.shape[0], iqt.shape[0]
    n_sel = min(TOPK_MAX, seq // 4)
    nq = seq // Q_TILE
    assert KEY_TILE == 2 * Q_TILE and nq % 2 == 0, "query tiles are paired so every step sees nq/2 + 1 key tiles"
    n_pairs = nq // 2
    n_units = n_pairs + 1
    kt_per_seq = seq // KEY_TILE
    avb_tiles = avb.reshape(kvw, n // KEY_TILE, KEY_TILE).transpose(1, 0, 2)
    n_heads = att_width // ATT_HEAD_DIM
    gw2 = n_heads * Q_TILE
    qa = lambda h: pl.BlockSpec((h, Q_TILE), lambda b, p: (0, b * nq + p))
    qb = lambda h: pl.BlockSpec((h, Q_TILE), lambda b, p: (0, b * nq + nq - 1 - p))
    outb = pl.BlockSpec((1, 1, Q_TILE, att_width), lambda b, p: (b, p, 0, 0))
    return pl.pallas_call(
        functools.partial(_prompt_attention_kernel, n_sel=n_sel, n_units=n_units),
        grid=(bsz, n_pairs),
        in_specs=[qa(idx_width), qb(idx_width), qa(att_width), qb(att_width), qa(IDX_HEADS), qb(IDX_HEADS),
                  pl.BlockSpec((seq, IDX_DIM), lambda b, p: (b, 0)),
                  pl.BlockSpec((seq, kvw), lambda b, p: (b, 0)),
                  pl.BlockSpec((kt_per_seq, kvw, KEY_TILE), lambda b, p: (b, 0, 0))],
        out_specs=[outb, outb],
        out_shape=[jax.ShapeDtypeStruct((bsz, n_pairs, Q_TILE, att_width), BF16)] * 2,
        scratch_shapes=[pltpu.VMEM((2, IDX_DIM, IDX_HEADS * Q_TILE), BF16),
                        pltpu.VMEM((2, ATT_KV_HEADS, kvw, gw2 // ATT_KV_HEADS), BF16),
                        pltpu.VMEM((2, IDX_HEADS, Q_TILE), F32),
                        pltpu.VMEM((n_units, KEY_TILE, Q_TILE), F32),
                        pltpu.VMEM((n_units, 32, 8, Q_TILE), I32),
                        pltpu.VMEM((n_units, KEY_TILE, Q_TILE), F32),
                        pltpu.VMEM((n_units, 2, ATT_KV_HEADS, KEY_TILE // 2, gw2 // ATT_KV_HEADS), F32),
                        pltpu.VMEM((n_units, ATT_KV_HEADS, ATT_HEAD_DIM, gw2 // ATT_KV_HEADS), F32)],
        compiler_params=_params("arbitrary", "arbitrary"),
        name="attention_prompt",
    )(iqt, iqt, aqt, aqt, iwt, iwt, ikb, akb, avb_tiles)


def _prompt_layer(x, lbs, layer, w_main, w_tok, g1, gq_col, gk_row, gi_row, gk_col, gi_col, gn_row, w_out, g2,
                  w_up, w_dn, att_width, idx_width, rec_width, idx_w_scale):
    bsz, seq, d = x.shape
    n = bsz * seq
    xf = x.reshape(n, d)
    tm = _row_tile(seq, 512)
    tps = seq // tm
    kvw = ATT_KV_HEADS * ATT_HEAD_DIM
    row = lambda w: pl.BlockSpec((tm, w), lambda i: (i, 0))
    colb = lambda h: pl.BlockSpec((h, tm), lambda i: (0, i))
    seqb = lambda h: pl.BlockSpec((1, h, tm), lambda i: (i // tps, 0, i % tps))
    akb, ikb, rec4, aqt, iqt, avb, iwt, akt, avt, ikt = pl.pallas_call(
        functools.partial(_inproj_prompt_kernel, idx_w_scale=idx_w_scale),
        grid=(n // tm,),
        in_specs=[row(d), _resident(g1.shape), _resident(w_main.shape), _resident(w_tok.shape),
                  _resident(gk_row.shape), _resident(gi_row.shape), _resident(gq_col.shape),
                  _resident(gk_col.shape), _resident(gi_col.shape)],
        out_specs=[row(kvw), row(IDX_DIM), row(4 * rec_width),
                   colb(att_width), colb(idx_width), colb(kvw), colb(IDX_HEADS),
                   seqb(kvw), seqb(kvw), seqb(IDX_DIM)],
        out_shape=[jax.ShapeDtypeStruct((n, kvw), BF16), jax.ShapeDtypeStruct((n, IDX_DIM), BF16),
                   jax.ShapeDtypeStruct((n, 4 * rec_width), F32),
                   jax.ShapeDtypeStruct((att_width, n), BF16), jax.ShapeDtypeStruct((idx_width, n), BF16),
                   jax.ShapeDtypeStruct((kvw, n), BF16), jax.ShapeDtypeStruct((IDX_HEADS, n), F32),
                   jax.ShapeDtypeStruct((bsz, kvw, seq), F32), jax.ShapeDtypeStruct((bsz, kvw, seq), F32),
                   jax.ShapeDtypeStruct((bsz, IDX_DIM, seq), F32)],
        compiler_params=_params("arbitrary"),
        name="inproj_prompt",
    )(xf, g1, w_main, w_tok, gk_row, gi_row, gq_col, gk_col, gi_col)

    tb = _row_tile(seq, 256)
    n_rec = rec_width // REC_DIM
    tiles = seq // tb
    gs = 2 if bsz % 2 == 0 else 1

    def hgrn(direct):
        return pl.pallas_call(
            functools.partial(_hgrn_prompt_kernel, layer=layer, rec_width=rec_width, direct=direct),
            grid=(bsz // gs, tiles),
            in_specs=[_resident(lbs.shape), pl.BlockSpec((gs, tb, 4 * rec_width), lambda b, t: (b, t, 0)),
                      _resident(gn_row.shape)],
            out_specs=[pl.BlockSpec((gs, tb, rec_width), lambda b, t: (b, t, 0)),
                       pl.BlockSpec((gs, n_rec, REC_DIM, REC_DIM), lambda b, t: (b, 0, 0, 0))],
            out_shape=[jax.ShapeDtypeStruct((bsz, seq, rec_width), BF16),
                       jax.ShapeDtypeStruct((bsz, n_rec, REC_DIM, REC_DIM), F32)],
            scratch_shapes=[pltpu.VMEM((gs, n_rec, REC_DIM, REC_DIM), F32),
                            pltpu.VMEM((CHUNK, rec_width), F32), pltpu.VMEM((CHUNK, rec_width), F32)],
            compiler_params=_params("arbitrary", "arbitrary"),
            name="hgrn_prompt_direct" if direct else "hgrn_prompt",
        )

    lb_min = jnp.min(jnp.cumsum(jax.nn.softmax(lbs, axis=0), axis=0)[layer])
    factorable = lb_min >= float(np.exp(-HGRN_SAFE_SPAN / (CHUNK // 2)))
    rec, s_fin = lax.cond(factorable, hgrn(False), hgrn(True), lbs, rec4.reshape(bsz, seq, 4 * rec_width), gn_row)
    rec = rec.reshape(n, rec_width)

    att_a, att_b = _prompt_attention(iqt, aqt, iwt, ikb, akb, avb, bsz, seq)
    if (seq // tm) % 2:
        att_a = jnp.concatenate([att_a, att_b[:, ::-1]], axis=1).reshape(1, 1, n, att_width)
        att_b = None

    y = _outproj_mlp(xf, att_a, att_b, rec, w_out, g2, w_up, w_dn, "outproj_mlp_prompt")
    to_heads = lambda t: t.reshape(bsz, ATT_KV_HEADS, ATT_HEAD_DIM, seq).transpose(0, 3, 1, 2)
    return y.reshape(bsz, seq, d), to_heads(akt), to_heads(avt), ikt.transpose(0, 2, 1), s_fin


def _sample_layer(x, s0, lbs, layer, cache_k, cache_v, cache_kidx, page_table, w_main, w_qry, g1, gq_row,
                  gk_row, gi_row, gn_row, w_out, g2, w_up, w_dn, att_width, idx_width, rec_width, idx_w_scale):
    bd, t_new, d = x.shape
    assert t_new == 1, "the sample path handles one new token per sequence"
    kvw = ATT_KV_HEADS * ATT_HEAD_DIM
    n_heads = att_width // ATT_HEAD_DIM
    n_rec = rec_width // REC_DIM
    page = cache_k.shape[2]
    n_pages = page_table.shape[1]
    past = n_pages * page
    width = past + V7X_LANES
    tile = 2048 if past % 2048 == 0 else past
    xf = x.reshape(bd, d)

    full = lambda shape: _resident(shape)
    akn, av, ikn, rec4, aq, iq, iw = pl.pallas_call(
        functools.partial(_inproj_sample_kernel, idx_w_scale=idx_w_scale),
        grid=(1,),
        in_specs=[full(xf.shape), full(g1.shape), full(w_main.shape), full(w_qry.shape),
                  full(gk_row.shape), full(gi_row.shape)],
        out_specs=[full((bd, kvw)), full((bd, kvw)), full((bd, IDX_DIM)), full((bd, 4 * rec_width)),
                   full((bd, att_width)), full((bd, idx_width)), full((bd, 128))],
        out_shape=[jax.ShapeDtypeStruct((bd, kvw), F32), jax.ShapeDtypeStruct((bd, kvw), F32),
                   jax.ShapeDtypeStruct((bd, IDX_DIM), F32), jax.ShapeDtypeStruct((bd, 4 * rec_width), F32),
                   jax.ShapeDtypeStruct((bd, att_width), F32), jax.ShapeDtypeStruct((bd, idx_width), F32),
                   jax.ShapeDtypeStruct((bd, 128), F32)],
        compiler_params=_params("arbitrary"),
        name="inproj_sample",
    )(xf, g1, w_main, w_qry, gk_row, gi_row)

    sb = 4 if bd % 4 == 0 else 1
    rec, s1 = pl.pallas_call(
        functools.partial(_hgrn_step_kernel, layer=layer, rec_width=rec_width),
        grid=(bd // sb,),
        in_specs=[_resident(lbs.shape), pl.BlockSpec((sb, 1, 4 * rec_width), lambda b: (b, 0, 0)),
                  _resident(gn_row.shape), pl.BlockSpec((sb, n_rec, REC_DIM, REC_DIM), lambda b: (b, 0, 0, 0))],
        out_specs=[pl.BlockSpec((sb, 1, rec_width), lambda b: (b, 0, 0)),
                   pl.BlockSpec((sb, n_rec, REC_DIM, REC_DIM), lambda b: (b, 0, 0, 0))],
        out_shape=[jax.ShapeDtypeStruct((bd, 1, rec_width), BF16),
                   jax.ShapeDtypeStruct((bd, n_rec, REC_DIM, REC_DIM), F32)],
        compiler_params=_params("arbitrary"),
        name="hgrn_step",
    )(lbs, rec4.reshape(bd, 1, 4 * rec_width), gn_row, s0)

    kidx_t = cache_kidx.transpose(0, 1, 3, 2)
    k_t = cache_k.transpose(0, 1, 3, 4, 2)
    v_t = cache_v.transpose(0, 1, 3, 4, 2)
    iq3 = iq.reshape(bd, IDX_HEADS, IDX_DIM)
    iw3 = iw[:, 0:IDX_HEADS].reshape(bd, IDX_HEADS, 1)
    ikn3 = ikn.reshape(bd, 1, IDX_DIM)
    scores = pl.pallas_call(
        functools.partial(_sample_scores_kernel, layer=layer, n_pages=n_pages, page=page, tile=tile),
        grid_spec=pltpu.PrefetchScalarGridSpec(
            num_scalar_prefetch=1,
            grid=(bd,),
            in_specs=[pl.BlockSpec((1, IDX_HEADS, IDX_DIM), lambda b, pt: (b, 0, 0)),
                      pl.BlockSpec((1, IDX_HEADS, 1), lambda b, pt: (b, 0, 0)),
                      pl.BlockSpec((1, 1, IDX_DIM), lambda b, pt: (b, 0, 0)),
                      pl.BlockSpec(memory_space=pl.ANY)],
            out_specs=pl.BlockSpec((1, 1, width), lambda b, pt: (b, 0, 0)),
            scratch_shapes=[pltpu.VMEM((2, IDX_DIM, past), F32), pltpu.SemaphoreType.DMA((2,))]),
        out_shape=jax.ShapeDtypeStruct((bd, 1, width), F32),
        compiler_params=_params("arbitrary"),
        name="sample_scores",
    )(page_table, iq3, iw3, ikn3, kidx_t)

    n_sel = min(TOPK_MAX, (past + 1) // 4)
    bias = pl.pallas_call(
        functools.partial(_sample_select_kernel, n_sel=n_sel, past=past),
        grid=(1,),
        in_specs=[full((bd, width))],
        out_specs=full((bd, width)),
        out_shape=jax.ShapeDtypeStruct((bd, width), F32),
        scratch_shapes=[pltpu.VMEM((bd, width), F32)],
        compiler_params=_params("arbitrary"),
        name="sample_select",
    )(scores.reshape(bd, width))

    att_t = pl.pallas_call(
        functools.partial(_sample_attention_kernel, layer=layer, n_pages=n_pages, page=page, tile=tile),
        grid_spec=pltpu.PrefetchScalarGridSpec(
            num_scalar_prefetch=1,
            grid=(bd,),
            in_specs=[pl.BlockSpec((1, n_heads, ATT_HEAD_DIM), lambda b, pt: (b, 0, 0)),
                      pl.BlockSpec((1, ATT_HEAD_DIM), lambda b, pt: (0, 0)),
                      pl.BlockSpec((1, 1, kvw), lambda b, pt: (b, 0, 0)),
                      pl.BlockSpec((1, 1, kvw), lambda b, pt: (b, 0, 0)),
                      pl.BlockSpec((1, 1, width), lambda b, pt: (b, 0, 0)),
                      pl.BlockSpec(memory_space=pl.ANY), pl.BlockSpec(memory_space=pl.ANY)],
            out_specs=pl.BlockSpec((1, kvw, n_heads), lambda b, pt: (b, 0, 0)),
            scratch_shapes=[pltpu.VMEM((2, ATT_KV_HEADS, ATT_HEAD_DIM, past), F32),
                            pltpu.VMEM((2, ATT_KV_HEADS, ATT_HEAD_DIM, past), F32),
                            pltpu.VMEM((n_heads, width), F32),
                            pltpu.SemaphoreType.DMA((2,)), pltpu.SemaphoreType.DMA((2,))]),
        out_shape=jax.ShapeDtypeStruct((bd, kvw, n_heads), F32),
        compiler_params=_params("arbitrary"),
        name="sample_attention",
    )(page_table, aq.reshape(bd, n_heads, ATT_HEAD_DIM), gq_row, akn.reshape(bd, 1, kvw), av.reshape(bd, 1, kvw),
      bias.reshape(bd, 1, width), k_t, v_t)
    att_g = att_t.reshape(bd, ATT_KV_HEADS, ATT_HEAD_DIM, n_heads)
    group = n_heads // ATT_KV_HEADS
    att = jnp.stack([att_g[:, h // group, :, h] for h in range(n_heads)], axis=1)

    y = _outproj_mlp(xf, att.reshape(1, 1, bd, att_width), None, rec.reshape(bd, rec_width), w_out, g2, w_up, w_dn,
                     "outproj_mlp_sample")
    return (y.reshape(bd, 1, d), akn.reshape(bd, 1, ATT_KV_HEADS, ATT_HEAD_DIM),
            av.reshape(bd, 1, ATT_KV_HEADS, ATT_HEAD_DIM), ikn.reshape(bd, 1, IDX_DIM), s1)


def kernel(x_prompt, x_sample, cache_k, cache_v, cache_kidx, state_hgrn, page_table, norm1_g, w_in, q_norm_g,
           k_norm_g, idx_k_norm_g, lower_bounds, rec_norm_g, w_out, norm2_g, w_up, w_down):
    depth = w_in.shape[0]
    d_model = x_prompt.shape[-1]
    kv_width = ATT_KV_HEADS * ATT_HEAD_DIM
    idx_width = IDX_HEADS * IDX_DIM
    rec_width = lower_bounds.shape[-1]
    att_width = w_out.shape[1] - rec_width
    assert w_in.shape[-1] == att_width + 2 * kv_width + idx_width + IDX_DIM + IDX_HEADS + 4 * rec_width
    assert rec_norm_g.shape[-1] == REC_DIM and q_norm_g.shape[-1] == ATT_HEAD_DIM
    idx_w_scale = (IDX_HEADS ** -0.5) * (IDX_DIM ** -0.5)
    att_scale = ATT_HEAD_DIM ** -0.5
    n_heads = att_width // ATT_HEAD_DIM
    lbs = lower_bounds.astype(F32)

    yp, ys = x_prompt, x_sample
    outs = [[] for _ in range(8)]
    for l in range(depth):
        w_main, w_tok, w_qry = _layer_weights(w_in[l], att_width, kv_width, idx_width, rec_width)
        g1 = norm1_g[l].reshape(1, d_model)
        g2 = norm2_g[l].reshape(1, d_model)
        gq_row = (q_norm_g[l] * att_scale).reshape(1, ATT_HEAD_DIM)
        gq_col = jnp.tile(q_norm_g[l] * att_scale, n_heads).reshape(att_width, 1)
        gk_row = jnp.tile(k_norm_g[l], ATT_KV_HEADS).reshape(1, kv_width)
        gi_row = idx_k_norm_g[l].reshape(1, IDX_DIM)
        gk_col = gk_row.reshape(kv_width, 1)
        gi_col = gi_row.reshape(IDX_DIM, 1)
        gn_row = rec_norm_g[l].reshape(1, REC_DIM)
        wo, wu, wd = w_out[l].astype(BF16), w_up[l].astype(BF16), w_down[l].astype(BF16)
        yp, kp, vp, ip, sp = _prompt_layer(yp, lbs, l, w_main, w_tok, g1, gq_col, gk_row, gi_row, gk_col, gi_col,
                                           gn_row, wo, g2, wu, wd, att_width, idx_width, rec_width, idx_w_scale)
        ys, k_s, v_s, i_s, s_s = _sample_layer(ys, state_hgrn[l], lbs, l, cache_k, cache_v, cache_kidx,
                                               page_table, w_main, w_qry, g1, gq_row, gk_row, gi_row, gn_row, wo,
                                               g2, wu, wd, att_width, idx_width, rec_width, idx_w_scale)
        for acc, val in zip(outs, (kp, vp, ip, sp, k_s, v_s, i_s, s_s)):
            acc.append(val)
    return (yp, ys) + tuple(jnp.stack(o) for o in outs)
```

```python
import functools

import jax
import jax.numpy as jnp
import numpy as np
from jax import lax
from jax.experimental import pallas as pl
from jax.experimental.pallas import tpu as pltpu

F32 = jnp.float32
BF16 = jnp.bfloat16
I32 = jnp.int32

EPS = 1e-6
MASKED = -1e30
DEN_FLOOR = 1e-17
INT_MIN = -(2 ** 31)
NEG_INF = float("-inf")
LOG2_E = 1.4426950408889634
INT_MAX = 2 ** 31 - 1

ATT_HEAD_DIM = 64
ATT_KV_HEADS = 2
IDX_HEADS = 8
IDX_DIM = 64
REC_DIM = 128
TOPK_MAX = 256
CHUNK = 64
HGRN_SAFE_SPAN = 60.0

V7X_LANES = 128
V7X_VMEM_LIMIT_BYTES = 56 * 1024 * 1024
Q_TILE = 128
KEY_TILE = 256


def _dot(a, b):
    return jnp.dot(a, b, preferred_element_type=F32)


def _dot_nt(a, b):
    return lax.dot_general(a, b, (((1,), (1,)), ((), ())), preferred_element_type=F32)


def _dot_tn(a, b):
    return lax.dot_general(a, b, (((0,), (0,)), ((), ())), preferred_element_type=F32)


def _params(*semantics):
    return pltpu.CompilerParams(dimension_semantics=semantics, vmem_limit_bytes=V7X_VMEM_LIMIT_BYTES)


def _resident(shape):
    nd = len(shape)
    return pl.BlockSpec(shape, lambda *_: (0,) * nd)


def _rms(x, axis):
    return x * lax.rsqrt(jnp.mean(x * x, axis=axis, keepdims=True) + EPS)


def _sigmoid_pair(x):
    t = jnp.exp(-jnp.abs(x))
    r = 1.0 / (1.0 + t)
    tr = t * r
    pos = x >= 0
    return jnp.where(pos, r, tr), jnp.where(pos, tr, r)


def _silu(x):
    s, _ = _sigmoid_pair(x)
    return x * s


def _radix_candidate(prefix_bits):
    key = prefix_bits ^ INT_MIN
    return lax.bitcast_convert_type(key ^ ((key >> 31) & INT_MAX), F32)


def _lower_bound(lbs_ref, layer):
    lbs = lbs_ref[...]
    e = jnp.exp(lbs - jnp.max(lbs, axis=0, keepdims=True))
    p = e / jnp.sum(e, axis=0, keepdims=True)
    return jnp.sum(p[0:layer + 1, :], axis=0, keepdims=True)


def _head_rms_lanes(x, gain_row):
    sq = x * x
    lane = lax.broadcasted_iota(I32, x.shape, 1)
    lo = lane < ATT_HEAD_DIM
    s_lo = jnp.sum(jnp.where(lo, sq, 0.0), axis=-1, keepdims=True)
    s_hi = jnp.sum(jnp.where(lo, 0.0, sq), axis=-1, keepdims=True)
    inv = jnp.where(lo, lax.rsqrt(s_lo / ATT_HEAD_DIM + EPS), lax.rsqrt(s_hi / ATT_HEAD_DIM + EPS))
    return x * inv * gain_row


def _inproj_prompt_kernel(x_ref, g1_ref, wm_ref, wt_ref, gk_ref, gi_ref, gq_ref, gkc_ref, gic_ref,
                          akb_ref, ikb_ref, rec4_ref, aqt_ref, iqt_ref, avb_ref, iwt_ref, akt_ref, avt_ref, ikt_ref,
                          *, idx_w_scale):
    x = x_ref[...]
    hb = (_rms(x, -1) * g1_ref[...]).astype(BF16)
    y = _dot(hb, wm_ref[...])
    akb_ref[...] = _head_rms_lanes(y[:, 0:128], gk_ref[...]).astype(BF16)
    ikb_ref[...] = (_rms(y[:, 128:128 + IDX_DIM], -1) * gi_ref[...]).astype(BF16)
    rec4_ref[...] = y[:, 256:]

    yt = _dot_nt(wt_ref[...], hb)
    n_q = aqt_ref.shape[0] // ATT_HEAD_DIM
    for h in range(n_q):
        rows = slice(h * ATT_HEAD_DIM, (h + 1) * ATT_HEAD_DIM)
        aqt_ref[rows, :] = (_rms(yt[rows, :], 0) * gq_ref[rows, :]).astype(BF16)
    o = aqt_ref.shape[0]
    iqt_ref[...] = yt[o:o + iqt_ref.shape[0], :].astype(BF16)
    o += iqt_ref.shape[0]
    kvw = ATT_KV_HEADS * ATT_HEAD_DIM
    av_t = yt[o:o + kvw, :]
    avt_ref[0] = av_t
    avb_ref[...] = av_t.astype(BF16)
    o += kvw
    for g in range(ATT_KV_HEADS):
        rows = slice(g * ATT_HEAD_DIM, (g + 1) * ATT_HEAD_DIM)
        akt_ref[0, rows, :] = _rms(yt[o + g * ATT_HEAD_DIM:o + (g + 1) * ATT_HEAD_DIM, :], 0) * gkc_ref[rows, :]
    o += kvw
    ikt_ref[0] = _rms(yt[o:o + IDX_DIM, :], 0) * gic_ref[...]
    o += IDX_DIM
    iwt_ref[...] = yt[o:o + IDX_HEADS, :] * idx_w_scale


def _inproj_sample_kernel(x_ref, g1_ref, wm_ref, wq_ref, gk_ref, gi_ref,
                          akn_ref, av_ref, ikn_ref, rec4_ref, aq_ref, iq_ref, iw_ref, *, idx_w_scale):
    x = x_ref[...]
    hb = (_rms(x, -1) * g1_ref[...]).astype(BF16)
    y = _dot(hb, wm_ref[...])
    akn_ref[...] = _head_rms_lanes(y[:, 0:128], gk_ref[...])
    ikn_ref[...] = _rms(y[:, 128:128 + IDX_DIM], -1) * gi_ref[...]
    rec4_ref[...] = y[:, 256:]
    yq = _dot(hb, wq_ref[...])
    wa = aq_ref.shape[1]
    wi = iq_ref.shape[1]
    aq_ref[...] = yq[:, 0:wa]
    iq_ref[...] = yq[:, wa:wa + wi]
    av_ref[...] = yq[:, wa + wi:wa + wi + 128]
    iw_ref[...] = yq[:, wa + wi + 128:wa + wi + 256] * idx_w_scale


def _hgrn_gates(rq, rf, ri, rg, lb):
    sp, sn = _sigmoid_pair(rf)
    q = _silu(rq) * (REC_DIM ** -0.5)
    log_f = jnp.log(lb + (1.0 - lb) * sp)
    k = (1.0 - lb) * sn
    return q, log_f, k, ri, _silu(rg)


def _hgrn_prompt_kernel(lbs_ref, rec4_ref, gn_ref, rec_ref, sfin_ref, st_ref, bsc_ref, ksc_ref, *, layer, rec_width,
                        direct):
    n_heads = rec_width // REC_DIM
    n_seq, tb = rec4_ref.shape[0], rec4_ref.shape[1]

    @pl.when(pl.program_id(1) == 0)
    def _():
        st_ref[...] = jnp.zeros_like(st_ref)

    lb = _lower_bound(lbs_ref, layer)
    w = rec_width
    row = lax.broadcasted_iota(I32, (CHUNK, CHUNK), 0)
    col = lax.broadcasted_iota(I32, (CHUNK, CHUNK), 1)
    causal = row >= col
    tri = jnp.where(causal, 1.0, 0.0).astype(BF16)
    gn = gn_ref[...]
    gates = [_hgrn_gates(rec4_ref[sq, :, 0:w], rec4_ref[sq, :, w:2 * w], rec4_ref[sq, :, 2 * w:3 * w],
                         rec4_ref[sq, :, 3 * w:4 * w], lb) for sq in range(n_seq)]

    for c in range(tb // CHUNK):
        rows = slice(c * CHUNK, (c + 1) * CHUNK)
        for sq in range(n_seq):
            q, log_f, k, v, gate = gates[sq]
            gc = log_f[rows, :]
            g1 = gc.astype(BF16)
            r1 = gc - g1.astype(F32)
            g2 = r1.astype(BF16)
            g3 = (r1 - g2.astype(F32)).astype(BF16)
            b = _dot(tri, g1) + _dot(tri, g2) + _dot(tri, g3)
            b_last = b[CHUNK - 1:CHUNK, :]
            b_mid = b[CHUNK // 2 - 1:CHUNK // 2, :]
            qc, kc, vc = q[rows, :], k[rows, :], v[rows, :]
            heads = [slice(h * REC_DIM, (h + 1) * REC_DIM) for h in range(n_heads)]

            if direct:
                bsc_ref[...] = b
                ksc_ref[...] = kc

                def col_step(s, acc, b=b, qc=qc):
                    b_s = bsc_ref[pl.ds(s, 1), :]
                    wgt = qc * ksc_ref[pl.ds(s, 1), :] * jnp.exp(jnp.minimum(b - b_s, 0.0))
                    hit = causal & (col == s)
                    return tuple(jnp.where(hit, jnp.sum(wgt[:, hs], axis=1, keepdims=True), a_h)
                                 for hs, a_h in zip(heads, acc))

                a_heads = lax.fori_loop(0, CHUNK, col_step, (jnp.zeros((CHUNK, CHUNK), F32),) * n_heads)
            else:
                q_in = (qc * jnp.exp(b - b_mid)).astype(BF16)
                k_in = (kc * jnp.exp(b_mid - b)).astype(BF16)
                a_heads = tuple(jnp.where(causal, _dot_nt(q_in[:, hs], k_in[:, hs]), 0.0) for hs in heads)
            q_st = (qc * jnp.exp(b)).astype(BF16)
            k_st = (kc * jnp.exp(b_last - b)).astype(BF16)
            vb = vc.astype(BF16)
            decay = jnp.exp(b_last)
            for h in range(n_heads):
                cols = heads[h]
                st = st_ref[sq, h]
                o = _dot(a_heads[h].astype(BF16), vb[:, cols]) + _dot_nt(q_st[:, cols], st.astype(BF16))
                st_ref[sq, h] = st * decay[:, cols] + _dot_tn(vb[:, cols], k_st[:, cols])
                o = _rms(o, -1) * gn * gate[rows, cols]
                rec_ref[sq, rows, cols] = o.astype(rec_ref.dtype)

    @pl.when(pl.program_id(1) == pl.num_programs(1) - 1)
    def _():
        for sq in range(n_seq):
            for h in range(n_heads):
                sfin_ref[sq, h] = st_ref[sq, h].T


def _row_to_col(x_row):
    n = x_row.shape[1]
    eye = lax.broadcasted_iota(I32, (n, n), 0) == lax.broadcasted_iota(I32, (n, n), 1)
    return jnp.sum(jnp.where(eye, jnp.broadcast_to(x_row, (n, n)), 0.0), axis=1, keepdims=True)


def _col_to_row(x_col):
    n = x_col.shape[0]
    eye = lax.broadcasted_iota(I32, (n, n), 0) == lax.broadcasted_iota(I32, (n, n), 1)
    return jnp.sum(jnp.where(eye, jnp.broadcast_to(x_col, (n, n)), 0.0), axis=0, keepdims=True)


def _hgrn_step_kernel(lbs_ref, rec4_ref, gn_ref, s0_ref, rec_ref, s1_ref, *, layer, rec_width):
    n_heads = rec_width // REC_DIM
    lb = _lower_bound(lbs_ref, layer)
    w = rec_width
    gn = gn_ref[...]
    for sq in range(rec4_ref.shape[0]):
        r4 = rec4_ref[sq]
        q, log_f, k, v, gate = _hgrn_gates(r4[:, 0:w], r4[:, w:2 * w], r4[:, 2 * w:3 * w], r4[:, 3 * w:4 * w], lb)
        f = jnp.exp(log_f)
        for h in range(n_heads):
            cols = slice(h * REC_DIM, (h + 1) * REC_DIM)
            s1 = _row_to_col(f[:, cols]) * s0_ref[sq, h] + _row_to_col(k[:, cols]) * v[:, cols]
            s1_ref[sq, h] = s1
            o = jnp.sum(_row_to_col(q[:, cols]) * s1, axis=0, keepdims=True)
            rec_ref[sq, :, cols] = (_rms(o, -1) * gn * gate[:, cols]).astype(rec_ref.dtype)


def _prompt_attention_kernel(iqa_ref, iqb_ref, aqa_ref, aqb_ref, iwa_ref, iwb_ref, ikb_ref, akb_ref, avb_ref,
                             oa_ref, ob_ref, iqs_ref, qpad_ref, iws_ref, keys_ref, planes_ref, bias_ref, logit_ref, pv_ref,
                             *, n_sel, n_units):
    p = pl.program_id(1)
    nq = 2 * pl.num_programs(1)
    j_of = (p, nq - 1 - p)
    n_a = p // 2 + 1
    n_heads = aqa_ref.shape[0] // ATT_HEAD_DIM
    group = n_heads // ATT_KV_HEADS
    half = KEY_TILE // 2
    cw = 2 * Q_TILE
    gw = group * Q_TILE
    ct_per_group = gw // cw

    for s, (iq_ref, aq_ref, iw_ref) in enumerate(((iqa_ref, aqa_ref, iwa_ref), (iqb_ref, aqb_ref, iwb_ref))):
        iqs_ref[s] = jnp.concatenate([iq_ref[h * IDX_DIM:(h + 1) * IDX_DIM, :] for h in range(IDX_HEADS)], axis=1)
        halves = [jnp.concatenate([aq_ref[(g * group + hh) * ATT_HEAD_DIM:(g * group + hh + 1) * ATT_HEAD_DIM, :]
                                   for hh in range(group)], axis=1) for g in range(ATT_KV_HEADS)]
        zero = jnp.zeros_like(halves[0])
        for g in range(ATT_KV_HEADS):
            qpad_ref[s, g] = jnp.concatenate([halves[g] if gg == g else zero for gg in range(ATT_KV_HEADS)], axis=0)
        iws_ref[s] = iw_ref[...]

    def unit(u):
        is_b = u >= n_a
        return is_b, jnp.where(is_b, 1, 0), jnp.where(is_b, u - n_a, u), jnp.where(is_b, j_of[1], j_of[0])

    def key_rows(kt, mc):
        return pl.ds(pl.multiple_of(kt * KEY_TILE + mc * half, half), half)

    row_h = lax.broadcasted_iota(I32, (half, Q_TILE), 0)
    lane_h = lax.broadcasted_iota(I32, (half, Q_TILE), 1)

    n_pieces = n_units * 2 * ATT_KV_HEADS

    neg = jnp.full((8, gw), MASKED, F32)
    raw_max = [[neg] * ATT_KV_HEADS, [neg] * ATT_KV_HEADS]

    def qk_piece(piece):
        u, mc, g = piece // (2 * ATT_KV_HEADS), (piece // ATT_KV_HEADS) % 2, piece % ATT_KV_HEADS
        is_b, s, kt, _ = unit(u)
        logit = _dot(akb_ref[key_rows(kt, mc), :], qpad_ref[s, g])
        logit_ref[u, mc, g] = logit
        part = jnp.max(logit.reshape(half // 8, 8, gw), axis=0)
        raw_max[0][g] = jnp.maximum(raw_max[0][g], jnp.where(is_b, neg, part))
        raw_max[1][g] = jnp.maximum(raw_max[1][g], jnp.where(is_b, part, neg))

    srl = lax.shift_right_logical
    for u in range(n_units):
        _, s, kt, j = unit(u)
        iw = iws_ref[s]
        for mc in range(2):
            ik = ikb_ref[key_rows(kt, mc), :]
            sc = jnp.zeros((half, Q_TILE), F32)
            for ct in range(IDX_HEADS // 2):
                d = _dot(ik, iqs_ref[s, :, ct * cw:(ct + 1) * cw])
                for hh in range(2):
                    h = 2 * ct + hh
                    sc = sc + jnp.maximum(d[:, hh * Q_TILE:(hh + 1) * Q_TILE], 0.0) * iw[h:h + 1, :]
            causal = kt * KEY_TILE + mc * half + row_h <= j * Q_TILE + lane_h
            keys_ref[u, mc * half:(mc + 1) * half, :] = jnp.where(causal, sc, NEG_INF)

        bits = lax.bitcast_convert_type(keys_ref[u], I32)
        okey = bits ^ ((bits >> 31) | INT_MIN)
        w = [okey[8 * v:8 * v + 8, :] for v in range(KEY_TILE // 8)]
        span, m = 16, 0x0000FFFF
        while span:
            for k in range(32):
                if not k & span:
                    t = (w[k] ^ srl(w[k + span], jnp.int32(span))) & jnp.int32(m)
                    w[k] = w[k] ^ t
                    w[k + span] = w[k + span] ^ (t << span)
            span >>= 1
            m = (m ^ (m << span)) & 0xFFFFFFFF
        for i in range(32):
            planes_ref[u, i] = w[i]
        for piece in range(u * 2 * ATT_KV_HEADS, (u + 1) * 2 * ATT_KV_HEADS):
            qk_piece(piece)

    zero8 = jnp.zeros((8, Q_TILE), I32)
    zero_q = jnp.zeros((1, Q_TILE), I32)
    max_a = (n_units - 2) // 2 + 1

    def count2(pred):
        tot = [zero8, zero8]
        for u in range(n_units):
            is_b, _, kt, _ = unit(u)
            m = pred(keys_ref[u], lambda a, b_: jnp.where(is_b, b_, a), kt)
            part = jnp.sum(jnp.where(m, 1, 0).reshape(KEY_TILE // 8, 8, Q_TILE), axis=0)
            tot[0] = tot[0] + jnp.where(is_b, zero8, part)
            tot[1] = tot[1] + jnp.where(is_b, part, zero8)
        return tuple(jnp.sum(t, axis=0, keepdims=True) for t in tot)

    def sliced_step(i, carry):
        alive, need, chosen = carry
        ones = [alive[u] & planes_ref[u, i] for u in range(n_units)]
        pcs = [lax.population_count(o) for o in ones]
        total, part_a = pcs[0], pcs[0]
        for u in range(1, n_units):
            total = total + pcs[u]
            if u < max_a:
                part_a = part_a + jnp.where(u < n_a, pcs[u], 0)
        c1 = (jnp.sum(part_a, axis=0, keepdims=True), jnp.sum(total - part_a, axis=0, keepdims=True))
        take = tuple(jnp.where(c1[t] >= need[t], 1, 0) for t in range(2))
        bit = jnp.left_shift(jnp.int32(1), 31 - i)
        new_alive = []
        for u in range(n_units):
            is_b, _, _, _ = unit(u)
            new_alive.append(jnp.where(jnp.where(is_b, take[1], take[0]) != 0, ones[u], alive[u] ^ ones[u]))
        return (tuple(new_alive), tuple(need[t] - (1 - take[t]) * c1[t] for t in range(2)),
                tuple(chosen[t] | (take[t] * bit) for t in range(2)))

    full = jnp.full((8, Q_TILE), -1, I32)
    want = jnp.full((1, Q_TILE), n_sel, I32)
    _, _, guess = lax.fori_loop(0, 32, sliced_step, ((full,) * n_units, (want, want), (zero_q, zero_q)))

    g_thr = tuple(_radix_candidate(gs) for gs in guess)
    g_found = tuple(gt > NEG_INF for gt in g_thr)
    g_at = count2(lambda sc, pick, kt: sc >= pick(g_thr[0], g_thr[1]))
    g_above = count2(lambda sc, pick, kt: sc > pick(g_thr[0], g_thr[1]))
    bad = [jnp.where(g_found[t] & ((g_at[t] < n_sel) | (g_above[t] >= n_sel)), 1, 0) for t in range(2)]
    confirmed = jnp.max(jnp.maximum(bad[0], bad[1])) == 0

    def bit_step(i, carry):
        pref, at_thr = carry
        cand = tuple(pr | jnp.left_shift(jnp.int32(1), 31 - i) for pr in pref)
        cf = tuple(_radix_candidate(c) for c in cand)
        cnt = count2(lambda sc, pick, kt: sc >= pick(cf[0], cf[1]))
        take = tuple((cnt[t] >= n_sel) & (cf[t] > NEG_INF) for t in range(2))
        return (tuple(jnp.where(take[t], cand[t], pref[t]) for t in range(2)),
                tuple(jnp.where(take[t], cnt[t], at_thr[t]) for t in range(2)))

    def float_search():
        pref, at_thr = lax.fori_loop(0, 32, bit_step, ((zero_q, zero_q), (zero_q, zero_q)))
        thr = tuple(jnp.where(pref[t] != 0, _radix_candidate(pref[t]), NEG_INF) for t in range(2))
        return pref, at_thr, count2(lambda sc, pick, kt: sc > pick(thr[0], thr[1]))

    at_fast = tuple(jnp.where(g_found[t], g_at[t], 0) for t in range(2))
    g_pref = tuple(jnp.where(g_found[t], guess[t], 0) for t in range(2))
    pref, at_thr, above = lax.cond(confirmed, lambda: (g_pref, at_fast, g_above), float_search)
    found = tuple(pr != 0 for pr in pref)
    thr = tuple(jnp.where(found[t], _radix_candidate(pref[t]), NEG_INF) for t in range(2))

    need = tuple(n_sel - a for a in above)
    any_tie = jnp.max(jnp.maximum(jnp.where(at_thr[0] > n_sel, 1, 0), jnp.where(at_thr[1] > n_sel, 1, 0))) > 0
    pos_bits = max(1, (ikb_ref.shape[0] - 1).bit_length())
    row_k = lax.broadcasted_iota(I32, (KEY_TILE, Q_TILE), 0)

    def last_kept_tie():
        def step(i, pre):
            cand = tuple(pr | jnp.left_shift(jnp.int32(1), pos_bits - 1 - i) for pr in pre)
            cnt = count2(lambda sc, pick, kt: (sc == pick(thr[0], thr[1])) & (kt * KEY_TILE + row_k < pick(cand[0], cand[1])))
            return tuple(jnp.where(cnt[t] < need[t], cand[t], pre[t]) for t in range(2))
        return lax.fori_loop(0, pos_bits, step, (zero_q, zero_q))

    big = jnp.full((1, Q_TILE), INT_MAX, I32)
    last_tie = lax.cond(any_tie, last_kept_tie, lambda: (big, big))
    last_tie = tuple(jnp.where(found[t], last_tie[t], -1) for t in range(2))

    for u in range(n_units):
        is_b, _, kt, _ = unit(u)
        t_u = jnp.where(is_b, thr[1], thr[0])
        last_u = jnp.where(is_b, last_tie[1], last_tie[0])
        sc = keys_ref[u]
        keep = (sc > t_u) | ((sc == t_u) & (kt * KEY_TILE + row_k <= last_u))
        bias_ref[u] = jnp.where(keep, 0.0, MASKED)

    def kept_max():
        cmax = [[neg] * ATT_KV_HEADS, [neg] * ATT_KV_HEADS]
        for u in range(n_units):
            is_b, _, _, _ = unit(u)
            for mc in range(2):
                b_rows = bias_ref[u, mc * half:(mc + 1) * half, :]
                bias_g = jnp.concatenate([b_rows] * group, axis=1)
                for g in range(ATT_KV_HEADS):
                    part = jnp.max((logit_ref[u, mc, g] + bias_g).reshape(half // 8, 8, gw), axis=0)
                    cmax[0][g] = jnp.maximum(cmax[0][g], jnp.where(is_b, neg, part))
                    cmax[1][g] = jnp.maximum(cmax[1][g], jnp.where(is_b, part, neg))
        return cmax

    def attend(cmax):
        shift = [[jnp.max(c, axis=0, keepdims=True) for c in cm] for cm in cmax]
        zsum = jnp.zeros((8, cw), F32)
        lsum = [[[zsum] * ct_per_group for _ in range(ATT_KV_HEADS)] for _ in range(2)]
        for u in range(n_units):
            is_b, _, kt, _ = unit(u)
            for g in range(ATT_KV_HEADS):
                m_g = jnp.where(is_b, shift[1][g], shift[0][g])
                for ct in range(ct_per_group):
                    cols = slice(ct * cw, (ct + 1) * cw)
                    contrib = jnp.zeros((ATT_HEAD_DIM, cw), F32)
                    for mc in range(2):
                        b_rows = bias_ref[u, mc * half:(mc + 1) * half, :]
                        bias_c = jnp.concatenate([b_rows] * (cw // Q_TILE), axis=1)
                        pr = jnp.exp2(logit_ref[u, mc, g, :, cols] + bias_c - m_g[:, cols])
                        part = jnp.sum(pr.reshape(half // 8, 8, cw), axis=0)
                        lsum[0][g][ct] = lsum[0][g][ct] + jnp.where(is_b, zsum, part)
                        lsum[1][g][ct] = lsum[1][g][ct] + jnp.where(is_b, part, zsum)
                        v_t = avb_ref[kt, g * ATT_HEAD_DIM:(g + 1) * ATT_HEAD_DIM, mc * half:(mc + 1) * half]
                        contrib = contrib + _dot(v_t, pr.astype(BF16))
                    pv_ref[u, g, :, cols] = contrib

        den_min = jnp.full((1, cw), jnp.inf, F32)
        for g in range(ATT_KV_HEADS):
            for ct in range(ct_per_group):
                cols = slice(ct * cw, (ct + 1) * cw)
                total = pv_ref[0, g, :, cols]
                for u in range(1, n_units):
                    total = total + pv_ref[u, g, :, cols]
                part_a = pv_ref[0, g, :, cols]
                for u in range(1, max_a):
                    part_a = part_a + jnp.where(u < n_a, pv_ref[u, g, :, cols], 0.0)
                for s, (o_ref, acc) in enumerate(((oa_ref, part_a), (ob_ref, total - part_a))):
                    den = jnp.sum(lsum[s][g][ct], axis=0, keepdims=True)
                    den_min = jnp.minimum(den_min, den)
                    out_t = acc / den
                    blk = jnp.concatenate([out_t[:, 0:Q_TILE], out_t[:, Q_TILE:cw]], axis=0)
                    c_out = (g * ct_per_group + ct) * 2 * ATT_HEAD_DIM
                    o_ref[0, 0, :, c_out:c_out + 2 * ATT_HEAD_DIM] = blk.T.astype(o_ref.dtype)
        return jnp.min(den_min)

    den_min = attend(raw_max)

    @pl.when(jnp.logical_not(den_min >= DEN_FLOOR))
    def _():
        attend(kept_max())


def _page_copy(pt_ref, src_hbm, layer, buf, sem, seq, slot, page):
    def copy(p):
        dst = buf.at[(slot,) + (slice(None),) * (len(buf.shape) - 2) + (pl.ds(pl.multiple_of(p * page, page), page),)]
        return pltpu.make_async_copy(src_hbm.at[layer, pt_ref[seq, p]], dst, sem.at[slot])
    return copy


def _start_pages(copy, n_pages):
    def body(p, c):
        copy(p).start()
        return c
    lax.fori_loop(0, n_pages, body, 0, unroll=8 if n_pages % 8 == 0 else 1)


def _wait_slot(buf, sem, slot):
    pltpu.make_async_copy(buf.at[1 - slot], buf.at[slot], sem.at[slot]).wait()


def _sample_scores_kernel(pt_ref, iq_ref, iw_ref, ikn_ref, kidx_hbm, sc_ref, buf, sem, *, layer, n_pages, page, tile):
    b = pl.program_id(0)
    nb = pl.num_programs(0)
    slot = b % 2

    @pl.when(b == 0)
    def _():
        _start_pages(_page_copy(pt_ref, kidx_hbm, layer, buf, sem, 0, 0, page), n_pages)

    @pl.when(b + 1 < nb)
    def _():
        _start_pages(_page_copy(pt_ref, kidx_hbm, layer, buf, sem, b + 1, 1 - slot, page), n_pages)

    _wait_slot(buf, sem, slot)

    iq = iq_ref[0]
    iqb = iq.astype(BF16)
    iw = iw_ref[0]
    past = n_pages * page
    for t in range(past // tile):
        kb = buf[slot, :, t * tile:(t + 1) * tile].astype(BF16)
        d = _dot(iqb, kb)
        sc_ref[0, :, t * tile:(t + 1) * tile] = jnp.sum(jnp.maximum(d, 0.0) * iw, axis=0, keepdims=True)
    d_new = jnp.sum(iq * ikn_ref[0], axis=1, keepdims=True)
    s_new = jnp.sum(jnp.maximum(d_new, 0.0) * iw, axis=0, keepdims=True)
    lane = lax.broadcasted_iota(I32, (1, V7X_LANES), 1)
    sc_ref[0, :, past:past + V7X_LANES] = jnp.where(lane == 0, s_new, 0.0)


def _sample_select_kernel(sc_ref, bias_ref, keys_ref, *, n_sel, past):
    rows, width = sc_ref.shape
    col = lax.broadcasted_iota(I32, (rows, width), 1)
    keys_ref[...] = jnp.where(col <= past, sc_ref[...], NEG_INF)

    def count(pred):
        return jnp.sum(pred(keys_ref[...]).astype(I32), axis=1, keepdims=True)

    def bit_step(i, carry):
        prefix, at_thr = carry
        cand = prefix | jnp.left_shift(jnp.int32(1), 31 - i)
        cf = _radix_candidate(cand)
        cnt = count(lambda sc: sc >= cf)
        take = (cnt >= n_sel) & (cf > NEG_INF)
        return jnp.where(take, cand, prefix), jnp.where(take, cnt, at_thr)

    zero = jnp.zeros((rows, 1), I32)
    prefix, _ = lax.fori_loop(0, 32, bit_step, (zero, zero))
    found = prefix != 0
    thr = jnp.where(found, _radix_candidate(prefix), NEG_INF)
    need = n_sel - count(lambda sc: sc > thr)
    pos_bits = max(1, (width - 1).bit_length())

    def pos_step(i, pre):
        cand = pre | jnp.left_shift(jnp.int32(1), pos_bits - 1 - i)
        cnt = count(lambda sc: (sc == thr) & (col < cand))
        return jnp.where(cnt < need, cand, pre)

    last_tie = jnp.where(found, lax.fori_loop(0, pos_bits, pos_step, zero), -1)
    sc = keys_ref[...]
    keep = (sc > thr) | ((sc == thr) & (col <= last_tie))
    bias_ref[...] = jnp.where(keep, 0.0, MASKED)


def _sample_attention_kernel(pt_ref, aq_ref, gq_ref, akn_ref, av_ref, bias_ref, k_hbm, v_hbm, o_ref,
                             kbuf, vbuf, logit_ref, ksem, vsem, *, layer, n_pages, page, tile):
    b = pl.program_id(0)
    nb = pl.num_programs(0)
    slot = b % 2

    def start(seq, s):
        _start_pages(_page_copy(pt_ref, k_hbm, layer, kbuf, ksem, seq, s, page), n_pages)
        _start_pages(_page_copy(pt_ref, v_hbm, layer, vbuf, vsem, seq, s, page), n_pages)

    @pl.when(b == 0)
    def _():
        start(0, 0)

    @pl.when(b + 1 < nb)
    def _():
        start(b + 1, 1 - slot)

    _wait_slot(kbuf, ksem, slot)
    _wait_slot(vbuf, vsem, slot)

    n_heads = aq_ref.shape[1]
    group = n_heads // ATT_KV_HEADS
    kvw = ATT_KV_HEADS * ATT_HEAD_DIM
    past = n_pages * page
    q = _rms(aq_ref[0], -1) * gq_ref[...]
    zeros = jnp.zeros_like(q)
    head = lax.broadcasted_iota(I32, (n_heads, kvw), 0)
    q_pad = jnp.where(head < group, jnp.concatenate([q, zeros], axis=1), jnp.concatenate([zeros, q], axis=1))
    q_pad_b = q_pad.astype(BF16)

    for t in range(past // tile):
        cols = slice(t * tile, (t + 1) * tile)
        kb = kbuf[slot, :, :, cols].reshape(kvw, tile).astype(BF16)
        logit_ref[:, cols] = _dot(q_pad_b, kb) + bias_ref[0, :, cols]
    lane = lax.broadcasted_iota(I32, (n_heads, V7X_LANES), 1)
    l_new = jnp.sum(q_pad * akn_ref[0], axis=1, keepdims=True) + bias_ref[0, :, past:past + 1]
    logit_ref[:, past:past + V7X_LANES] = jnp.where(lane == 0, l_new, MASKED)

    m = jnp.max(logit_ref[...], axis=1, keepdims=True)
    lsum = jnp.zeros((n_heads, 1), F32)
    acc = jnp.zeros((kvw, n_heads), F32)
    for t in range(past // tile):
        cols = slice(t * tile, (t + 1) * tile)
        p = jnp.exp(logit_ref[:, cols] - m)
        lsum = lsum + jnp.sum(p, axis=1, keepdims=True)
        acc = acc + _dot_nt(vbuf[slot, :, :, cols].reshape(kvw, tile).astype(BF16), p.astype(BF16))
    p_new = jnp.exp(logit_ref[:, past:past + 1] - m)
    acc = acc + _row_to_col(av_ref[0]) * _col_to_row(p_new)
    o_ref[0] = acc / _col_to_row(lsum + p_new)


def _outproj_mlp_kernel(x_ref, atta_ref, attb_ref, rec_ref, wo_ref, g2_ref, wup_ref, wdn_ref, o_ref, *, ff_tile, tiles_per_seq):
    n_t = atta_ref.shape[1]
    att = jnp.concatenate([atta_ref[0, k] for k in range(n_t)], axis=0)
    if tiles_per_seq:
        late = jnp.concatenate([attb_ref[0, n_t - 1 - k] for k in range(n_t)], axis=0)
        att = jnp.where(pl.program_id(0) % tiles_per_seq < tiles_per_seq // 2, att, late)
    mix = jnp.concatenate([att.astype(BF16), rec_ref[...].astype(BF16)], axis=1)
    y = x_ref[...] + _dot(mix, wo_ref[...])
    h2 = (_rms(y, -1) * g2_ref[...]).astype(BF16)
    acc = y
    for c in range(wup_ref.shape[1] // ff_tile):
        u = jnp.maximum(_dot(h2, wup_ref[:, c * ff_tile:(c + 1) * ff_tile]), 0.0)
        acc = acc + _dot((u * u).astype(BF16), wdn_ref[c * ff_tile:(c + 1) * ff_tile, :])
    o_ref[...] = acc


def _row_tile(n_rows, want):
    return want if n_rows % want == 0 else n_rows


def _layer_weights(w_in, att_width, kv_width, idx_width, rec_width):
    o = [0]
    for wdt in (att_width, kv_width, kv_width, idx_width, IDX_DIM, IDX_HEADS, rec_width, rec_width, rec_width, rec_width):
        o.append(o[-1] + wdt)
    aq, ak, av, iq, ik, iw, rq, rf, ri, rg = (w_in[:, o[i]:o[i + 1]] for i in range(10))
    d = w_in.shape[0]
    z = lambda n: jnp.zeros((d, n), w_in.dtype)
    w_main = jnp.concatenate([ak, ik, z(128 - IDX_DIM), rq, rf, ri, rg], axis=1).astype(BF16)
    w_tok = jnp.concatenate([aq, iq, av, ak, ik, iw, z(16 - IDX_HEADS)], axis=1).T.astype(BF16)
    w_qry = jnp.concatenate([aq, iq, av, iw, z(128 - IDX_HEADS)], axis=1).astype(BF16)
    return w_main, w_tok, w_qry


def _outproj_mlp(x, att_a, att_b, rec, w_out, g2, w_up, w_dn, name):
    n, d = x.shape
    tm = _row_tile(n, 512)
    ff = w_up.shape[1]
    width = att_a.shape[-1]
    if att_b is None:
        assert att_a.shape[:3] == (1, 1, n)
        tps, att_b = 0, att_a
        spec_a = spec_b = pl.BlockSpec((1, 1, tm, width), lambda i: (0, 0, i, 0))
    else:
        rows = att_a.shape[2]
        n_t = tm // rows
        tps = att_a.shape[1] * 2 * rows // tm
        assert tm % rows == 0 and tps % 2 == 0 and tps * tm == 2 * att_a.shape[1] * rows
        half = tps // 2
        spec_a = pl.BlockSpec((1, n_t, rows, width), lambda i: (i // tps, jnp.minimum(i % tps, half - 1), 0, 0))
        spec_b = pl.BlockSpec((1, n_t, rows, width), lambda i: (i // tps, jnp.clip(tps - 1 - i % tps, 0, half - 1), 0, 0))
    return pl.pallas_call(
        functools.partial(_outproj_mlp_kernel, ff_tile=min(ff, 1024), tiles_per_seq=tps),
        grid=(n // tm,),
        in_specs=[pl.BlockSpec((tm, d), lambda i: (i, 0)), spec_a, spec_b,
                  pl.BlockSpec((tm, rec.shape[1]), lambda i: (i, 0)),
                  _resident(w_out.shape), _resident(g2.shape), _resident(w_up.shape), _resident(w_dn.shape)],
        out_specs=pl.BlockSpec((tm, d), lambda i: (i, 0)),
        out_shape=jax.ShapeDtypeStruct((n, d), F32),
        compiler_params=_params("arbitrary"),
        name=name,
    )(x, att_a, att_b, rec, w_out, g2, w_up, w_dn)


def _prompt_attention(iqt, aqt, iwt, ikb, akb, avb, bsz, seq):
    n = bsz * seq
    kvw = ATT_KV_HEADS * ATT_HEAD_DIM
    att_width, idx_width = aqt.shape[0], iqt.shape[0]
    n_sel = min(TOPK_MAX, seq // 4)
    nq = seq // Q_TILE
    assert KEY_TILE == 2 * Q_TILE and nq % 2 == 0, "query tiles are paired so every step sees nq/2 + 1 key tiles"
    n_pairs = nq // 2
    n_units = n_pairs + 1
    kt_per_seq = seq // KEY_TILE
    avb_tiles = avb.reshape(kvw, n // KEY_TILE, KEY_TILE).transpose(1, 0, 2)
    n_heads = att_width // ATT_HEAD_DIM
    gw2 = n_heads * Q_TILE
    qa = lambda h: pl.BlockSpec((h, Q_TILE), lambda b, p: (0, b * nq + p))
    qb = lambda h: pl.BlockSpec((h, Q_TILE), lambda b, p: (0, b * nq + nq - 1 - p))
    outb = pl.BlockSpec((1, 1, Q_TILE, att_width), lambda b, p: (b, p, 0, 0))
    return pl.pallas_call(
        functools.partial(_prompt_attention_kernel, n_sel=n_sel, n_units=n_units),
        grid=(bsz, n_pairs),
        in_specs=[qa(idx_width), qb(idx_width), qa(att_width), qb(att_width), qa(IDX_HEADS), qb(IDX_HEADS),
                  pl.BlockSpec((seq, IDX_DIM), lambda b, p: (b, 0)),
                  pl.BlockSpec((seq, kvw), lambda b, p: (b, 0)),
                  pl.BlockSpec((kt_per_seq, kvw, KEY_TILE), lambda b, p: (b, 0, 0))],
        out_specs=[outb, outb],
        out_shape=[jax.ShapeDtypeStruct((bsz, n_pairs, Q_TILE, att_width), BF16)] * 2,
        scratch_shapes=[pltpu.VMEM((2, IDX_DIM, IDX_HEADS * Q_TILE), BF16),
                        pltpu.VMEM((2, ATT_KV_HEADS, kvw, gw2 // ATT_KV_HEADS), BF16),
                        pltpu.VMEM((2, IDX_HEADS, Q_TILE), F32),
                        pltpu.VMEM((n_units, KEY_TILE, Q_TILE), F32),
                        pltpu.VMEM((n_units, 32, 8, Q_TILE), I32),
                        pltpu.VMEM((n_units, KEY_TILE, Q_TILE), F32),
                        pltpu.VMEM((n_units, 2, ATT_KV_HEADS, KEY_TILE // 2, gw2 // ATT_KV_HEADS), F32),
                        pltpu.VMEM((n_units, ATT_KV_HEADS, ATT_HEAD_DIM, gw2 // ATT_KV_HEADS), F32)],
        compiler_params=_params("arbitrary", "arbitrary"),
        name="attention_prompt",
    )(iqt, iqt, aqt, aqt, iwt, iwt, ikb, akb, avb_tiles)


def _prompt_layer(x, lbs, layer, w_main, w_tok, g1, gq_col, gk_row, gi_row, gk_col, gi_col, gn_row, w_out, g2,
                  w_up, w_dn, att_width, idx_width, rec_width, idx_w_scale):
    bsz, seq, d = x.shape
    n = bsz * seq
    xf = x.reshape(n, d)
    tm = _row_tile(seq, 512)
    tps = seq // tm
    kvw = ATT_KV_HEADS * ATT_HEAD_DIM
    row = lambda w: pl.BlockSpec((tm, w), lambda i: (i, 0))
    colb = lambda h: pl.BlockSpec((h, tm), lambda i: (0, i))
    seqb = lambda h: pl.BlockSpec((1, h, tm), lambda i: (i // tps, 0, i % tps))
    akb, ikb, rec4, aqt, iqt, avb, iwt, akt, avt, ikt = pl.pallas_call(
        functools.partial(_inproj_prompt_kernel, idx_w_scale=idx_w_scale),
        grid=(n // tm,),
        in_specs=[row(d), _resident(g1.shape), _resident(w_main.shape), _resident(w_tok.shape),
                  _resident(gk_row.shape), _resident(gi_row.shape), _resident(gq_col.shape),
                  _resident(gk_col.shape), _resident(gi_col.shape)],
        out_specs=[row(kvw), row(IDX_DIM), row(4 * rec_width),
                   colb(att_width), colb(idx_width), colb(kvw), colb(IDX_HEADS),
                   seqb(kvw), seqb(kvw), seqb(IDX_DIM)],
        out_shape=[jax.ShapeDtypeStruct((n, kvw), BF16), jax.ShapeDtypeStruct((n, IDX_DIM), BF16),
                   jax.ShapeDtypeStruct((n, 4 * rec_width), F32),
                   jax.ShapeDtypeStruct((att_width, n), BF16), jax.ShapeDtypeStruct((idx_width, n), BF16),
                   jax.ShapeDtypeStruct((kvw, n), BF16), jax.ShapeDtypeStruct((IDX_HEADS, n), F32),
                   jax.ShapeDtypeStruct((bsz, kvw, seq), F32), jax.ShapeDtypeStruct((bsz, kvw, seq), F32),
                   jax.ShapeDtypeStruct((bsz, IDX_DIM, seq), F32)],
        compiler_params=_params("arbitrary"),
        name="inproj_prompt",
    )(xf, g1, w_main, w_tok, gk_row, gi_row, gq_col, gk_col, gi_col)

    tb = _row_tile(seq, 256)
    n_rec = rec_width // REC_DIM
    tiles = seq // tb
    gs = 2 if bsz % 2 == 0 else 1

    def hgrn(direct):
        return pl.pallas_call(
            functools.partial(_hgrn_prompt_kernel, layer=layer, rec_width=rec_width, direct=direct),
            grid=(bsz // gs, tiles),
            in_specs=[_resident(lbs.shape), pl.BlockSpec((gs, tb, 4 * rec_width), lambda b, t: (b, t, 0)),
                      _resident(gn_row.shape)],
            out_specs=[pl.BlockSpec((gs, tb, rec_width), lambda b, t: (b, t, 0)),
                       pl.BlockSpec((gs, n_rec, REC_DIM, REC_DIM), lambda b, t: (b, 0, 0, 0))],
            out_shape=[jax.ShapeDtypeStruct((bsz, seq, rec_width), BF16),
                       jax.ShapeDtypeStruct((bsz, n_rec, REC_DIM, REC_DIM), F32)],
            scratch_shapes=[pltpu.VMEM((gs, n_rec, REC_DIM, REC_DIM), F32),
                            pltpu.VMEM((CHUNK, rec_width), F32), pltpu.VMEM((CHUNK, rec_width), F32)],
            compiler_params=_params("arbitrary", "arbitrary"),
            name="hgrn_prompt_direct" if direct else "hgrn_prompt",
        )

    lb_min = jnp.min(jnp.cumsum(jax.nn.softmax(lbs, axis=0), axis=0)[layer])
    factorable = lb_min >= float(np.exp(-HGRN_SAFE_SPAN / (CHUNK // 2)))
    rec, s_fin = lax.cond(factorable, hgrn(False), hgrn(True), lbs, rec4.reshape(bsz, seq, 4 * rec_width), gn_row)
    rec = rec.reshape(n, rec_width)

    att_a, att_b = _prompt_attention(iqt, aqt, iwt, ikb, akb, avb, bsz, seq)
    if (seq // tm) % 2:
        att_a = jnp.concatenate([att_a, att_b[:, ::-1]], axis=1).reshape(1, 1, n, att_width)
        att_b = None

    y = _outproj_mlp(xf, att_a, att_b, rec, w_out, g2, w_up, w_dn, "outproj_mlp_prompt")
    to_heads = lambda t: t.reshape(bsz, ATT_KV_HEADS, ATT_HEAD_DIM, seq).transpose(0, 3, 1, 2)
    return y.reshape(bsz, seq, d), to_heads(akt), to_heads(avt), ikt.transpose(0, 2, 1), s_fin


def _sample_layer(x, s0, lbs, layer, cache_k, cache_v, cache_kidx, page_table, w_main, w_qry, g1, gq_row,
                  gk_row, gi_row, gn_row, w_out, g2, w_up, w_dn, att_width, idx_width, rec_width, idx_w_scale):
    bd, t_new, d = x.shape
    assert t_new == 1, "the sample path handles one new token per sequence"
    kvw = ATT_KV_HEADS * ATT_HEAD_DIM
    n_heads = att_width // ATT_HEAD_DIM
    n_rec = rec_width // REC_DIM
    page = cache_k.shape[2]
    n_pages = page_table.shape[1]
    past = n_pages * page
    width = past + V7X_LANES
    tile = 2048 if past % 2048 == 0 else past
    xf = x.reshape(bd, d)

    full = lambda shape: _resident(shape)
    akn, av, ikn, rec4, aq, iq, iw = pl.pallas_call(
        functools.partial(_inproj_sample_kernel, idx_w_scale=idx_w_scale),
        grid=(1,),
        in_specs=[full(xf.shape), full(g1.shape), full(w_main.shape), full(w_qry.shape),
                  full(gk_row.shape), full(gi_row.shape)],
        out_specs=[full((bd, kvw)), full((bd, kvw)), full((bd, IDX_DIM)), full((bd, 4 * rec_width)),
                   full((bd, att_width)), full((bd, idx_width)), full((bd, 128))],
        out_shape=[jax.ShapeDtypeStruct((bd, kvw), F32), jax.ShapeDtypeStruct((bd, kvw), F32),
                   jax.ShapeDtypeStruct((bd, IDX_DIM), F32), jax.ShapeDtypeStruct((bd, 4 * rec_width), F32),
                   jax.ShapeDtypeStruct((bd, att_width), F32), jax.ShapeDtypeStruct((bd, idx_width), F32),
                   jax.ShapeDtypeStruct((bd, 128), F32)],
        compiler_params=_params("arbitrary"),
        name="inproj_sample",
    )(xf, g1, w_main, w_qry, gk_row, gi_row)

    sb = 4 if bd % 4 == 0 else 1
    rec, s1 = pl.pallas_call(
        functools.partial(_hgrn_step_kernel, layer=layer, rec_width=rec_width),
        grid=(bd // sb,),
        in_specs=[_resident(lbs.shape), pl.BlockSpec((sb, 1, 4 * rec_width), lambda b: (b, 0, 0)),
                  _resident(gn_row.shape), pl.BlockSpec((sb, n_rec, REC_DIM, REC_DIM), lambda b: (b, 0, 0, 0))],
        out_specs=[pl.BlockSpec((sb, 1, rec_width), lambda b: (b, 0, 0)),
                   pl.BlockSpec((sb, n_rec, REC_DIM, REC_DIM), lambda b: (b, 0, 0, 0))],
        out_shape=[jax.ShapeDtypeStruct((bd, 1, rec_width), BF16),
                   jax.ShapeDtypeStruct((bd, n_rec, REC_DIM, REC_DIM), F32)],
        compiler_params=_params("arbitrary"),
        name="hgrn_step",
    )(lbs, rec4.reshape(bd, 1, 4 * rec_width), gn_row, s0)

    kidx_t = cache_kidx.transpose(0, 1, 3, 2)
    k_t = cache_k.transpose(0, 1, 3, 4, 2)
    v_t = cache_v.transpose(0, 1, 3, 4, 2)
    iq3 = iq.reshape(bd, IDX_HEADS, IDX_DIM)
    iw3 = iw[:, 0:IDX_HEADS].reshape(bd, IDX_HEADS, 1)
    ikn3 = ikn.reshape(bd, 1, IDX_DIM)
    scores = pl.pallas_call(
        functools.partial(_sample_scores_kernel, layer=layer, n_pages=n_pages, page=page, tile=tile),
        grid_spec=pltpu.PrefetchScalarGridSpec(
            num_scalar_prefetch=1,
            grid=(bd,),
            in_specs=[pl.BlockSpec((1, IDX_HEADS, IDX_DIM), lambda b, pt: (b, 0, 0)),
                      pl.BlockSpec((1, IDX_HEADS, 1), lambda b, pt: (b, 0, 0)),
                      pl.BlockSpec((1, 1, IDX_DIM), lambda b, pt: (b, 0, 0)),
                      pl.BlockSpec(memory_space=pl.ANY)],
            out_specs=pl.BlockSpec((1, 1, width), lambda b, pt: (b, 0, 0)),
            scratch_shapes=[pltpu.VMEM((2, IDX_DIM, past), F32), pltpu.SemaphoreType.DMA((2,))]),
        out_shape=jax.ShapeDtypeStruct((bd, 1, width), F32),
        compiler_params=_params("arbitrary"),
        name="sample_scores",
    )(page_table, iq3, iw3, ikn3, kidx_t)

    n_sel = min(TOPK_MAX, (past + 1) // 4)
    bias = pl.pallas_call(
        functools.partial(_sample_select_kernel, n_sel=n_sel, past=past),
        grid=(1,),
        in_specs=[full((bd, width))],
        out_specs=full((bd, width)),
        out_shape=jax.ShapeDtypeStruct((bd, width), F32),
        scratch_shapes=[pltpu.VMEM((bd, width), F32)],
        compiler_params=_params("arbitrary"),
        name="sample_select",
    )(scores.reshape(bd, width))

    att_t = pl.pallas_call(
        functools.partial(_sample_attention_kernel, layer=layer, n_pages=n_pages, page=page, tile=tile),
        grid_spec=pltpu.PrefetchScalarGridSpec(
            num_scalar_prefetch=1,
            grid=(bd,),
            in_specs=[pl.BlockSpec((1, n_heads, ATT_HEAD_DIM), lambda b, pt: (b, 0, 0)),
                      pl.BlockSpec((1, ATT_HEAD_DIM), lambda b, pt: (0, 0)),
                      pl.BlockSpec((1, 1, kvw), lambda b, pt: (b, 0, 0)),
                      pl.BlockSpec((1, 1, kvw), lambda b, pt: (b, 0, 0)),
                      pl.BlockSpec((1, 1, width), lambda b, pt: (b, 0, 0)),
                      pl.BlockSpec(memory_space=pl.ANY), pl.BlockSpec(memory_space=pl.ANY)],
            out_specs=pl.BlockSpec((1, kvw, n_heads), lambda b, pt: (b, 0, 0)),
            scratch_shapes=[pltpu.VMEM((2, ATT_KV_HEADS, ATT_HEAD_DIM, past), F32),
                            pltpu.VMEM((2, ATT_KV_HEADS, ATT_HEAD_DIM, past), F32),
                            pltpu.VMEM((n_heads, width), F32),
                            pltpu.SemaphoreType.DMA((2,)), pltpu.SemaphoreType.DMA((2,))]),
        out_shape=jax.ShapeDtypeStruct((bd, kvw, n_heads), F32),
        compiler_params=_params("arbitrary"),
        name="sample_attention",
    )(page_table, aq.reshape(bd, n_heads, ATT_HEAD_DIM), gq_row, akn.reshape(bd, 1, kvw), av.reshape(bd, 1, kvw),
      bias.reshape(bd, 1, width), k_t, v_t)
    att_g = att_t.reshape(bd, ATT_KV_HEADS, ATT_HEAD_DIM, n_heads)
    group = n_heads // ATT_KV_HEADS
    att = jnp.stack([att_g[:, h // group, :, h] for h in range(n_heads)], axis=1)

    y = _outproj_mlp(xf, att.reshape(1, 1, bd, att_width), None, rec.reshape(bd, rec_width), w_out, g2, w_up, w_dn,
                     "outproj_mlp_sample")
    return (y.reshape(bd, 1, d), akn.reshape(bd, 1, ATT_KV_HEADS, ATT_HEAD_DIM),
            av.reshape(bd, 1, ATT_KV_HEADS, ATT_HEAD_DIM), ikn.reshape(bd, 1, IDX_DIM), s1)


def kernel(x_prompt, x_sample, cache_k, cache_v, cache_kidx, state_hgrn, page_table, norm1_g, w_in, q_norm_g,
           k_norm_g, idx_k_norm_g, lower_bounds, rec_norm_g, w_out, norm2_g, w_up, w_down):
    depth = w_in.shape[0]
    d_model = x_prompt.shape[-1]
    kv_width = ATT_KV_HEADS * ATT_HEAD_DIM
    idx_width = IDX_HEADS * IDX_DIM
    rec_width = lower_bounds.shape[-1]
    att_width = w_out.shape[1] - rec_width
    assert w_in.shape[-1] == att_width + 2 * kv_width + idx_width + IDX_DIM + IDX_HEADS + 4 * rec_width
    assert rec_norm_g.shape[-1] == REC_DIM and q_norm_g.shape[-1] == ATT_HEAD_DIM
    idx_w_scale = (IDX_HEADS ** -0.5) * (IDX_DIM ** -0.5)
    att_scale = ATT_HEAD_DIM ** -0.5
    n_heads = att_width // ATT_HEAD_DIM
    lbs = lower_bounds.astype(F32)

    yp, ys = x_prompt, x_sample
    outs = [[] for _ in range(8)]
    for l in range(depth):
        w_main, w_tok, w_qry = _layer_weights(w_in[l], att_width, kv_width, idx_width, rec_width)
        g1 = norm1_g[l].reshape(1, d_model)
        g2 = norm2_g[l].reshape(1, d_model)
        gq_row = (q_norm_g[l] * att_scale).reshape(1, ATT_HEAD_DIM)
        gq_col = jnp.tile(q_norm_g[l] * (att_scale * LOG2_E), n_heads).reshape(att_width, 1)
        gk_row = jnp.tile(k_norm_g[l], ATT_KV_HEADS).reshape(1, kv_width)
        gi_row = idx_k_norm_g[l].reshape(1, IDX_DIM)
        gk_col = gk_row.reshape(kv_width, 1)
        gi_col = gi_row.reshape(IDX_DIM, 1)
        gn_row = rec_norm_g[l].reshape(1, REC_DIM)
        wo, wu, wd = w_out[l].astype(BF16), w_up[l].astype(BF16), w_down[l].astype(BF16)
        yp, kp, vp, ip, sp = _prompt_layer(yp, lbs, l, w_main, w_tok, g1, gq_col, gk_row, gi_row, gk_col, gi_col,
                                           gn_row, wo, g2, wu, wd, att_width, idx_width, rec_width, idx_w_scale)
        ys, k_s, v_s, i_s, s_s = _sample_layer(ys, state_hgrn[l], lbs, l, cache_k, cache_v, cache_kidx,
                                               page_table, w_main, w_qry, g1, gq_row, gk_row, gi_row, gn_row, wo,
                                               g2, wu, wd, att_width, idx_width, rec_width, idx_w_scale)
        for acc, val in zip(outs, (kp, vp, ip, sp, k_s, v_s, i_s, s_s)):
            acc.append(val)
    return (yp, ys) + tuple(jnp.stack(o) for o in outs)
```

```python
import functools

import jax
import jax.numpy as jnp
import numpy as np
from jax import lax
from jax.experimental import pallas as pl
from jax.experimental.pallas import tpu as pltpu

F32 = jnp.float32
BF16 = jnp.bfloat16
I32 = jnp.int32

EPS = 1e-6
MASKED = -1e30
DEN_FLOOR = 1e-17
INT_MIN = -(2 ** 31)
NEG_INF = float("-inf")
LOG2_E = 1.4426950408889634
INT_MAX = 2 ** 31 - 1

ATT_HEAD_DIM = 64
ATT_KV_HEADS = 2
IDX_HEADS = 8
IDX_DIM = 64
REC_DIM = 128
TOPK_MAX = 256
CHUNK = 64
HGRN_SAFE_SPAN = 60.0

V7X_LANES = 128
V7X_VMEM_LIMIT_BYTES = 56 * 1024 * 1024
Q_TILE = 128
KEY_TILE = 256
SUM_ROWS = 16


def _dot(a, b):
    return jnp.dot(a, b, preferred_element_type=F32)


def _dot_nt(a, b):
    return lax.dot_general(a, b, (((1,), (1,)), ((), ())), preferred_element_type=F32)


def _dot_tn(a, b):
    return lax.dot_general(a, b, (((0,), (0,)), ((), ())), preferred_element_type=F32)


def _params(*semantics):
    return pltpu.CompilerParams(dimension_semantics=semantics, vmem_limit_bytes=V7X_VMEM_LIMIT_BYTES)


def _resident(shape):
    nd = len(shape)
    return pl.BlockSpec(shape, lambda *_: (0,) * nd)


def _rms(x, axis):
    return x * lax.rsqrt(jnp.mean(x * x, axis=axis, keepdims=True) + EPS)


def _sigmoid_pair(x):
    t = jnp.exp(-jnp.abs(x))
    r = 1.0 / (1.0 + t)
    tr = t * r
    pos = x >= 0
    return jnp.where(pos, r, tr), jnp.where(pos, tr, r)


def _silu(x):
    s, _ = _sigmoid_pair(x)
    return x * s


def _radix_candidate(prefix_bits):
    key = prefix_bits ^ INT_MIN
    return lax.bitcast_convert_type(key ^ ((key >> 31) & INT_MAX), F32)


def _lower_bound(lbs_ref, layer):
    lbs = lbs_ref[...]
    e = jnp.exp(lbs - jnp.max(lbs, axis=0, keepdims=True))
    p = e / jnp.sum(e, axis=0, keepdims=True)
    return jnp.sum(p[0:layer + 1, :], axis=0, keepdims=True)


def _head_rms_lanes(x, gain_row):
    sq = x * x
    lane = lax.broadcasted_iota(I32, x.shape, 1)
    lo = lane < ATT_HEAD_DIM
    s_lo = jnp.sum(jnp.where(lo, sq, 0.0), axis=-1, keepdims=True)
    s_hi = jnp.sum(jnp.where(lo, 0.0, sq), axis=-1, keepdims=True)
    inv = jnp.where(lo, lax.rsqrt(s_lo / ATT_HEAD_DIM + EPS), lax.rsqrt(s_hi / ATT_HEAD_DIM + EPS))
    return x * inv * gain_row


def _inproj_prompt_kernel(x_ref, g1_ref, wm_ref, wt_ref, gk_ref, gi_ref, gq_ref, gkc_ref, gic_ref,
                          akb_ref, ikb_ref, rec4_ref, aqt_ref, iqt_ref, avb_ref, iwt_ref, akt_ref, avt_ref, ikt_ref,
                          *, idx_w_scale):
    x = x_ref[...]
    hb = (_rms(x, -1) * g1_ref[...]).astype(BF16)
    y = _dot(hb, wm_ref[...])
    akb_ref[...] = _head_rms_lanes(y[:, 0:128], gk_ref[...]).astype(BF16)
    ikb_ref[...] = (_rms(y[:, 128:128 + IDX_DIM], -1) * gi_ref[...]).astype(BF16)
    rec4_ref[...] = y[:, 256:]

    yt = _dot_nt(wt_ref[...], hb)
    n_q = aqt_ref.shape[0] // ATT_HEAD_DIM
    for h in range(n_q):
        rows = slice(h * ATT_HEAD_DIM, (h + 1) * ATT_HEAD_DIM)
        aqt_ref[rows, :] = (_rms(yt[rows, :], 0) * gq_ref[rows, :]).astype(BF16)
    o = aqt_ref.shape[0]
    iqt_ref[...] = yt[o:o + iqt_ref.shape[0], :].astype(BF16)
    o += iqt_ref.shape[0]
    kvw = ATT_KV_HEADS * ATT_HEAD_DIM
    av_t = yt[o:o + kvw, :]
    avt_ref[0] = av_t
    avb_ref[...] = av_t.astype(BF16)
    o += kvw
    for g in range(ATT_KV_HEADS):
        rows = slice(g * ATT_HEAD_DIM, (g + 1) * ATT_HEAD_DIM)
        akt_ref[0, rows, :] = _rms(yt[o + g * ATT_HEAD_DIM:o + (g + 1) * ATT_HEAD_DIM, :], 0) * gkc_ref[rows, :]
    o += kvw
    ikt_ref[0] = _rms(yt[o:o + IDX_DIM, :], 0) * gic_ref[...]
    o += IDX_DIM
    iwt_ref[...] = yt[o:o + IDX_HEADS, :] * idx_w_scale


def _inproj_sample_kernel(x_ref, g1_ref, wm_ref, wq_ref, gk_ref, gi_ref,
                          akn_ref, av_ref, ikn_ref, rec4_ref, aq_ref, iq_ref, iw_ref, *, idx_w_scale):
    x = x_ref[...]
    hb = (_rms(x, -1) * g1_ref[...]).astype(BF16)
    y = _dot(hb, wm_ref[...])
    akn_ref[...] = _head_rms_lanes(y[:, 0:128], gk_ref[...])
    ikn_ref[...] = _rms(y[:, 128:128 + IDX_DIM], -1) * gi_ref[...]
    rec4_ref[...] = y[:, 256:]
    yq = _dot(hb, wq_ref[...])
    wa = aq_ref.shape[1]
    wi = iq_ref.shape[1]
    aq_ref[...] = yq[:, 0:wa]
    iq_ref[...] = yq[:, wa:wa + wi]
    av_ref[...] = yq[:, wa + wi:wa + wi + 128]
    iw_ref[...] = yq[:, wa + wi + 128:wa + wi + 256] * idx_w_scale


def _hgrn_gates(rq, rf, ri, rg, lb):
    sp, sn = _sigmoid_pair(rf)
    q = _silu(rq) * (REC_DIM ** -0.5)
    log_f = jnp.log(lb + (1.0 - lb) * sp)
    k = (1.0 - lb) * sn
    return q, log_f, k, ri, _silu(rg)


def _hgrn_prompt_kernel(lbs_ref, rec4_ref, gn_ref, rec_ref, sfin_ref, st_ref, bsc_ref, ksc_ref, *, layer, rec_width,
                        direct):
    n_heads = rec_width // REC_DIM
    n_seq, tb = rec4_ref.shape[0], rec4_ref.shape[1]

    @pl.when(pl.program_id(1) == 0)
    def _():
        st_ref[...] = jnp.zeros_like(st_ref)

    lb = _lower_bound(lbs_ref, layer)
    w = rec_width
    row = lax.broadcasted_iota(I32, (CHUNK, CHUNK), 0)
    col = lax.broadcasted_iota(I32, (CHUNK, CHUNK), 1)
    causal = row >= col
    tri = jnp.where(causal, 1.0, 0.0).astype(BF16)
    gn = gn_ref[...]
    gates = [_hgrn_gates(rec4_ref[sq, :, 0:w], rec4_ref[sq, :, w:2 * w], rec4_ref[sq, :, 2 * w:3 * w],
                         rec4_ref[sq, :, 3 * w:4 * w], lb) for sq in range(n_seq)]

    for c in range(tb // CHUNK):
        rows = slice(c * CHUNK, (c + 1) * CHUNK)
        for sq in range(n_seq):
            q, log_f, k, v, gate = gates[sq]
            gc = log_f[rows, :]
            g1 = gc.astype(BF16)
            r1 = gc - g1.astype(F32)
            g2 = r1.astype(BF16)
            g3 = (r1 - g2.astype(F32)).astype(BF16)
            b = _dot(tri, g1) + _dot(tri, g2) + _dot(tri, g3)
            b_last = b[CHUNK - 1:CHUNK, :]
            b_mid = b[CHUNK // 2 - 1:CHUNK // 2, :]
            qc, kc, vc = q[rows, :], k[rows, :], v[rows, :]
            heads = [slice(h * REC_DIM, (h + 1) * REC_DIM) for h in range(n_heads)]

            if direct:
                bsc_ref[...] = b
                ksc_ref[...] = kc

                def col_step(s, acc, b=b, qc=qc):
                    b_s = bsc_ref[pl.ds(s, 1), :]
                    wgt = qc * ksc_ref[pl.ds(s, 1), :] * jnp.exp(jnp.minimum(b - b_s, 0.0))
                    hit = causal & (col == s)
                    return tuple(jnp.where(hit, jnp.sum(wgt[:, hs], axis=1, keepdims=True), a_h)
                                 for hs, a_h in zip(heads, acc))

                a_heads = lax.fori_loop(0, CHUNK, col_step, (jnp.zeros((CHUNK, CHUNK), F32),) * n_heads)
            else:
                q_in = (qc * jnp.exp(b - b_mid)).astype(BF16)
                k_in = (kc * jnp.exp(b_mid - b)).astype(BF16)
                a_heads = tuple(jnp.where(causal, _dot_nt(q_in[:, hs], k_in[:, hs]), 0.0) for hs in heads)
            q_st = (qc * jnp.exp(b)).astype(BF16)
            k_st = (kc * jnp.exp(b_last - b)).astype(BF16)
            vb = vc.astype(BF16)
            decay = jnp.exp(b_last)
            for h in range(n_heads):
                cols = heads[h]
                st = st_ref[sq, h]
                o = _dot(a_heads[h].astype(BF16), vb[:, cols]) + _dot_nt(q_st[:, cols], st.astype(BF16))
                st_ref[sq, h] = st * decay[:, cols] + _dot_tn(vb[:, cols], k_st[:, cols])
                o = _rms(o, -1) * gn * gate[rows, cols]
                rec_ref[sq, rows, cols] = o.astype(rec_ref.dtype)

    @pl.when(pl.program_id(1) == pl.num_programs(1) - 1)
    def _():
        for sq in range(n_seq):
            for h in range(n_heads):
                sfin_ref[sq, h] = st_ref[sq, h].T


def _row_to_col(x_row):
    n = x_row.shape[1]
    eye = lax.broadcasted_iota(I32, (n, n), 0) == lax.broadcasted_iota(I32, (n, n), 1)
    return jnp.sum(jnp.where(eye, jnp.broadcast_to(x_row, (n, n)), 0.0), axis=1, keepdims=True)


def _col_to_row(x_col):
    n = x_col.shape[0]
    eye = lax.broadcasted_iota(I32, (n, n), 0) == lax.broadcasted_iota(I32, (n, n), 1)
    return jnp.sum(jnp.where(eye, jnp.broadcast_to(x_col, (n, n)), 0.0), axis=0, keepdims=True)


def _hgrn_step_kernel(lbs_ref, rec4_ref, gn_ref, s0_ref, rec_ref, s1_ref, *, layer, rec_width):
    n_heads = rec_width // REC_DIM
    lb = _lower_bound(lbs_ref, layer)
    w = rec_width
    gn = gn_ref[...]
    for sq in range(rec4_ref.shape[0]):
        r4 = rec4_ref[sq]
        q, log_f, k, v, gate = _hgrn_gates(r4[:, 0:w], r4[:, w:2 * w], r4[:, 2 * w:3 * w], r4[:, 3 * w:4 * w], lb)
        f = jnp.exp(log_f)
        for h in range(n_heads):
            cols = slice(h * REC_DIM, (h + 1) * REC_DIM)
            s1 = _row_to_col(f[:, cols]) * s0_ref[sq, h] + _row_to_col(k[:, cols]) * v[:, cols]
            s1_ref[sq, h] = s1
            o = jnp.sum(_row_to_col(q[:, cols]) * s1, axis=0, keepdims=True)
            rec_ref[sq, :, cols] = (_rms(o, -1) * gn * gate[:, cols]).astype(rec_ref.dtype)


def _prompt_attention_kernel(iqa_ref, iqb_ref, aqa_ref, aqb_ref, iwa_ref, iwb_ref, ikb_ref, akb_ref, avb_ref,
                             oa_ref, ob_ref, iqs_ref, qpad_ref, iws_ref, keys_ref, planes_ref, bias_ref, logit_ref, pv_ref,
                             *, n_sel, n_units):
    p = pl.program_id(1)
    nq = 2 * pl.num_programs(1)
    j_of = (p, nq - 1 - p)
    n_a = p // 2 + 1
    n_heads = aqa_ref.shape[0] // ATT_HEAD_DIM
    group = n_heads // ATT_KV_HEADS
    half = KEY_TILE // 2
    cw = 2 * Q_TILE
    gw = group * Q_TILE
    ct_per_group = gw // cw

    for s, (iq_ref, aq_ref, iw_ref) in enumerate(((iqa_ref, aqa_ref, iwa_ref), (iqb_ref, aqb_ref, iwb_ref))):
        iqs_ref[s] = jnp.concatenate([iq_ref[h * IDX_DIM:(h + 1) * IDX_DIM, :] for h in range(IDX_HEADS)], axis=1)
        halves = [jnp.concatenate([aq_ref[(g * group + hh) * ATT_HEAD_DIM:(g * group + hh + 1) * ATT_HEAD_DIM, :]
                                   for hh in range(group)], axis=1) for g in range(ATT_KV_HEADS)]
        zero = jnp.zeros_like(halves[0])
        for g in range(ATT_KV_HEADS):
            qpad_ref[s, g] = jnp.concatenate([halves[g] if gg == g else zero for gg in range(ATT_KV_HEADS)], axis=0)
        iws_ref[s] = iw_ref[...]

    def unit(u):
        is_b = u >= n_a
        return is_b, jnp.where(is_b, 1, 0), jnp.where(is_b, u - n_a, u), jnp.where(is_b, j_of[1], j_of[0])

    def key_rows(kt, mc):
        return pl.ds(pl.multiple_of(kt * KEY_TILE + mc * half, half), half)

    row_h = lax.broadcasted_iota(I32, (half, Q_TILE), 0)
    lane_h = lax.broadcasted_iota(I32, (half, Q_TILE), 1)

    n_pieces = n_units * 2 * ATT_KV_HEADS

    neg = jnp.full((8, gw), MASKED, F32)
    raw_max = [[neg] * ATT_KV_HEADS, [neg] * ATT_KV_HEADS]

    def qk_piece(piece):
        u, mc, g = piece // (2 * ATT_KV_HEADS), (piece // ATT_KV_HEADS) % 2, piece % ATT_KV_HEADS
        is_b, s, kt, _ = unit(u)
        logit = _dot(akb_ref[key_rows(kt, mc), :], qpad_ref[s, g])
        logit_ref[u, mc, g] = logit
        part = jnp.max(logit.reshape(half // 8, 8, gw), axis=0)
        raw_max[0][g] = jnp.maximum(raw_max[0][g], jnp.where(is_b, neg, part))
        raw_max[1][g] = jnp.maximum(raw_max[1][g], jnp.where(is_b, part, neg))

    srl = lax.shift_right_logical
    for u in range(n_units):
        _, s, kt, j = unit(u)
        iw = iws_ref[s]
        for mc in range(2):
            ik = ikb_ref[key_rows(kt, mc), :]
            sc = jnp.zeros((half, Q_TILE), F32)
            for ct in range(IDX_HEADS // 2):
                d = _dot(ik, iqs_ref[s, :, ct * cw:(ct + 1) * cw])
                for hh in range(2):
                    h = 2 * ct + hh
                    sc = sc + jnp.maximum(d[:, hh * Q_TILE:(hh + 1) * Q_TILE], 0.0) * iw[h:h + 1, :]
            causal = kt * KEY_TILE + mc * half + row_h <= j * Q_TILE + lane_h
            keys_ref[u, mc * half:(mc + 1) * half, :] = jnp.where(causal, sc, NEG_INF)

        bits = lax.bitcast_convert_type(keys_ref[u], I32)
        okey = bits ^ ((bits >> 31) | INT_MIN)
        w = [okey[8 * v:8 * v + 8, :] for v in range(KEY_TILE // 8)]
        span, m = 16, 0x0000FFFF
        while span:
            for k in range(32):
                if not k & span:
                    t = (w[k] ^ srl(w[k + span], jnp.int32(span))) & jnp.int32(m)
                    w[k] = w[k] ^ t
                    w[k + span] = w[k + span] ^ (t << span)
            span >>= 1
            m = (m ^ (m << span)) & 0xFFFFFFFF
        for i in range(32):
            planes_ref[u, i] = w[i]
        for piece in range(u * 2 * ATT_KV_HEADS, (u + 1) * 2 * ATT_KV_HEADS):
            qk_piece(piece)

    zero8 = jnp.zeros((8, Q_TILE), I32)
    zero_q = jnp.zeros((1, Q_TILE), I32)
    max_a = (n_units - 2) // 2 + 1

    def count2(pred):
        tot = [zero8, zero8]
        for u in range(n_units):
            is_b, _, kt, _ = unit(u)
            m = pred(keys_ref[u], lambda a, b_: jnp.where(is_b, b_, a), kt)
            part = jnp.sum(jnp.where(m, 1, 0).reshape(KEY_TILE // 8, 8, Q_TILE), axis=0)
            tot[0] = tot[0] + jnp.where(is_b, zero8, part)
            tot[1] = tot[1] + jnp.where(is_b, part, zero8)
        return tuple(jnp.sum(t, axis=0, keepdims=True) for t in tot)

    def sliced_step(i, carry):
        alive, need, chosen = carry
        ones = [alive[u] & planes_ref[u, i] for u in range(n_units)]
        pcs = [lax.population_count(o) for o in ones]
        total, part_a = pcs[0], pcs[0]
        for u in range(1, n_units):
            total = total + pcs[u]
            if u < max_a:
                part_a = part_a + jnp.where(u < n_a, pcs[u], 0)
        c1 = (jnp.sum(part_a, axis=0, keepdims=True), jnp.sum(total - part_a, axis=0, keepdims=True))
        take = tuple(jnp.where(c1[t] >= need[t], 1, 0) for t in range(2))
        bit = jnp.left_shift(jnp.int32(1), 31 - i)
        new_alive = []
        for u in range(n_units):
            is_b, _, _, _ = unit(u)
            new_alive.append(jnp.where(jnp.where(is_b, take[1], take[0]) != 0, ones[u], alive[u] ^ ones[u]))
        return (tuple(new_alive), tuple(need[t] - (1 - take[t]) * c1[t] for t in range(2)),
                tuple(chosen[t] | (take[t] * bit) for t in range(2)))

    full = jnp.full((8, Q_TILE), -1, I32)
    want = jnp.full((1, Q_TILE), n_sel, I32)
    _, _, guess = lax.fori_loop(0, 32, sliced_step, ((full,) * n_units, (want, want), (zero_q, zero_q)))

    g_thr = tuple(_radix_candidate(gs) for gs in guess)
    g_found = tuple(gt > NEG_INF for gt in g_thr)
    g_at = count2(lambda sc, pick, kt: sc >= pick(g_thr[0], g_thr[1]))
    g_above = count2(lambda sc, pick, kt: sc > pick(g_thr[0], g_thr[1]))
    bad = [jnp.where(g_found[t] & ((g_at[t] < n_sel) | (g_above[t] >= n_sel)), 1, 0) for t in range(2)]
    confirmed = jnp.max(jnp.maximum(bad[0], bad[1])) == 0

    def bit_step(i, carry):
        pref, at_thr = carry
        cand = tuple(pr | jnp.left_shift(jnp.int32(1), 31 - i) for pr in pref)
        cf = tuple(_radix_candidate(c) for c in cand)
        cnt = count2(lambda sc, pick, kt: sc >= pick(cf[0], cf[1]))
        take = tuple((cnt[t] >= n_sel) & (cf[t] > NEG_INF) for t in range(2))
        return (tuple(jnp.where(take[t], cand[t], pref[t]) for t in range(2)),
                tuple(jnp.where(take[t], cnt[t], at_thr[t]) for t in range(2)))

    def float_search():
        pref, at_thr = lax.fori_loop(0, 32, bit_step, ((zero_q, zero_q), (zero_q, zero_q)))
        thr = tuple(jnp.where(pref[t] != 0, _radix_candidate(pref[t]), NEG_INF) for t in range(2))
        return pref, at_thr, count2(lambda sc, pick, kt: sc > pick(thr[0], thr[1]))

    at_fast = tuple(jnp.where(g_found[t], g_at[t], 0) for t in range(2))
    g_pref = tuple(jnp.where(g_found[t], guess[t], 0) for t in range(2))
    pref, at_thr, above = lax.cond(confirmed, lambda: (g_pref, at_fast, g_above), float_search)
    found = tuple(pr != 0 for pr in pref)
    thr = tuple(jnp.where(found[t], _radix_candidate(pref[t]), NEG_INF) for t in range(2))

    need = tuple(n_sel - a for a in above)
    any_tie = jnp.max(jnp.maximum(jnp.where(at_thr[0] > n_sel, 1, 0), jnp.where(at_thr[1] > n_sel, 1, 0))) > 0
    pos_bits = max(1, (ikb_ref.shape[0] - 1).bit_length())
    row_k = lax.broadcasted_iota(I32, (KEY_TILE, Q_TILE), 0)

    def last_kept_tie():
        def step(i, pre):
            cand = tuple(pr | jnp.left_shift(jnp.int32(1), pos_bits - 1 - i) for pr in pre)
            cnt = count2(lambda sc, pick, kt: (sc == pick(thr[0], thr[1])) & (kt * KEY_TILE + row_k < pick(cand[0], cand[1])))
            return tuple(jnp.where(cnt[t] < need[t], cand[t], pre[t]) for t in range(2))
        return lax.fori_loop(0, pos_bits, step, (zero_q, zero_q))

    big = jnp.full((1, Q_TILE), INT_MAX, I32)
    last_tie = lax.cond(any_tie, last_kept_tie, lambda: (big, big))
    last_tie = tuple(jnp.where(found[t], last_tie[t], -1) for t in range(2))

    for u in range(n_units):
        is_b, _, kt, _ = unit(u)
        t_u = jnp.where(is_b, thr[1], thr[0])
        last_u = jnp.where(is_b, last_tie[1], last_tie[0])
        sc = keys_ref[u]
        keep = (sc > t_u) | ((sc == t_u) & (kt * KEY_TILE + row_k <= last_u))
        bias_ref[u] = jnp.where(keep, 0.0, MASKED)

    def kept_max():
        cmax = [[neg] * ATT_KV_HEADS, [neg] * ATT_KV_HEADS]
        for u in range(n_units):
            is_b, _, _, _ = unit(u)
            for mc in range(2):
                b_rows = bias_ref[u, mc * half:(mc + 1) * half, :]
                bias_g = jnp.concatenate([b_rows] * group, axis=1)
                for g in range(ATT_KV_HEADS):
                    part = jnp.max((logit_ref[u, mc, g] + bias_g).reshape(half // 8, 8, gw), axis=0)
                    cmax[0][g] = jnp.maximum(cmax[0][g], jnp.where(is_b, neg, part))
                    cmax[1][g] = jnp.maximum(cmax[1][g], jnp.where(is_b, part, neg))
        return cmax

    def attend(cmax):
        shift = [[jnp.max(c, axis=0, keepdims=True) for c in cm] for cm in cmax]
        ones_rows = jnp.ones((SUM_ROWS, half), BF16)
        for u in range(n_units):
            is_b, _, kt, _ = unit(u)
            for g in range(ATT_KV_HEADS):
                m_g = jnp.where(is_b, shift[1][g], shift[0][g])
                for ct in range(ct_per_group):
                    cols = slice(ct * cw, (ct + 1) * cw)
                    contrib = jnp.zeros((ATT_HEAD_DIM + SUM_ROWS, cw), F32)
                    for mc in range(2):
                        b_rows = bias_ref[u, mc * half:(mc + 1) * half, :]
                        bias_c = jnp.concatenate([b_rows] * (cw // Q_TILE), axis=1)
                        pr = jnp.exp2(logit_ref[u, mc, g, :, cols] + bias_c - m_g[:, cols])
                        v_t = avb_ref[kt, g * ATT_HEAD_DIM:(g + 1) * ATT_HEAD_DIM, mc * half:(mc + 1) * half]
                        contrib = contrib + _dot(jnp.concatenate([v_t, ones_rows], axis=0), pr.astype(BF16))
                    pv_ref[u, g, :, cols] = contrib

        den_min = jnp.full((1, cw), jnp.inf, F32)
        for g in range(ATT_KV_HEADS):
            for ct in range(ct_per_group):
                cols = slice(ct * cw, (ct + 1) * cw)
                total = pv_ref[0, g, :, cols]
                for u in range(1, n_units):
                    total = total + pv_ref[u, g, :, cols]
                part_a = pv_ref[0, g, :, cols]
                for u in range(1, max_a):
                    part_a = part_a + jnp.where(u < n_a, pv_ref[u, g, :, cols], 0.0)
                for s, (o_ref, acc) in enumerate(((oa_ref, part_a), (ob_ref, total - part_a))):
                    den = acc[ATT_HEAD_DIM:ATT_HEAD_DIM + 1, :]
                    den_min = jnp.minimum(den_min, den)
                    out_t = acc[0:ATT_HEAD_DIM, :] / den
                    blk = jnp.concatenate([out_t[:, 0:Q_TILE], out_t[:, Q_TILE:cw]], axis=0)
                    c_out = (g * ct_per_group + ct) * 2 * ATT_HEAD_DIM
                    o_ref[0, 0, :, c_out:c_out + 2 * ATT_HEAD_DIM] = blk.T.astype(o_ref.dtype)
        return jnp.min(den_min)

    den_min = attend(raw_max)

    @pl.when(jnp.logical_not(den_min >= DEN_FLOOR))
    def _():
        attend(kept_max())


def _page_copy(pt_ref, src_hbm, layer, buf, sem, seq, slot, page):
    def copy(p):
        dst = buf.at[(slot,) + (slice(None),) * (len(buf.shape) - 2) + (pl.ds(pl.multiple_of(p * page, page), page),)]
        return pltpu.make_async_copy(src_hbm.at[layer, pt_ref[seq, p]], dst, sem.at[slot])
    return copy


def _start_pages(copy, n_pages):
    def body(p, c):
        copy(p).start()
        return c
    lax.fori_loop(0, n_pages, body, 0, unroll=8 if n_pages % 8 == 0 else 1)


def _wait_slot(buf, sem, slot):
    pltpu.make_async_copy(buf.at[1 - slot], buf.at[slot], sem.at[slot]).wait()


def _sample_scores_kernel(pt_ref, iq_ref, iw_ref, ikn_ref, kidx_hbm, sc_ref, buf, sem, *, layer, n_pages, page, tile):
    b = pl.program_id(0)
    nb = pl.num_programs(0)
    slot = b % 2

    @pl.when(b == 0)
    def _():
        _start_pages(_page_copy(pt_ref, kidx_hbm, layer, buf, sem, 0, 0, page), n_pages)

    @pl.when(b + 1 < nb)
    def _():
        _start_pages(_page_copy(pt_ref, kidx_hbm, layer, buf, sem, b + 1, 1 - slot, page), n_pages)

    _wait_slot(buf, sem, slot)

    iq = iq_ref[0]
    iqb = iq.astype(BF16)
    iw = iw_ref[0]
    past = n_pages * page
    for t in range(past // tile):
        kb = buf[slot, :, t * tile:(t + 1) * tile].astype(BF16)
        d = _dot(iqb, kb)
        sc_ref[0, :, t * tile:(t + 1) * tile] = jnp.sum(jnp.maximum(d, 0.0) * iw, axis=0, keepdims=True)
    d_new = jnp.sum(iq * ikn_ref[0], axis=1, keepdims=True)
    s_new = jnp.sum(jnp.maximum(d_new, 0.0) * iw, axis=0, keepdims=True)
    lane = lax.broadcasted_iota(I32, (1, V7X_LANES), 1)
    sc_ref[0, :, past:past + V7X_LANES] = jnp.where(lane == 0, s_new, 0.0)


def _sample_select_kernel(sc_ref, bias_ref, keys_ref, *, n_sel, past):
    rows, width = sc_ref.shape
    col = lax.broadcasted_iota(I32, (rows, width), 1)
    keys_ref[...] = jnp.where(col <= past, sc_ref[...], NEG_INF)

    def count(pred):
        return jnp.sum(pred(keys_ref[...]).astype(I32), axis=1, keepdims=True)

    def bit_step(i, carry):
        prefix, at_thr = carry
        cand = prefix | jnp.left_shift(jnp.int32(1), 31 - i)
        cf = _radix_candidate(cand)
        cnt = count(lambda sc: sc >= cf)
        take = (cnt >= n_sel) & (cf > NEG_INF)
        return jnp.where(take, cand, prefix), jnp.where(take, cnt, at_thr)

    zero = jnp.zeros((rows, 1), I32)
    prefix, _ = lax.fori_loop(0, 32, bit_step, (zero, zero))
    found = prefix != 0
    thr = jnp.where(found, _radix_candidate(prefix), NEG_INF)
    need = n_sel - count(lambda sc: sc > thr)
    pos_bits = max(1, (width - 1).bit_length())

    def pos_step(i, pre):
        cand = pre | jnp.left_shift(jnp.int32(1), pos_bits - 1 - i)
        cnt = count(lambda sc: (sc == thr) & (col < cand))
        return jnp.where(cnt < need, cand, pre)

    last_tie = jnp.where(found, lax.fori_loop(0, pos_bits, pos_step, zero), -1)
    sc = keys_ref[...]
    keep = (sc > thr) | ((sc == thr) & (col <= last_tie))
    bias_ref[...] = jnp.where(keep, 0.0, MASKED)


def _sample_attention_kernel(pt_ref, aq_ref, gq_ref, akn_ref, av_ref, bias_ref, k_hbm, v_hbm, o_ref,
                             kbuf, vbuf, logit_ref, ksem, vsem, *, layer, n_pages, page, tile):
    b = pl.program_id(0)
    nb = pl.num_programs(0)
    slot = b % 2

    def start(seq, s):
        _start_pages(_page_copy(pt_ref, k_hbm, layer, kbuf, ksem, seq, s, page), n_pages)
        _start_pages(_page_copy(pt_ref, v_hbm, layer, vbuf, vsem, seq, s, page), n_pages)

    @pl.when(b == 0)
    def _():
        start(0, 0)

    @pl.when(b + 1 < nb)
    def _():
        start(b + 1, 1 - slot)

    _wait_slot(kbuf, ksem, slot)
    _wait_slot(vbuf, vsem, slot)

    n_heads = aq_ref.shape[1]
    group = n_heads // ATT_KV_HEADS
    kvw = ATT_KV_HEADS * ATT_HEAD_DIM
    past = n_pages * page
    q = _rms(aq_ref[0], -1) * gq_ref[...]
    zeros = jnp.zeros_like(q)
    head = lax.broadcasted_iota(I32, (n_heads, kvw), 0)
    q_pad = jnp.where(head < group, jnp.concatenate([q, zeros], axis=1), jnp.concatenate([zeros, q], axis=1))
    q_pad_b = q_pad.astype(BF16)

    for t in range(past // tile):
        cols = slice(t * tile, (t + 1) * tile)
        kb = kbuf[slot, :, :, cols].reshape(kvw, tile).astype(BF16)
        logit_ref[:, cols] = _dot(q_pad_b, kb) + bias_ref[0, :, cols]
    lane = lax.broadcasted_iota(I32, (n_heads, V7X_LANES), 1)
    l_new = jnp.sum(q_pad * akn_ref[0], axis=1, keepdims=True) + bias_ref[0, :, past:past + 1]
    logit_ref[:, past:past + V7X_LANES] = jnp.where(lane == 0, l_new, MASKED)

    m = jnp.max(logit_ref[...], axis=1, keepdims=True)
    lsum = jnp.zeros((n_heads, 1), F32)
    acc = jnp.zeros((kvw, n_heads), F32)
    for t in range(past // tile):
        cols = slice(t * tile, (t + 1) * tile)
        p = jnp.exp(logit_ref[:, cols] - m)
        lsum = lsum + jnp.sum(p, axis=1, keepdims=True)
        acc = acc + _dot_nt(vbuf[slot, :, :, cols].reshape(kvw, tile).astype(BF16), p.astype(BF16))
    p_new = jnp.exp(logit_ref[:, past:past + 1] - m)
    acc = acc + _row_to_col(av_ref[0]) * _col_to_row(p_new)
    o_ref[0] = acc / _col_to_row(lsum + p_new)


def _outproj_mlp_kernel(x_ref, atta_ref, attb_ref, rec_ref, wo_ref, g2_ref, wup_ref, wdn_ref, o_ref, *, ff_tile, tiles_per_seq):
    n_t = atta_ref.shape[1]
    att = jnp.concatenate([atta_ref[0, k] for k in range(n_t)], axis=0)
    if tiles_per_seq:
        late = jnp.concatenate([attb_ref[0, n_t - 1 - k] for k in range(n_t)], axis=0)
        att = jnp.where(pl.program_id(0) % tiles_per_seq < tiles_per_seq // 2, att, late)
    mix = jnp.concatenate([att.astype(BF16), rec_ref[...].astype(BF16)], axis=1)
    y = x_ref[...] + _dot(mix, wo_ref[...])
    h2 = (_rms(y, -1) * g2_ref[...]).astype(BF16)
    acc = y
    for c in range(wup_ref.shape[1] // ff_tile):
        u = jnp.maximum(_dot(h2, wup_ref[:, c * ff_tile:(c + 1) * ff_tile]), 0.0)
        acc = acc + _dot((u * u).astype(BF16), wdn_ref[c * ff_tile:(c + 1) * ff_tile, :])
    o_ref[...] = acc


def _row_tile(n_rows, want):
    return want if n_rows % want == 0 else n_rows


def _layer_weights(w_in, att_width, kv_width, idx_width, rec_width):
    o = [0]
    for wdt in (att_width, kv_width, kv_width, idx_width, IDX_DIM, IDX_HEADS, rec_width, rec_width, rec_width, rec_width):
        o.append(o[-1] + wdt)
    aq, ak, av, iq, ik, iw, rq, rf, ri, rg = (w_in[:, o[i]:o[i + 1]] for i in range(10))
    d = w_in.shape[0]
    z = lambda n: jnp.zeros((d, n), w_in.dtype)
    w_main = jnp.concatenate([ak, ik, z(128 - IDX_DIM), rq, rf, ri, rg], axis=1).astype(BF16)
    w_tok = jnp.concatenate([aq, iq, av, ak, ik, iw, z(16 - IDX_HEADS)], axis=1).T.astype(BF16)
    w_qry = jnp.concatenate([aq, iq, av, iw, z(128 - IDX_HEADS)], axis=1).astype(BF16)
    return w_main, w_tok, w_qry


def _outproj_mlp(x, att_a, att_b, rec, w_out, g2, w_up, w_dn, name):
    n, d = x.shape
    tm = _row_tile(n, 512)
    ff = w_up.shape[1]
    width = att_a.shape[-1]
    if att_b is None:
        assert att_a.shape[:3] == (1, 1, n)
        tps, att_b = 0, att_a
        spec_a = spec_b = pl.BlockSpec((1, 1, tm, width), lambda i: (0, 0, i, 0))
    else:
        rows = att_a.shape[2]
        n_t = tm // rows
        tps = att_a.shape[1] * 2 * rows // tm
        assert tm % rows == 0 and tps % 2 == 0 and tps * tm == 2 * att_a.shape[1] * rows
        half = tps // 2
        spec_a = pl.BlockSpec((1, n_t, rows, width), lambda i: (i // tps, jnp.minimum(i % tps, half - 1), 0, 0))
        spec_b = pl.BlockSpec((1, n_t, rows, width), lambda i: (i // tps, jnp.clip(tps - 1 - i % tps, 0, half - 1), 0, 0))
    return pl.pallas_call(
        functools.partial(_outproj_mlp_kernel, ff_tile=min(ff, 1024), tiles_per_seq=tps),
        grid=(n // tm,),
        in_specs=[pl.BlockSpec((tm, d), lambda i: (i, 0)), spec_a, spec_b,
                  pl.BlockSpec((tm, rec.shape[1]), lambda i: (i, 0)),
                  _resident(w_out.shape), _resident(g2.shape), _resident(w_up.shape), _resident(w_dn.shape)],
        out_specs=pl.BlockSpec((tm, d), lambda i: (i, 0)),
        out_shape=jax.ShapeDtypeStruct((n, d), F32),
        compiler_params=_params("arbitrary"),
        name=name,
    )(x, att_a, att_b, rec, w_out, g2, w_up, w_dn)


def _prompt_attention(iqt, aqt, iwt, ikb, akb, avb, bsz, seq):
    n = bsz * seq
    kvw = ATT_KV_HEADS * ATT_HEAD_DIM
    att_width, idx_width = aqt.shape[0], iqt.shape[0]
    n_sel = min(TOPK_MAX, seq // 4)
    nq = seq // Q_TILE
    assert KEY_TILE == 2 * Q_TILE and nq % 2 == 0, "query tiles are paired so every step sees nq/2 + 1 key tiles"
    n_pairs = nq // 2
    n_units = n_pairs + 1
    kt_per_seq = seq // KEY_TILE
    avb_tiles = avb.reshape(kvw, n // KEY_TILE, KEY_TILE).transpose(1, 0, 2)
    n_heads = att_width // ATT_HEAD_DIM
    gw2 = n_heads * Q_TILE
    qa = lambda h: pl.BlockSpec((h, Q_TILE), lambda b, p: (0, b * nq + p))
    qb = lambda h: pl.BlockSpec((h, Q_TILE), lambda b, p: (0, b * nq + nq - 1 - p))
    outb = pl.BlockSpec((1, 1, Q_TILE, att_width), lambda b, p: (b, p, 0, 0))
    return pl.pallas_call(
        functools.partial(_prompt_attention_kernel, n_sel=n_sel, n_units=n_units),
        grid=(bsz, n_pairs),
        in_specs=[qa(idx_width), qb(idx_width), qa(att_width), qb(att_width), qa(IDX_HEADS), qb(IDX_HEADS),
                  pl.BlockSpec((seq, IDX_DIM), lambda b, p: (b, 0)),
                  pl.BlockSpec((seq, kvw), lambda b, p: (b, 0)),
                  pl.BlockSpec((kt_per_seq, kvw, KEY_TILE), lambda b, p: (b, 0, 0))],
        out_specs=[outb, outb],
        out_shape=[jax.ShapeDtypeStruct((bsz, n_pairs, Q_TILE, att_width), BF16)] * 2,
        scratch_shapes=[pltpu.VMEM((2, IDX_DIM, IDX_HEADS * Q_TILE), BF16),
                        pltpu.VMEM((2, ATT_KV_HEADS, kvw, gw2 // ATT_KV_HEADS), BF16),
                        pltpu.VMEM((2, IDX_HEADS, Q_TILE), F32),
                        pltpu.VMEM((n_units, KEY_TILE, Q_TILE), F32),
                        pltpu.VMEM((n_units, 32, 8, Q_TILE), I32),
                        pltpu.VMEM((n_units, KEY_TILE, Q_TILE), F32),
                        pltpu.VMEM((n_units, 2, ATT_KV_HEADS, KEY_TILE // 2, gw2 // ATT_KV_HEADS), F32),
                        pltpu.VMEM((n_units, ATT_KV_HEADS, ATT_HEAD_DIM + SUM_ROWS, gw2 // ATT_KV_HEADS), F32)],
        compiler_params=_params("arbitrary", "arbitrary"),
        name="attention_prompt",
    )(iqt, iqt, aqt, aqt, iwt, iwt, ikb, akb, avb_tiles)


def _prompt_layer(x, lbs, layer, w_main, w_tok, g1, gq_col, gk_row, gi_row, gk_col, gi_col, gn_row, w_out, g2,
                  w_up, w_dn, att_width, idx_width, rec_width, idx_w_scale):
    bsz, seq, d = x.shape
    n = bsz * seq
    xf = x.reshape(n, d)
    tm = _row_tile(seq, 512)
    tps = seq // tm
    kvw = ATT_KV_HEADS * ATT_HEAD_DIM
    row = lambda w: pl.BlockSpec((tm, w), lambda i: (i, 0))
    colb = lambda h: pl.BlockSpec((h, tm), lambda i: (0, i))
    seqb = lambda h: pl.BlockSpec((1, h, tm), lambda i: (i // tps, 0, i % tps))
    akb, ikb, rec4, aqt, iqt, avb, iwt, akt, avt, ikt = pl.pallas_call(
        functools.partial(_inproj_prompt_kernel, idx_w_scale=idx_w_scale),
        grid=(n // tm,),
        in_specs=[row(d), _resident(g1.shape), _resident(w_main.shape), _resident(w_tok.shape),
                  _resident(gk_row.shape), _resident(gi_row.shape), _resident(gq_col.shape),
                  _resident(gk_col.shape), _resident(gi_col.shape)],
        out_specs=[row(kvw), row(IDX_DIM), row(4 * rec_width),
                   colb(att_width), colb(idx_width), colb(kvw), colb(IDX_HEADS),
                   seqb(kvw), seqb(kvw), seqb(IDX_DIM)],
        out_shape=[jax.ShapeDtypeStruct((n, kvw), BF16), jax.ShapeDtypeStruct((n, IDX_DIM), BF16),
                   jax.ShapeDtypeStruct((n, 4 * rec_width), F32),
                   jax.ShapeDtypeStruct((att_width, n), BF16), jax.ShapeDtypeStruct((idx_width, n), BF16),
                   jax.ShapeDtypeStruct((kvw, n), BF16), jax.ShapeDtypeStruct((IDX_HEADS, n), F32),
                   jax.ShapeDtypeStruct((bsz, kvw, seq), F32), jax.ShapeDtypeStruct((bsz, kvw, seq), F32),
                   jax.ShapeDtypeStruct((bsz, IDX_DIM, seq), F32)],
        compiler_params=_params("arbitrary"),
        name="inproj_prompt",
    )(xf, g1, w_main, w_tok, gk_row, gi_row, gq_col, gk_col, gi_col)

    tb = _row_tile(seq, 256)
    n_rec = rec_width // REC_DIM
    tiles = seq // tb
    gs = 2 if bsz % 2 == 0 else 1

    def hgrn(direct):
        return pl.pallas_call(
            functools.partial(_hgrn_prompt_kernel, layer=layer, rec_width=rec_width, direct=direct),
            grid=(bsz // gs, tiles),
            in_specs=[_resident(lbs.shape), pl.BlockSpec((gs, tb, 4 * rec_width), lambda b, t: (b, t, 0)),
                      _resident(gn_row.shape)],
            out_specs=[pl.BlockSpec((gs, tb, rec_width), lambda b, t: (b, t, 0)),
                       pl.BlockSpec((gs, n_rec, REC_DIM, REC_DIM), lambda b, t: (b, 0, 0, 0))],
            out_shape=[jax.ShapeDtypeStruct((bsz, seq, rec_width), BF16),
                       jax.ShapeDtypeStruct((bsz, n_rec, REC_DIM, REC_DIM), F32)],
            scratch_shapes=[pltpu.VMEM((gs, n_rec, REC_DIM, REC_DIM), F32),
                            pltpu.VMEM((CHUNK, rec_width), F32), pltpu.VMEM((CHUNK, rec_width), F32)],
            compiler_params=_params("arbitrary", "arbitrary"),
            name="hgrn_prompt_direct" if direct else "hgrn_prompt",
        )

    lb_min = jnp.min(jnp.cumsum(jax.nn.softmax(lbs, axis=0), axis=0)[layer])
    factorable = lb_min >= float(np.exp(-HGRN_SAFE_SPAN / (CHUNK // 2)))
    rec, s_fin = lax.cond(factorable, hgrn(False), hgrn(True), lbs, rec4.reshape(bsz, seq, 4 * rec_width), gn_row)
    rec = rec.reshape(n, rec_width)

    att_a, att_b = _prompt_attention(iqt, aqt, iwt, ikb, akb, avb, bsz, seq)
    if (seq // tm) % 2:
        att_a = jnp.concatenate([att_a, att_b[:, ::-1]], axis=1).reshape(1, 1, n, att_width)
        att_b = None

    y = _outproj_mlp(xf, att_a, att_b, rec, w_out, g2, w_up, w_dn, "outproj_mlp_prompt")
    to_heads = lambda t: t.reshape(bsz, ATT_KV_HEADS, ATT_HEAD_DIM, seq).transpose(0, 3, 1, 2)
    return y.reshape(bsz, seq, d), to_heads(akt), to_heads(avt), ikt.transpose(0, 2, 1), s_fin


def _sample_layer(x, s0, lbs, layer, cache_k, cache_v, cache_kidx, page_table, w_main, w_qry, g1, gq_row,
                  gk_row, gi_row, gn_row, w_out, g2, w_up, w_dn, att_width, idx_width, rec_width, idx_w_scale):
    bd, t_new, d = x.shape
    assert t_new == 1, "the sample path handles one new token per sequence"
    kvw = ATT_KV_HEADS * ATT_HEAD_DIM
    n_heads = att_width // ATT_HEAD_DIM
    n_rec = rec_width // REC_DIM
    page = cache_k.shape[2]
    n_pages = page_table.shape[1]
    past = n_pages * page
    width = past + V7X_LANES
    tile = 2048 if past % 2048 == 0 else past
    xf = x.reshape(bd, d)

    full = lambda shape: _resident(shape)
    akn, av, ikn, rec4, aq, iq, iw = pl.pallas_call(
        functools.partial(_inproj_sample_kernel, idx_w_scale=idx_w_scale),
        grid=(1,),
        in_specs=[full(xf.shape), full(g1.shape), full(w_main.shape), full(w_qry.shape),
                  full(gk_row.shape), full(gi_row.shape)],
        out_specs=[full((bd, kvw)), full((bd, kvw)), full((bd, IDX_DIM)), full((bd, 4 * rec_width)),
                   full((bd, att_width)), full((bd, idx_width)), full((bd, 128))],
        out_shape=[jax.ShapeDtypeStruct((bd, kvw), F32), jax.ShapeDtypeStruct((bd, kvw), F32),
                   jax.ShapeDtypeStruct((bd, IDX_DIM), F32), jax.ShapeDtypeStruct((bd, 4 * rec_width), F32),
                   jax.ShapeDtypeStruct((bd, att_width), F32), jax.ShapeDtypeStruct((bd, idx_width), F32),
                   jax.ShapeDtypeStruct((bd, 128), F32)],
        compiler_params=_params("arbitrary"),
        name="inproj_sample",
    )(xf, g1, w_main, w_qry, gk_row, gi_row)

    sb = 4 if bd % 4 == 0 else 1
    rec, s1 = pl.pallas_call(
        functools.partial(_hgrn_step_kernel, layer=layer, rec_width=rec_width),
        grid=(bd // sb,),
        in_specs=[_resident(lbs.shape), pl.BlockSpec((sb, 1, 4 * rec_width), lambda b: (b, 0, 0)),
                  _resident(gn_row.shape), pl.BlockSpec((sb, n_rec, REC_DIM, REC_DIM), lambda b: (b, 0, 0, 0))],
        out_specs=[pl.BlockSpec((sb, 1, rec_width), lambda b: (b, 0, 0)),
                   pl.BlockSpec((sb, n_rec, REC_DIM, REC_DIM), lambda b: (b, 0, 0, 0))],
        out_shape=[jax.ShapeDtypeStruct((bd, 1, rec_width), BF16),
                   jax.ShapeDtypeStruct((bd, n_rec, REC_DIM, REC_DIM), F32)],
        compiler_params=_params("arbitrary"),
        name="hgrn_step",
    )(lbs, rec4.reshape(bd, 1, 4 * rec_width), gn_row, s0)

    kidx_t = cache_kidx.transpose(0, 1, 3, 2)
    k_t = cache_k.transpose(0, 1, 3, 4, 2)
    v_t = cache_v.transpose(0, 1, 3, 4, 2)
    iq3 = iq.reshape(bd, IDX_HEADS, IDX_DIM)
    iw3 = iw[:, 0:IDX_HEADS].reshape(bd, IDX_HEADS, 1)
    ikn3 = ikn.reshape(bd, 1, IDX_DIM)
    scores = pl.pallas_call(
        functools.partial(_sample_scores_kernel, layer=layer, n_pages=n_pages, page=page, tile=tile),
        grid_spec=pltpu.PrefetchScalarGridSpec(
            num_scalar_prefetch=1,
            grid=(bd,),
            in_specs=[pl.BlockSpec((1, IDX_HEADS, IDX_DIM), lambda b, pt: (b, 0, 0)),
                      pl.BlockSpec((1, IDX_HEADS, 1), lambda b, pt: (b, 0, 0)),
                      pl.BlockSpec((1, 1, IDX_DIM), lambda b, pt: (b, 0, 0)),
                      pl.BlockSpec(memory_space=pl.ANY)],
            out_specs=pl.BlockSpec((1, 1, width), lambda b, pt: (b, 0, 0)),
            scratch_shapes=[pltpu.VMEM((2, IDX_DIM, past), F32), pltpu.SemaphoreType.DMA((2,))]),
        out_shape=jax.ShapeDtypeStruct((bd, 1, width), F32),
        compiler_params=_params("arbitrary"),
        name="sample_scores",
    )(page_table, iq3, iw3, ikn3, kidx_t)

    n_sel = min(TOPK_MAX, (past + 1) // 4)
    bias = pl.pallas_call(
        functools.partial(_sample_select_kernel, n_sel=n_sel, past=past),
        grid=(1,),
        in_specs=[full((bd, width))],
        out_specs=full((bd, width)),
        out_shape=jax.ShapeDtypeStruct((bd, width), F32),
        scratch_shapes=[pltpu.VMEM((bd, width), F32)],
        compiler_params=_params("arbitrary"),
        name="sample_select",
    )(scores.reshape(bd, width))

    att_t = pl.pallas_call(
        functools.partial(_sample_attention_kernel, layer=layer, n_pages=n_pages, page=page, tile=tile),
        grid_spec=pltpu.PrefetchScalarGridSpec(
            num_scalar_prefetch=1,
            grid=(bd,),
            in_specs=[pl.BlockSpec((1, n_heads, ATT_HEAD_DIM), lambda b, pt: (b, 0, 0)),
                      pl.BlockSpec((1, ATT_HEAD_DIM), lambda b, pt: (0, 0)),
                      pl.BlockSpec((1, 1, kvw), lambda b, pt: (b, 0, 0)),
                      pl.BlockSpec((1, 1, kvw), lambda b, pt: (b, 0, 0)),
                      pl.BlockSpec((1, 1, width), lambda b, pt: (b, 0, 0)),
                      pl.BlockSpec(memory_space=pl.ANY), pl.BlockSpec(memory_space=pl.ANY)],
            out_specs=pl.BlockSpec((1, kvw, n_heads), lambda b, pt: (b, 0, 0)),
            scratch_shapes=[pltpu.VMEM((2, ATT_KV_HEADS, ATT_HEAD_DIM, past), F32),
                            pltpu.VMEM((2, ATT_KV_HEADS, ATT_HEAD_DIM, past), F32),
                            pltpu.VMEM((n_heads, width), F32),
                            pltpu.SemaphoreType.DMA((2,)), pltpu.SemaphoreType.DMA((2,))]),
        out_shape=jax.ShapeDtypeStruct((bd, kvw, n_heads), F32),
        compiler_params=_params("arbitrary"),
        name="sample_attention",
    )(page_table, aq.reshape(bd, n_heads, ATT_HEAD_DIM), gq_row, akn.reshape(bd, 1, kvw), av.reshape(bd, 1, kvw),
      bias.reshape(bd, 1, width), k_t, v_t)
    att_g = att_t.reshape(bd, ATT_KV_HEADS, ATT_HEAD_DIM, n_heads)
    group = n_heads // ATT_KV_HEADS
    att = jnp.stack([att_g[:, h // group, :, h] for h in range(n_heads)], axis=1)

    y = _outproj_mlp(xf, att.reshape(1, 1, bd, att_width), None, rec.reshape(bd, rec_width), w_out, g2, w_up, w_dn,
                     "outproj_mlp_sample")
    return (y.reshape(bd, 1, d), akn.reshape(bd, 1, ATT_KV_HEADS, ATT_HEAD_DIM),
            av.reshape(bd, 1, ATT_KV_HEADS, ATT_HEAD_DIM), ikn.reshape(bd, 1, IDX_DIM), s1)


def kernel(x_prompt, x_sample, cache_k, cache_v, cache_kidx, state_hgrn, page_table, norm1_g, w_in, q_norm_g,
           k_norm_g, idx_k_norm_g, lower_bounds, rec_norm_g, w_out, norm2_g, w_up, w_down):
    depth = w_in.shape[0]
    d_model = x_prompt.shape[-1]
    kv_width = ATT_KV_HEADS * ATT_HEAD_DIM
    idx_width = IDX_HEADS * IDX_DIM
    rec_width = lower_bounds.shape[-1]
    att_width = w_out.shape[1] - rec_width
    assert w_in.shape[-1] == att_width + 2 * kv_width + idx_width + IDX_DIM + IDX_HEADS + 4 * rec_width
    assert rec_norm_g.shape[-1] == REC_DIM and q_norm_g.shape[-1] == ATT_HEAD_DIM
    idx_w_scale = (IDX_HEADS ** -0.5) * (IDX_DIM ** -0.5)
    att_scale = ATT_HEAD_DIM ** -0.5
    n_heads = att_width // ATT_HEAD_DIM
    lbs = lower_bounds.astype(F32)

    yp, ys = x_prompt, x_sample
    outs = [[] for _ in range(8)]
    for l in range(depth):
        w_main, w_tok, w_qry = _layer_weights(w_in[l], att_width, kv_width, idx_width, rec_width)
        g1 = norm1_g[l].reshape(1, d_model)
        g2 = norm2_g[l].reshape(1, d_model)
        gq_row = (q_norm_g[l] * att_scale).reshape(1, ATT_HEAD_DIM)
        gq_col = jnp.tile(q_norm_g[l] * (att_scale * LOG2_E), n_heads).reshape(att_width, 1)
        gk_row = jnp.tile(k_norm_g[l], ATT_KV_HEADS).reshape(1, kv_width)
        gi_row = idx_k_norm_g[l].reshape(1, IDX_DIM)
        gk_col = gk_row.reshape(kv_width, 1)
        gi_col = gi_row.reshape(IDX_DIM, 1)
        gn_row = rec_norm_g[l].reshape(1, REC_DIM)
        wo, wu, wd = w_out[l].astype(BF16), w_up[l].astype(BF16), w_down[l].astype(BF16)
        yp, kp, vp, ip, sp = _prompt_layer(yp, lbs, l, w_main, w_tok, g1, gq_col, gk_row, gi_row, gk_col, gi_col,
                                           gn_row, wo, g2, wu, wd, att_width, idx_width, rec_width, idx_w_scale)
        ys, k_s, v_s, i_s, s_s = _sample_layer(ys, state_hgrn[l], lbs, l, cache_k, cache_v, cache_kidx,
                                               page_table, w_main, w_qry, g1, gq_row, gk_row, gi_row, gn_row, wo,
                                               g2, wu, wd, att_width, idx_width, rec_width, idx_w_scale)
        for acc, val in zip(outs, (kp, vp, ip, sp, k_s, v_s, i_s, s_s)):
            acc.append(val)
    return (yp, ys) + tuple(jnp.stack(o) for o in outs)
```
